```python
import jax, jax.numpy as jnp
from jax import lax
import numpy as np

D_MODEL = 2048
BATCH = 1
SEQ = 8192
DEPTH = 4

CHUNK = 64
N_META = 16
Q_BLOCK = 128
MLA_HEADS = 8
MLA_Q_LORA = 512
MLA_KV_LORA = 512
MLA_NOPE = 128
MLA_ROPE = 64
MLA_V = 128
ROPE_THETA = 10000.0
FOX_HEADS = 8
FOX_HD = 128
FOX_W = FOX_HEADS * FOX_HD
FORGET_BIAS = 3.0
MIX_WIDTH = MLA_HEADS * MLA_V + FOX_W
D_FF = 5632
CONV_K = 3
EPS = 1e-6
NEG = -1e30
IN_SPLIT_SIZES = (MLA_Q_LORA, MLA_KV_LORA, MLA_ROPE, FOX_W, FOX_W, FOX_W, FOX_W, FOX_HEADS)
IN_COLS = sum(IN_SPLIT_SIZES)

kernel_name = "hybrid_mla_fox_convffn_trunk"


def rms_norm(x, g):
    xf = x.astype(jnp.float32)
    y = xf * lax.rsqrt(jnp.mean(xf * xf, axis=-1, keepdims=True) + EPS)
    return (y * g.astype(jnp.float32)).astype(x.dtype)


def apply_rope(x, cos, sin):
    xf = x.astype(jnp.float32)
    half = xf.shape[-1] // 2
    x1, x2 = xf[..., :half], xf[..., half:]
    out = jnp.concatenate([x1 * cos - x2 * sin, x2 * cos + x1 * sin], axis=-1)
    return out.astype(x.dtype)


def mla_mixer(c_q, c_kv, k_rope, g_q, g_kv, w_q_up, w_kv_up, cos, sin, chunk_id):
    B, L, _ = c_q.shape
    q = (rms_norm(c_q, g_q) @ w_q_up).reshape(B, L, MLA_HEADS, MLA_NOPE + MLA_ROPE)
    q_nope = q[..., :MLA_NOPE]
    q_rope = apply_rope(q[..., MLA_NOPE:], cos[None, :, None], sin[None, :, None])
    kv = (rms_norm(c_kv, g_kv) @ w_kv_up).reshape(B, L, MLA_HEADS, MLA_NOPE + MLA_V)
    k_nope, v = kv[..., :MLA_NOPE], kv[..., MLA_NOPE:]
    k_r = apply_rope(k_rope, cos[None], sin[None])
    scale = (MLA_NOPE + MLA_ROPE) ** -0.5

    def one_block(start):
        qn = lax.dynamic_slice_in_dim(q_nope, start, Q_BLOCK, axis=1)
        qr = lax.dynamic_slice_in_dim(q_rope, start, Q_BLOCK, axis=1)
        cq = lax.dynamic_slice_in_dim(chunk_id, start, Q_BLOCK)
        s = (jnp.einsum('bqhd,bkhd->bhqk', qn, k_nope)
             + jnp.einsum('bqhr,bkr->bhqk', qr, k_r)).astype(jnp.float32) * scale
        mask = chunk_id[None, :] <= cq[:, None]
        s = jnp.where(mask, s, NEG)
        p = jax.nn.softmax(s, axis=-1).astype(v.dtype)
        return jnp.einsum('bhqk,bkhd->bqhd', p, v)

    starts = jnp.arange(L // Q_BLOCK, dtype=jnp.int32) * Q_BLOCK
    out = lax.map(one_block, starts)
    return jnp.moveaxis(out, 0, 1).reshape(B, L, MLA_HEADS * MLA_V)


def fox_mixer(q, k, v, gate, f_logit, b_f, g_q, g_k, pos):
    B, L, _ = q.shape
    q = rms_norm(q.reshape(B, L, FOX_HEADS, FOX_HD), g_q)
    k = rms_norm(k.reshape(B, L, FOX_HEADS, FOX_HD), g_k)
    v = v.reshape(B, L, FOX_HEADS, FOX_HD)
    log_f = jax.nn.log_sigmoid(f_logit.astype(jnp.float32) + b_f.astype(jnp.float32))
    c = jnp.cumsum(log_f, axis=1).transpose(0, 2, 1)
    scale = FOX_HD ** -0.5

    def one_block(start):
        qb = lax.dynamic_slice_in_dim(q, start, Q_BLOCK, axis=1)
        cq = lax.dynamic_slice_in_dim(c, start, Q_BLOCK, axis=2)
        pq = lax.dynamic_slice_in_dim(pos, start, Q_BLOCK)
        s = jnp.einsum('bqhd,bkhd->bhqk', qb, k).astype(jnp.float32) * scale
        s = s + cq[..., :, None] - c[..., None, :]
        mask = pos[None, :] <= pq[:, None]
        s = jnp.where(mask, s, NEG)
        p = jax.nn.softmax(s, axis=-1).astype(v.dtype)
        return jnp.einsum('bhqk,bkhd->bqhd', p, v)

    starts = jnp.arange(L // Q_BLOCK, dtype=jnp.int32) * Q_BLOCK
    out = jnp.moveaxis(lax.map(one_block, starts), 0, 1).reshape(B, L, FOX_W)
    return out * jax.nn.sigmoid(gate)


def conv_ffn(x, w_up, w_conv, b_conv, w_down):
    L = x.shape[1]
    h = x @ w_up
    hp = jnp.pad(h, ((0, 0), (CONV_K - 1, 0), (0, 0)))
    h = b_conv + sum(w_conv[j] * hp[:, j:j + L] for j in range(CONV_K))
    gate, up = jnp.split(h, 2, axis=-1)
    return (jax.nn.gelu(gate, approximate=True) * up) @ w_down


def setup_inputs(seed: int = 0) -> dict:
    key = jax.random.key(seed)
    ks = jax.random.split(key, 20)
    f32 = jnp.float32
    nrm = lambda k, shp, s: jax.random.normal(k, shp, f32) * s
    gain = lambda k, shp: 1.0 + 0.05 * jax.random.normal(k, shp, f32)
    return {
        "x": jax.random.normal(ks[0], (BATCH, SEQ, D_MODEL), f32),
        "meta_tokens": nrm(ks[1], (N_META, D_MODEL), 1.0),
        "ln_mix_pre": gain(ks[2], (DEPTH, D_MODEL)),
        "w_in": nrm(ks[3], (DEPTH, D_MODEL, IN_COLS), D_MODEL ** -0.5),
        "b_forget": FORGET_BIAS + 0.5 * jax.random.normal(ks[4], (DEPTH, FOX_HEADS), f32),
        "g_q_latent": gain(ks[5], (DEPTH, MLA_Q_LORA)),
        "g_kv_latent": gain(ks[6], (DEPTH, MLA_KV_LORA)),
        "w_q_up": nrm(ks[7], (DEPTH, MLA_Q_LORA, MLA_HEADS * (MLA_NOPE + MLA_ROPE)), MLA_Q_LORA ** -0.5),
        "w_kv_up": nrm(ks[8], (DEPTH, MLA_KV_LORA, MLA_HEADS * (MLA_NOPE + MLA_V)), MLA_KV_LORA ** -0.5),
        "g_fox_q": gain(ks[9], (DEPTH, FOX_HD)),
        "g_fox_k": gain(ks[10], (DEPTH, FOX_HD)),
        "w_out": nrm(ks[11], (DEPTH, MIX_WIDTH, D_MODEL), MIX_WIDTH ** -0.5),
        "ln_mix_post": gain(ks[12], (DEPTH, D_MODEL)),
        "ln_ffn_pre": gain(ks[13], (DEPTH, D_MODEL)),
        "w_ffn_up": nrm(ks[14], (DEPTH, D_MODEL, 2 * D_FF), D_MODEL ** -0.5),
        "w_ffn_conv": nrm(ks[15], (DEPTH, CONV_K, 2 * D_FF), CONV_K ** -0.5),
        "b_ffn_conv": nrm(ks[16], (DEPTH, 2 * D_FF), 0.02),
        "w_ffn_down": nrm(ks[17], (DEPTH, D_FF, D_MODEL), D_FF ** -0.5),
        "ln_ffn_post": gain(ks[18], (DEPTH, D_MODEL)),
    }


def reference(x, meta_tokens, ln_mix_pre, w_in, b_forget, g_q_latent, g_kv_latent, w_q_up, w_kv_up,
              g_fox_q, g_fox_k, w_out, ln_mix_post, ln_ffn_pre, w_ffn_up, w_ffn_conv, b_ffn_conv,
              w_ffn_down, ln_ffn_post):
    B, S, D = x.shape
    L = N_META + S
    L_pad = -(-L // Q_BLOCK) * Q_BLOCK
    h = jnp.concatenate([
        jnp.broadcast_to(meta_tokens.astype(x.dtype)[None], (B, N_META, D)),
        x,
        jnp.zeros((B, L_pad - L, D), x.dtype)], axis=1)
    pos = jnp.arange(L_pad, dtype=jnp.int32)
    chunk_id = jnp.where(pos < N_META, 0,
                         jnp.where(pos < L, 1 + (pos - N_META) // CHUNK, 2 + S // CHUNK)).astype(jnp.int32)
    half = MLA_ROPE // 2
    inv_freq = ROPE_THETA ** (-jnp.arange(half, dtype=jnp.float32) / half)
    ang = pos.astype(jnp.float32)[:, None] * inv_freq[None, :]
    cos, sin = jnp.cos(ang), jnp.sin(ang)
    split_idx = [int(v) for v in np.cumsum(IN_SPLIT_SIZES)[:-1]]

    for l in range(DEPTH):
        hn = rms_norm(h, ln_mix_pre[l])
        c_q, c_kv, k_rope, fq, fk, fv, fg, ff = jnp.split(hn @ w_in[l], split_idx, axis=-1)
        a = mla_mixer(c_q, c_kv, k_rope, g_q_latent[l], g_kv_latent[l], w_q_up[l], w_kv_up[l],
                      cos, sin, chunk_id)
        b = fox_mixer(fq, fk, fv, fg, ff, b_forget[l], g_fox_q[l], g_fox_k[l], pos)
        mix = jnp.concatenate([a, b], axis=-1) @ w_out[l]
        h = h + rms_norm(mix, ln_mix_post[l])
        f = conv_ffn(rms_norm(h, ln_ffn_pre[l]), w_ffn_up[l], w_ffn_conv[l], b_ffn_conv[l], w_ffn_down[l])
        h = h + rms_norm(f, ln_ffn_post[l])

    return h[:, N_META:N_META + S]
```

```python
import functools

import jax
import jax.numpy as jnp
import numpy as np
from jax import lax
from jax.experimental import pallas as pl
from jax.experimental.pallas import tpu as pltpu

F32 = jnp.float32
BF16 = jnp.bfloat16

D_MODEL = 2048
SEQ = 8192
DEPTH = 4
CHUNK = 64
CHUNK_SHIFT = 6
N_META = 16
MLA_HEADS = 8
MLA_Q_LORA = 512
MLA_KV_LORA = 512
MLA_NOPE = 128
MLA_ROPE = 64
MLA_V = 128
ROPE_THETA = 10000.0
FOX_HEADS = 8
FOX_HD = 128
FOX_W = FOX_HEADS * FOX_HD
D_FF = 5632
CONV_K = 3
EPS = 1e-6
NEG = -1e30

LANE = 128
Q_TILE = 256
ROW_PAD = Q_TILE - N_META
L_BUF = ROW_PAD + N_META + SEQ
N_QT = L_BUF // Q_TILE

MLA_SCALE = (MLA_NOPE + MLA_ROPE) ** -0.5
FOX_SCALE = FOX_HD ** -0.5

TAIL_W = 4 * LANE
IN_PACKED = MLA_Q_LORA + MLA_KV_LORA + 4 * FOX_W + TAIL_W
GATE_COL0 = MLA_Q_LORA + MLA_KV_LORA + 3 * FOX_W
Q_HEAD_W = 3 * LANE
QK_W = 2 * LANE

TM_PROJ = 768
TN_PROJ = 512
TM_OUT = 384
TM_FFN = 768
FC_FFN = 512
HALO = 16

VMEM_LIMIT = 56 * 1024 * 1024


def _rms(x, g):
    ms = jnp.mean(x * x, axis=-1, keepdims=True)
    return x * lax.rsqrt(ms + EPS) * g


def _cparams(sem):
    return pltpu.CompilerParams(dimension_semantics=sem, vmem_limit_bytes=VMEM_LIMIT)


def _proj_in_kernel(x_ref, g_ref, w_ref, o_ref, xn_ref):
    @pl.when(pl.program_id(1) == 0)
    def _():
        def body(r, carry):
            rows = pl.ds(pl.multiple_of(r * 64, 64), 64)
            xn_ref[rows, :] = _rms(x_ref[rows, :], g_ref[...]).astype(BF16)
            return carry
        lax.fori_loop(0, TM_PROJ // 64, body, 0)

    o_ref[...] = jnp.dot(xn_ref[...], w_ref[...], preferred_element_type=F32)


def _proj_in(h, g, w):
    return pl.pallas_call(
        _proj_in_kernel,
        grid=(L_BUF // TM_PROJ, IN_PACKED // TN_PROJ),
        in_specs=[
            pl.BlockSpec((TM_PROJ, D_MODEL), lambda i, j: (i, 0)),
            pl.BlockSpec((1, D_MODEL), lambda i, j: (0, 0)),
            pl.BlockSpec((D_MODEL, TN_PROJ), lambda i, j: (0, j)),
        ],
        out_specs=pl.BlockSpec((TM_PROJ, TN_PROJ), lambda i, j: (i, j)),
        out_shape=jax.ShapeDtypeStruct((L_BUF, IN_PACKED), F32),
        scratch_shapes=[pltpu.VMEM((TM_PROJ, D_MODEL), BF16)],
        compiler_params=_cparams(("parallel", "arbitrary")),
        name="proj_in",
    )(h, g, w)


def _prep_kernel(cq_ref, ckv_ref, fq_ref, fk_ref, fv_ref, tail_ref, cos_ref, sin_ref,
                 gql_ref, gkvl_ref, gfq_ref, gfk_ref, bf_ref, wq_ref, wkv_ref, tri_ref,
                 q_ref, k_ref, v_ref, qf_ref, kf_ref, vf_ref, c_ref, ct_ref, carry_ref):
    i = pl.program_id(0)
    cos = cos_ref[...]
    sin = sin_ref[...]
    kr = (tail_ref[:, 0:LANE] * cos + tail_ref[:, LANE:2 * LANE] * sin).astype(BF16)
    cqn = _rms(cq_ref[...], gql_ref[...]).astype(BF16)
    ckvn = _rms(ckv_ref[...], gkvl_ref[...]).astype(BF16)
    for h in range(MLA_HEADS):
        qh = jnp.dot(cqn, wq_ref[:, Q_HEAD_W * h:Q_HEAD_W * (h + 1)], preferred_element_type=F32)
        qr = qh[:, LANE:2 * LANE] * cos + qh[:, 2 * LANE:3 * LANE] * sin
        q_ref[h, :, 0:LANE] = (qh[:, 0:LANE] * MLA_SCALE).astype(BF16)
        q_ref[h, :, LANE:QK_W] = (qr * MLA_SCALE).astype(BF16)
        kvh = jnp.dot(ckvn, wkv_ref[:, 2 * LANE * h:2 * LANE * (h + 1)], preferred_element_type=F32)
        k_ref[h, :, 0:LANE] = kvh[:, 0:LANE].astype(BF16)
        k_ref[h, :, LANE:QK_W] = kr
        v_ref[h] = kvh[:, LANE:2 * LANE].astype(BF16)
    for h in range(FOX_HEADS):
        cols = slice(FOX_HD * h, FOX_HD * (h + 1))
        qf_ref[h] = (_rms(fq_ref[:, cols], gfq_ref[...]) * FOX_SCALE).astype(BF16)
        kf_ref[h] = _rms(fk_ref[:, cols], gfk_ref[...]).astype(BF16)
        vf_ref[h] = fv_ref[:, cols].astype(BF16)

    z = tail_ref[:, 2 * LANE:3 * LANE] + bf_ref[...]
    logf = jnp.minimum(z, 0.0) - jnp.log(1.0 + jnp.exp(-jnp.abs(z)))
    hi = logf.astype(BF16)
    r1 = logf - hi.astype(F32)
    mid = r1.astype(BF16)
    lo = (r1 - mid.astype(F32)).astype(BF16)
    tri = tri_ref[...]
    cs = (jnp.dot(tri, hi, preferred_element_type=F32)
          + jnp.dot(tri, mid, preferred_element_type=F32)
          + jnp.dot(tri, lo, preferred_element_type=F32))

    @pl.when(i == 0)
    def _():
        carry_ref[...] = jnp.zeros_like(carry_ref)

    c = cs + carry_ref[0:1, :]
    c_ref[...] = c
    carry_ref[...] = jnp.broadcast_to(c[Q_TILE - 1:Q_TILE, :], carry_ref.shape)
    ct_ref[...] = c.T[0:FOX_HEADS, :]


def _prep(proj, cos_t, sin_t, gql, gkvl, gfq, gfk, bf_pad, wq, wkv, tri):
    tm = Q_TILE
    row = lambda i: (i, 0)
    const = lambda i: (0, 0)
    head_out = lambda w: pl.BlockSpec((MLA_HEADS, tm, w), lambda i: (0, i, 0))
    return pl.pallas_call(
        _prep_kernel,
        grid=(L_BUF // tm,),
        in_specs=[
            pl.BlockSpec((tm, MLA_Q_LORA), lambda i: (i, 0)),
            pl.BlockSpec((tm, MLA_KV_LORA), lambda i: (i, 1)),
            pl.BlockSpec((tm, FOX_W), lambda i: (i, 1)),
            pl.BlockSpec((tm, FOX_W), lambda i: (i, 2)),
            pl.BlockSpec((tm, FOX_W), lambda i: (i, 3)),
            pl.BlockSpec((tm, TAIL_W), lambda i: (i, IN_PACKED // TAIL_W - 1)),
            pl.BlockSpec((tm, LANE), row),
            pl.BlockSpec((tm, LANE), row),
            pl.BlockSpec((1, MLA_Q_LORA), const),
            pl.BlockSpec((1, MLA_KV_LORA), const),
            pl.BlockSpec((1, FOX_HD), const),
            pl.BlockSpec((1, FOX_HD), const),
            pl.BlockSpec((1, LANE), const),
            pl.BlockSpec((MLA_Q_LORA, MLA_HEADS * Q_HEAD_W), const),
            pl.BlockSpec((MLA_KV_LORA, MLA_HEADS * 2 * LANE), const),
            pl.BlockSpec((tm, tm), const),
        ],
        out_specs=[
            head_out(QK_W), head_out(QK_W), head_out(MLA_V),
            head_out(FOX_HD), head_out(FOX_HD), head_out(FOX_HD),
            pl.BlockSpec((tm, LANE), row),
            pl.BlockSpec((FOX_HEADS, tm), lambda i: (0, i)),
        ],
        out_shape=[
            jax.ShapeDtypeStruct((MLA_HEADS, L_BUF, QK_W), BF16),
            jax.ShapeDtypeStruct((MLA_HEADS, L_BUF, QK_W), BF16),
            jax.ShapeDtypeStruct((MLA_HEADS, L_BUF, MLA_V), BF16),
            jax.ShapeDtypeStruct((FOX_HEADS, L_BUF, FOX_HD), BF16),
            jax.ShapeDtypeStruct((FOX_HEADS, L_BUF, FOX_HD), BF16),
            jax.ShapeDtypeStruct((FOX_HEADS, L_BUF, FOX_HD), BF16),
            jax.ShapeDtypeStruct((L_BUF, LANE), F32),
            jax.ShapeDtypeStruct((FOX_HEADS, L_BUF), F32),
        ],
        scratch_shapes=[pltpu.VMEM((8, LANE), F32)],
        compiler_params=_cparams(("arbitrary",)),
        name="attn_prep",
    )(proj, proj, proj, proj, proj, proj, cos_t, sin_t, gql, gkvl, gfq, gfk, bf_pad, wq, wkv, tri)


def _online_softmax_step(s, v, m_sc, l_sc, acc_sc, row_shift=None):
    m_prev = m_sc[...]
    m_chunk = jnp.max(s, axis=-1, keepdims=True)
    if row_shift is not None:
        m_chunk = m_chunk + row_shift
    m_new = jnp.maximum(m_prev, m_chunk)
    alpha = jnp.exp(m_prev - m_new)
    sub = m_new if row_shift is None else m_new - row_shift
    p = jnp.exp(s - sub)
    l_sc[...] = alpha * l_sc[...] + jnp.sum(p, axis=-1, keepdims=True)
    acc_sc[...] = alpha * acc_sc[...] + jnp.dot(p.astype(BF16), v, preferred_element_type=F32)
    m_sc[...] = m_new


def _scores(q, k):
    return lax.dot_general(q, k, (((1,), (1,)), ((), ())), preferred_element_type=F32)


def _tile_iotas():
    rows = lax.broadcasted_iota(jnp.int32, (Q_TILE, Q_TILE), 0)
    cols = lax.broadcasted_iota(jnp.int32, (Q_TILE, Q_TILE), 1)
    return rows, cols


def _mla_attn_kernel(q_ref, k_ref, v_ref, o_ref, m_sc, l_sc, acc_sc):
    i = pl.program_id(1)
    q = q_ref[0]
    m_sc[...] = jnp.full_like(m_sc, NEG)
    l_sc[...] = jnp.zeros_like(l_sc)
    acc_sc[...] = jnp.zeros_like(acc_sc)

    def chunk(j):
        rows = pl.ds(pl.multiple_of(j * Q_TILE, Q_TILE), Q_TILE)
        return k_ref[0, rows, :], v_ref[0, rows, :]

    @pl.when(i > 0)
    def _():
        _, cols = _tile_iotas()
        k, v = chunk(0)
        s = jnp.where(cols >= ROW_PAD, _scores(q, k), NEG)
        _online_softmax_step(s, v, m_sc, l_sc, acc_sc)

    def body(j, carry):
        k, v = chunk(j)
        _online_softmax_step(_scores(q, k), v, m_sc, l_sc, acc_sc)
        return carry
    lax.fori_loop(1, i, body, 0)

    rows, cols = _tile_iotas()
    k, v = chunk(i)
    first_key = jnp.where(i > 0, 0, ROW_PAD)
    mask = (jnp.right_shift(cols, CHUNK_SHIFT) <= jnp.right_shift(rows, CHUNK_SHIFT)) & (cols >= first_key)
    s = jnp.where(mask, _scores(q, k), NEG)
    _online_softmax_step(s, v, m_sc, l_sc, acc_sc)

    o_ref[...] = (acc_sc[...] / l_sc[...]).astype(o_ref.dtype)


def _mla_attn(q, k, v):
    return pl.pallas_call(
        _mla_attn_kernel,
        grid=(MLA_HEADS, N_QT),
        in_specs=[
            pl.BlockSpec((1, Q_TILE, QK_W), lambda h, i: (h, i, 0)),
            pl.BlockSpec((1, L_BUF, QK_W), lambda h, i: (h, 0, 0)),
            pl.BlockSpec((1, L_BUF, MLA_V), lambda h, i: (h, 0, 0)),
        ],
        out_specs=pl.BlockSpec((Q_TILE, MLA_V), lambda h, i: (i, h)),
        out_shape=jax.ShapeDtypeStruct((L_BUF, MLA_HEADS * MLA_V), BF16),
        scratch_shapes=[pltpu.VMEM((Q_TILE, 1), F32), pltpu.VMEM((Q_TILE, 1), F32),
                        pltpu.VMEM((Q_TILE, MLA_V), F32)],
        compiler_params=_cparams(("parallel", "arbitrary")),
        name="mla_attn",
    )(q, k, v)


def _fox_attn_kernel(q_ref, k_ref, v_ref, c_ref, ct_ref, gate_ref, o_ref, m_sc, l_sc, acc_sc):
    h = pl.program_id(0)
    i = pl.program_id(1)
    q = q_ref[0]
    m_sc[...] = jnp.full_like(m_sc, NEG)
    l_sc[...] = jnp.zeros_like(l_sc)
    acc_sc[...] = jnp.zeros_like(acc_sc)
    lane = lax.broadcasted_iota(jnp.int32, (Q_TILE, LANE), 1)
    cq = jnp.sum(jnp.where(lane == h, c_ref[...], 0.0), axis=-1, keepdims=True)

    def chunk(j):
        rows = pl.ds(pl.multiple_of(j * Q_TILE, Q_TILE), Q_TILE)
        return k_ref[0, rows, :], v_ref[0, rows, :], ct_ref[0, j]

    @pl.when(i > 0)
    def _():
        _, cols = _tile_iotas()
        k, v, ck = chunk(0)
        s = jnp.where(cols >= ROW_PAD, _scores(q, k) - ck, NEG)
        _online_softmax_step(s, v, m_sc, l_sc, acc_sc, row_shift=cq)

    def body(j, carry):
        k, v, ck = chunk(j)
        _online_softmax_step(_scores(q, k) - ck, v, m_sc, l_sc, acc_sc, row_shift=cq)
        return carry
    lax.fori_loop(1, i, body, 0)

    rows, cols = _tile_iotas()
    k, v, ck = chunk(i)
    first_key = jnp.where(i > 0, 0, ROW_PAD)
    mask = (cols <= rows) & (cols >= first_key)
    s = jnp.where(mask, _scores(q, k) - ck, NEG)
    _online_softmax_step(s, v, m_sc, l_sc, acc_sc, row_shift=cq)

    out = acc_sc[...] / l_sc[...]
    o_ref[...] = (out * jax.nn.sigmoid(gate_ref[...])).astype(o_ref.dtype)


def _fox_attn(q, k, v, c, ct, proj):
    return pl.pallas_call(
        _fox_attn_kernel,
        grid=(FOX_HEADS, N_QT),
        in_specs=[
            pl.BlockSpec((1, Q_TILE, FOX_HD), lambda h, i: (h, i, 0)),
            pl.BlockSpec((1, L_BUF, FOX_HD), lambda h, i: (h, 0, 0)),
            pl.BlockSpec((1, L_BUF, FOX_HD), lambda h, i: (h, 0, 0)),
            pl.BlockSpec((Q_TILE, LANE), lambda h, i: (i, 0)),
            pl.BlockSpec((1, N_QT, 1, Q_TILE), lambda h, i: (h, 0, 0, 0)),
            pl.BlockSpec((Q_TILE, FOX_HD), lambda h, i: (i, GATE_COL0 // FOX_HD + h)),
        ],
        out_specs=pl.BlockSpec((Q_TILE, FOX_HD), lambda h, i: (i, h)),
        out_shape=jax.ShapeDtypeStruct((L_BUF, FOX_W), BF16),
        scratch_shapes=[pltpu.VMEM((Q_TILE, 1), F32), pltpu.VMEM((Q_TILE, 1), F32),
                        pltpu.VMEM((Q_TILE, FOX_HD), F32)],
        compiler_params=_cparams(("parallel", "arbitrary")),
        name="fox_attn",
    )(q, k, v, c, ct, proj)


def _mix_out_kernel(a_ref, b_ref, wa_ref, wb_ref, h_ref, g_ref, o_ref):
    mix = (jnp.dot(a_ref[...], wa_ref[...], preferred_element_type=F32)
           + jnp.dot(b_ref[...], wb_ref[...], preferred_element_type=F32))
    out = h_ref[...] + _rms(mix, g_ref[...])
    row = pl.program_id(0) * TM_OUT + lax.broadcasted_iota(jnp.int32, (TM_OUT, 1), 0)
    o_ref[...] = jnp.where(row >= ROW_PAD, out, 0.0)


def _mix_out(a, b, wa, wb, h, g):
    return pl.pallas_call(
        _mix_out_kernel,
        grid=(L_BUF // TM_OUT,),
        in_specs=[
            pl.BlockSpec((TM_OUT, MLA_HEADS * MLA_V), lambda i: (i, 0)),
            pl.BlockSpec((TM_OUT, FOX_W), lambda i: (i, 0)),
            pl.BlockSpec((MLA_HEADS * MLA_V, D_MODEL), lambda i: (0, 0)),
            pl.BlockSpec((FOX_W, D_MODEL), lambda i: (0, 0)),
            pl.BlockSpec((TM_OUT, D_MODEL), lambda i: (i, 0)),
            pl.BlockSpec((1, D_MODEL), lambda i: (0, 0)),
        ],
        out_specs=pl.BlockSpec((TM_OUT, D_MODEL), lambda i: (i, 0)),
        out_shape=jax.ShapeDtypeStruct((L_BUF, D_MODEL), F32),
        compiler_params=_cparams(("parallel",)),
        name="mix_out",
    )(a, b, wa, wb, h, g)


def _gelu_tanh(x):
    return 0.5 * x * (1.0 + jnp.tanh(np.sqrt(2.0 / np.pi).astype(np.float32) * (x + 0.044715 * (x * x * x))))


def _ffn_kernel(h_ref, halo_ref, gpre_ref, wg_ref, wu_ref, cwg_ref, cwu_ref, cbg_ref, cbu_ref,
                wd_ref, gpost_ref, o_ref, xn_ref, ug_ref, uu_ref):
    c = pl.program_id(1)

    @pl.when(c == 0)
    def _():
        xn_ref[0:HALO, :] = _rms(halo_ref[...], gpre_ref[...]).astype(BF16)

        def body(r, carry):
            src = pl.ds(pl.multiple_of(r * 64, 64), 64)
            dst = pl.ds(pl.multiple_of(HALO + r * 64, 16), 64)
            xn_ref[dst, :] = _rms(h_ref[src, :], gpre_ref[...]).astype(BF16)
            return carry
        lax.fori_loop(0, TM_FFN // 64, body, 0)
        o_ref[...] = jnp.zeros_like(o_ref)

    xn = xn_ref[...]
    ug_ref[...] = jnp.dot(xn, wg_ref[...], preferred_element_type=F32)
    uu_ref[...] = jnp.dot(xn, wu_ref[...], preferred_element_type=F32)

    def conv(u_ref, w_ref, b_ref):
        acc = b_ref[...] + w_ref[CONV_K - 1:CONV_K, :] * u_ref[HALO:HALO + TM_FFN, :]
        for t in range(1, CONV_K):
            acc = acc + w_ref[CONV_K - 1 - t:CONV_K - t, :] * u_ref[HALO - t:HALO - t + TM_FFN, :]
        return acc

    act = _gelu_tanh(conv(ug_ref, cwg_ref, cbg_ref)) * conv(uu_ref, cwu_ref, cbu_ref)
    o_ref[...] += jnp.dot(act.astype(BF16), wd_ref[...], preferred_element_type=F32)

    @pl.when(c == pl.num_programs(1) - 1)
    def _():
        out = h_ref[...] + _rms(o_ref[...], gpost_ref[...])
        row = pl.program_id(0) * TM_FFN + lax.broadcasted_iota(jnp.int32, (TM_FFN, 1), 0)
        o_ref[...] = jnp.where(row >= ROW_PAD, out, 0.0)


def _ffn(h, gpre, w_up, w_conv, b_conv, w_down, gpost):
    n_fc = D_FF // FC_FFN
    halo_blocks = TM_FFN // HALO
    return pl.pallas_call(
        _ffn_kernel,
        grid=(L_BUF // TM_FFN, n_fc),
        in_specs=[
            pl.BlockSpec((TM_FFN, D_MODEL), lambda i, c: (i, 0)),
            pl.BlockSpec((HALO, D_MODEL), lambda i, c: (jnp.maximum(i * halo_blocks - 1, 0), 0)),
            pl.BlockSpec((1, D_MODEL), lambda i, c: (0, 0)),
            pl.BlockSpec((D_MODEL, FC_FFN), lambda i, c: (0, c)),
            pl.BlockSpec((D_MODEL, FC_FFN), lambda i, c: (0, n_fc + c)),
            pl.BlockSpec((CONV_K, FC_FFN), lambda i, c: (0, c)),
            pl.BlockSpec((CONV_K, FC_FFN), lambda i, c: (0, n_fc + c)),
            pl.BlockSpec((1, FC_FFN), lambda i, c: (0, c)),
            pl.BlockSpec((1, FC_FFN), lambda i, c: (0, n_fc + c)),
            pl.BlockSpec((FC_FFN, D_MODEL), lambda i, c: (c, 0)),
            pl.BlockSpec((1, D_MODEL), lambda i, c: (0, 0)),
        ],
        out_specs=pl.BlockSpec((TM_FFN, D_MODEL), lambda i, c: (i, 0)),
        out_shape=jax.ShapeDtypeStruct((L_BUF, D_MODEL), F32),
        scratch_shapes=[pltpu.VMEM((HALO + TM_FFN, D_MODEL), BF16),
                        pltpu.VMEM((HALO + TM_FFN, FC_FFN), F32),
                        pltpu.VMEM((HALO + TM_FFN, FC_FFN), F32)],
        compiler_params=_cparams(("parallel", "arbitrary")),
        name="conv_ffn",
    )(h, h, gpre, w_up, w_up, w_conv, w_conv, b_conv, b_conv, w_down, gpost)


def _rotate_half_cols(w):
    half = w.shape[-1] // 2
    return jnp.concatenate([-w[..., half:], w[..., :half]], axis=-1)


def _pad_cols(w, width):
    return jnp.pad(w, ((0, 0), (0, width - w.shape[-1])))


def _pack_w_in(w):
    o = np.cumsum([0, MLA_Q_LORA, MLA_KV_LORA, MLA_ROPE, FOX_W, FOX_W, FOX_W, FOX_W, FOX_HEADS])
    c_q, c_kv, k_rope, fq, fk, fv, fg, ff = [w[:, o[n]:o[n + 1]] for n in range(8)]
    tail = jnp.concatenate([_pad_cols(k_rope, LANE), _pad_cols(_rotate_half_cols(k_rope), LANE),
                            _pad_cols(ff, 2 * LANE)], axis=1)
    return jnp.concatenate([c_q, c_kv, fq, fk, fv, fg, tail], axis=1).astype(BF16)


def _pack_w_q_up(w):
    w = w.reshape(MLA_Q_LORA, MLA_HEADS, MLA_NOPE + MLA_ROPE)
    nope, rope = w[..., :MLA_NOPE], w[..., MLA_NOPE:]
    pad = ((0, 0), (0, 0), (0, LANE - MLA_ROPE))
    packed = jnp.concatenate([nope, jnp.pad(rope, pad), jnp.pad(_rotate_half_cols(rope), pad)], axis=-1)
    return packed.reshape(MLA_Q_LORA, MLA_HEADS * Q_HEAD_W).astype(BF16)


def _rope_tables():
    pos = jnp.maximum(jnp.arange(L_BUF, dtype=jnp.int32) - ROW_PAD, 0).astype(F32)
    half = MLA_ROPE // 2
    inv_freq = ROPE_THETA ** (-jnp.arange(half, dtype=F32) / half)
    ang = pos[:, None] * inv_freq[None, :]
    zeros = jnp.zeros((L_BUF, LANE - MLA_ROPE), F32)
    cos, sin = jnp.cos(ang), jnp.sin(ang)
    return (jnp.concatenate([cos, cos, zeros], axis=1), jnp.concatenate([sin, sin, zeros], axis=1))


def kernel(x, meta_tokens, ln_mix_pre, w_in, b_forget, g_q_latent, g_kv_latent, w_q_up, w_kv_up,
           g_fox_q, g_fox_k, w_out, ln_mix_post, ln_ffn_pre, w_ffn_up, w_ffn_conv, b_ffn_conv,
           w_ffn_down, ln_ffn_post):
    assert x.shape == (1, SEQ, D_MODEL), x.shape
    h = jnp.concatenate([jnp.zeros((ROW_PAD, D_MODEL), x.dtype), meta_tokens.astype(x.dtype), x[0]], axis=0)
    cos_t, sin_t = _rope_tables()
    tri = (lax.broadcasted_iota(jnp.int32, (Q_TILE, Q_TILE), 0)
           >= lax.broadcasted_iota(jnp.int32, (Q_TILE, Q_TILE), 1)).astype(BF16)
    row2d = lambda v: v.reshape(1, -1).astype(F32)

    for l in range(DEPTH):
        proj = _proj_in(h, row2d(ln_mix_pre[l]), _pack_w_in(w_in[l]))
        q, k, v, qf, kf, vf, c, ct = _prep(
            proj, cos_t, sin_t, row2d(g_q_latent[l]), row2d(g_kv_latent[l]),
            row2d(g_fox_q[l]), row2d(g_fox_k[l]), _pad_cols(row2d(b_forget[l]), LANE),
            _pack_w_q_up(w_q_up[l]), w_kv_up[l].astype(BF16), tri)
        a = _mla_attn(q, k, v)
        b = _fox_attn(qf, kf, vf, c, ct.reshape(FOX_HEADS, N_QT, 1, Q_TILE), proj)
        w_o = w_out[l].astype(BF16)
        h = _mix_out(a, b, w_o[:MLA_HEADS * MLA_V], w_o[MLA_HEADS * MLA_V:], h, row2d(ln_mix_post[l]))
        h = _ffn(h, row2d(ln_ffn_pre[l]), w_ffn_up[l].astype(BF16), w_ffn_conv[l].astype(F32),
                 row2d(b_ffn_conv[l]), w_ffn_down[l].astype(BF16), row2d(ln_ffn_post[l]))

    return h[ROW_PAD + N_META:][None]
```

```python
import functools

import jax
import jax.numpy as jnp
import numpy as np
from jax import lax
from jax.experimental import pallas as pl
from jax.experimental.pallas import tpu as pltpu

F32 = jnp.float32
BF16 = jnp.bfloat16

D_MODEL = 2048
SEQ = 8192
DEPTH = 4
CHUNK = 64
CHUNK_SHIFT = 6
N_META = 16
MLA_HEADS = 8
MLA_Q_LORA = 512
MLA_KV_LORA = 512
MLA_NOPE = 128
MLA_ROPE = 64
MLA_V = 128
ROPE_THETA = 10000.0
FOX_HEADS = 8
FOX_HD = 128
FOX_W = FOX_HEADS * FOX_HD
D_FF = 5632
CONV_K = 3
EPS = 1e-6
NEG = -1e30

LANE = 128
Q_TILE = 256
ROW_PAD = Q_TILE - N_META
L_BUF = ROW_PAD + N_META + SEQ
ATT_TILE = 768
N_AT = L_BUF // ATT_TILE

LOG2E = float(np.log2(np.e))
MLA_SCALE = (MLA_NOPE + MLA_ROPE) ** -0.5 * LOG2E
FOX_SCALE = FOX_HD ** -0.5 * LOG2E

TAIL_W = 4 * LANE
IN_PACKED = MLA_Q_LORA + MLA_KV_LORA + 4 * FOX_W + TAIL_W
GATE_COL0 = MLA_Q_LORA + MLA_KV_LORA + 3 * FOX_W
Q_HEAD_W = 3 * LANE
QK_W = 2 * LANE

TM_PROJ = 768
TN_PROJ = 512
TM_OUT = 384
TM_FFN = 768
FC_FFN = 512
HALO = 16

VMEM_LIMIT = 56 * 1024 * 1024


def _rms(x, g):
    ms = jnp.mean(x * x, axis=-1, keepdims=True)
    return x * lax.rsqrt(ms + EPS) * g


def _cparams(sem):
    return pltpu.CompilerParams(dimension_semantics=sem, vmem_limit_bytes=VMEM_LIMIT)


def _proj_in_kernel(x_ref, g_ref, w_ref, o_ref, xn_ref):
    @pl.when(pl.program_id(1) == 0)
    def _():
        def body(r, carry):
            rows = pl.ds(pl.multiple_of(r * 64, 64), 64)
            xn_ref[rows, :] = _rms(x_ref[rows, :], g_ref[...]).astype(BF16)
            return carry
        lax.fori_loop(0, TM_PROJ // 64, body, 0)

    o_ref[...] = jnp.dot(xn_ref[...], w_ref[...], preferred_element_type=F32)


def _proj_in(h, g, w):
    return pl.pallas_call(
        _proj_in_kernel,
        grid=(L_BUF // TM_PROJ, IN_PACKED // TN_PROJ),
        in_specs=[
            pl.BlockSpec((TM_PROJ, D_MODEL), lambda i, j: (i, 0)),
            pl.BlockSpec((1, D_MODEL), lambda i, j: (0, 0)),
            pl.BlockSpec((D_MODEL, TN_PROJ), lambda i, j: (0, j)),
        ],
        out_specs=pl.BlockSpec((TM_PROJ, TN_PROJ), lambda i, j: (i, j)),
        out_shape=jax.ShapeDtypeStruct((L_BUF, IN_PACKED), F32),
        scratch_shapes=[pltpu.VMEM((TM_PROJ, D_MODEL), BF16)],
        compiler_params=_cparams(("parallel", "arbitrary")),
        name="proj_in",
    )(h, g, w)


def _prep_kernel(cq_ref, ckv_ref, fq_ref, fk_ref, fv_ref, tail_ref, cos_ref, sin_ref,
                 gql_ref, gkvl_ref, gfq_ref, gfk_ref, bf_ref, wq_ref, wkv_ref, tri_ref,
                 q_ref, k_ref, v_ref, qf_ref, kf_ref, vf_ref, c_ref, ct_ref, carry_ref):
    i = pl.program_id(0)
    cos = cos_ref[...]
    sin = sin_ref[...]
    kr = (tail_ref[:, 0:LANE] * cos + tail_ref[:, LANE:2 * LANE] * sin).astype(BF16)
    cqn = _rms(cq_ref[...], gql_ref[...]).astype(BF16)
    ckvn = _rms(ckv_ref[...], gkvl_ref[...]).astype(BF16)
    for h in range(MLA_HEADS):
        qh = jnp.dot(cqn, wq_ref[:, Q_HEAD_W * h:Q_HEAD_W * (h + 1)], preferred_element_type=F32)
        qr = qh[:, LANE:2 * LANE] * cos + qh[:, 2 * LANE:3 * LANE] * sin
        q_ref[h, :, 0:LANE] = (qh[:, 0:LANE] * MLA_SCALE).astype(BF16)
        q_ref[h, :, LANE:QK_W] = (qr * MLA_SCALE).astype(BF16)
        kvh = jnp.dot(ckvn, wkv_ref[:, 2 * LANE * h:2 * LANE * (h + 1)], preferred_element_type=F32)
        k_ref[h, :, 0:LANE] = kvh[:, 0:LANE].astype(BF16)
        k_ref[h, :, LANE:QK_W] = kr
        v_ref[h] = kvh[:, LANE:2 * LANE].astype(BF16)
    for h in range(FOX_HEADS):
        cols = slice(FOX_HD * h, FOX_HD * (h + 1))
        qf_ref[h] = (_rms(fq_ref[:, cols], gfq_ref[...]) * FOX_SCALE).astype(BF16)
        kf_ref[h] = _rms(fk_ref[:, cols], gfk_ref[...]).astype(BF16)
        vf_ref[h] = fv_ref[:, cols].astype(BF16)

    z = tail_ref[:, 2 * LANE:3 * LANE] + bf_ref[...]
    logf = (jnp.minimum(z, 0.0) - jnp.log(1.0 + jnp.exp(-jnp.abs(z)))) * LOG2E
    hi = logf.astype(BF16)
    r1 = logf - hi.astype(F32)
    mid = r1.astype(BF16)
    lo = (r1 - mid.astype(F32)).astype(BF16)
    tri = tri_ref[...]
    cs = (jnp.dot(tri, hi, preferred_element_type=F32)
          + jnp.dot(tri, mid, preferred_element_type=F32)
          + jnp.dot(tri, lo, preferred_element_type=F32))

    @pl.when(i == 0)
    def _():
        carry_ref[...] = jnp.zeros_like(carry_ref)

    c = cs + carry_ref[0:1, :]
    c_ref[...] = c
    carry_ref[...] = jnp.broadcast_to(c[Q_TILE - 1:Q_TILE, :], carry_ref.shape)
    ct_ref[...] = c.T[0:FOX_HEADS, :]


def _prep(proj, cos_t, sin_t, gql, gkvl, gfq, gfk, bf_pad, wq, wkv, tri):
    tm = Q_TILE
    row = lambda i: (i, 0)
    const = lambda i: (0, 0)
    head_out = lambda w: pl.BlockSpec((MLA_HEADS, tm, w), lambda i: (0, i, 0))
    return pl.pallas_call(
        _prep_kernel,
        grid=(L_BUF // tm,),
        in_specs=[
            pl.BlockSpec((tm, MLA_Q_LORA), lambda i: (i, 0)),
            pl.BlockSpec((tm, MLA_KV_LORA), lambda i: (i, 1)),
            pl.BlockSpec((tm, FOX_W), lambda i: (i, 1)),
            pl.BlockSpec((tm, FOX_W), lambda i: (i, 2)),
            pl.BlockSpec((tm, FOX_W), lambda i: (i, 3)),
            pl.BlockSpec((tm, TAIL_W), lambda i: (i, IN_PACKED // TAIL_W - 1)),
            pl.BlockSpec((tm, LANE), row),
            pl.BlockSpec((tm, LANE), row),
            pl.BlockSpec((1, MLA_Q_LORA), const),
            pl.BlockSpec((1, MLA_KV_LORA), const),
            pl.BlockSpec((1, FOX_HD), const),
            pl.BlockSpec((1, FOX_HD), const),
            pl.BlockSpec((1, LANE), const),
            pl.BlockSpec((MLA_Q_LORA, MLA_HEADS * Q_HEAD_W), const),
            pl.BlockSpec((MLA_KV_LORA, MLA_HEADS * 2 * LANE), const),
            pl.BlockSpec((tm, tm), const),
        ],
        out_specs=[
            head_out(QK_W), head_out(QK_W), head_out(MLA_V),
            head_out(FOX_HD), head_out(FOX_HD), head_out(FOX_HD),
            pl.BlockSpec((tm, LANE), row),
            pl.BlockSpec((FOX_HEADS, tm), lambda i: (0, i)),
        ],
        out_shape=[
            jax.ShapeDtypeStruct((MLA_HEADS, L_BUF, QK_W), BF16),
            jax.ShapeDtypeStruct((MLA_HEADS, L_BUF, QK_W), BF16),
            jax.ShapeDtypeStruct((MLA_HEADS, L_BUF, MLA_V), BF16),
            jax.ShapeDtypeStruct((FOX_HEADS, L_BUF, FOX_HD), BF16),
            jax.ShapeDtypeStruct((FOX_HEADS, L_BUF, FOX_HD), BF16),
            jax.ShapeDtypeStruct((FOX_HEADS, L_BUF, FOX_HD), BF16),
            jax.ShapeDtypeStruct((L_BUF, LANE), F32),
            jax.ShapeDtypeStruct((FOX_HEADS, L_BUF), F32),
        ],
        scratch_shapes=[pltpu.VMEM((8, LANE), F32)],
        compiler_params=_cparams(("arbitrary",)),
        name="attn_prep",
    )(proj, proj, proj, proj, proj, proj, cos_t, sin_t, gql, gkvl, gfq, gfk, bf_pad, wq, wkv, tri)


def _online_softmax_step(s, v, m_sc, acc_sc, row_shift=None):
    m_prev = m_sc[...]
    m_chunk = jnp.max(s, axis=-1, keepdims=True)
    if row_shift is not None:
        m_chunk = m_chunk + row_shift
    m_new = jnp.maximum(m_prev, m_chunk)
    alpha = jnp.exp2(m_prev - m_new)
    sub = m_new if row_shift is None else m_new - row_shift
    p = jnp.exp2(s - jnp.tile(sub, (1, ATT_TILE // LANE)))
    v_ones = jnp.concatenate([v, jnp.ones_like(v)], axis=1)
    pv = jnp.dot(p.astype(BF16), v_ones, preferred_element_type=F32)
    acc_sc[...] = acc_sc[...] * jnp.tile(alpha, (1, 2)) + pv
    m_sc[...] = m_new


def _scores(q, k):
    return lax.dot_general(q, k, (((1,), (1,)), ((), ())), preferred_element_type=F32)


def _tile_iotas():
    rows = lax.broadcasted_iota(jnp.int32, (ATT_TILE, ATT_TILE), 0)
    cols = lax.broadcasted_iota(jnp.int32, (ATT_TILE, ATT_TILE), 1)
    return rows, cols


def _softmax_init(m_sc, acc_sc):
    m_sc[...] = jnp.full_like(m_sc, NEG)
    acc_sc[...] = jnp.zeros_like(acc_sc)


def _softmax_result(acc_sc):
    return acc_sc[:, 0:LANE] / acc_sc[:, LANE:2 * LANE]


def _mla_attn_kernel(q_ref, k_ref, v_ref, o_ref, m_sc, acc_sc):
    i = pl.program_id(1)
    q = q_ref[0]
    _softmax_init(m_sc, acc_sc)

    def chunk(j):
        rows = pl.ds(pl.multiple_of(j * ATT_TILE, ATT_TILE), ATT_TILE)
        return k_ref[0, rows, :], v_ref[0, rows, :]

    @pl.when(i > 0)
    def _():
        _, cols = _tile_iotas()
        k, v = chunk(0)
        s = jnp.where(cols >= ROW_PAD, _scores(q, k), NEG)
        _online_softmax_step(s, v, m_sc, acc_sc)

    def body(j, carry):
        k, v = chunk(j)
        _online_softmax_step(_scores(q, k), v, m_sc, acc_sc)
        return carry
    lax.fori_loop(1, i, body, 0)

    rows, cols = _tile_iotas()
    k, v = chunk(i)
    first_key = jnp.where(i > 0, 0, ROW_PAD)
    mask = (jnp.right_shift(cols, CHUNK_SHIFT) <= jnp.right_shift(rows, CHUNK_SHIFT)) & (cols >= first_key)
    s = jnp.where(mask, _scores(q, k), NEG)
    _online_softmax_step(s, v, m_sc, acc_sc)

    o_ref[...] = _softmax_result(acc_sc).astype(o_ref.dtype)


_ATT_SCRATCH = [pltpu.VMEM((ATT_TILE, LANE), F32), pltpu.VMEM((ATT_TILE, 2 * LANE), F32)]


def _mla_attn(q, k, v):
    return pl.pallas_call(
        _mla_attn_kernel,
        grid=(MLA_HEADS, N_AT),
        in_specs=[
            pl.BlockSpec((1, ATT_TILE, QK_W), lambda h, i: (h, i, 0)),
            pl.BlockSpec((1, L_BUF, QK_W), lambda h, i: (h, 0, 0)),
            pl.BlockSpec((1, L_BUF, MLA_V), lambda h, i: (h, 0, 0)),
        ],
        out_specs=pl.BlockSpec((ATT_TILE, MLA_V), lambda h, i: (i, h)),
        out_shape=jax.ShapeDtypeStruct((L_BUF, MLA_HEADS * MLA_V), BF16),
        scratch_shapes=_ATT_SCRATCH,
        compiler_params=_cparams(("parallel", "arbitrary")),
        name="mla_attn",
    )(q, k, v)


def _fox_attn_kernel(q_ref, k_ref, v_ref, c_ref, ct_ref, gate_ref, o_ref, m_sc, acc_sc):
    h = pl.program_id(0)
    i = pl.program_id(1)
    q = q_ref[0]
    _softmax_init(m_sc, acc_sc)
    lane = lax.broadcasted_iota(jnp.int32, (ATT_TILE, LANE), 1)
    cq = jnp.sum(jnp.where(lane == h, c_ref[...], 0.0), axis=-1, keepdims=True)

    def chunk(j):
        rows = pl.ds(pl.multiple_of(j * ATT_TILE, ATT_TILE), ATT_TILE)
        return k_ref[0, rows, :], v_ref[0, rows, :], ct_ref[0, j]

    @pl.when(i > 0)
    def _():
        _, cols = _tile_iotas()
        k, v, ck = chunk(0)
        s = jnp.where(cols >= ROW_PAD, _scores(q, k) - ck, NEG)
        _online_softmax_step(s, v, m_sc, acc_sc, row_shift=cq)

    def body(j, carry):
        k, v, ck = chunk(j)
        _online_softmax_step(_scores(q, k) - ck, v, m_sc, acc_sc, row_shift=cq)
        return carry
    lax.fori_loop(1, i, body, 0)

    rows, cols = _tile_iotas()
    k, v, ck = chunk(i)
    first_key = jnp.where(i > 0, 0, ROW_PAD)
    mask = (cols <= rows) & (cols >= first_key)
    s = jnp.where(mask, _scores(q, k) - ck, NEG)
    _online_softmax_step(s, v, m_sc, acc_sc, row_shift=cq)

    o_ref[...] = (_softmax_result(acc_sc) * jax.nn.sigmoid(gate_ref[...])).astype(o_ref.dtype)


def _fox_attn(q, k, v, c, ct, proj):
    return pl.pallas_call(
        _fox_attn_kernel,
        grid=(FOX_HEADS, N_AT),
        in_specs=[
            pl.BlockSpec((1, ATT_TILE, FOX_HD), lambda h, i: (h, i, 0)),
            pl.BlockSpec((1, L_BUF, FOX_HD), lambda h, i: (h, 0, 0)),
            pl.BlockSpec((1, L_BUF, FOX_HD), lambda h, i: (h, 0, 0)),
            pl.BlockSpec((ATT_TILE, LANE), lambda h, i: (i, 0)),
            pl.BlockSpec((1, N_AT, 1, ATT_TILE), lambda h, i: (h, 0, 0, 0)),
            pl.BlockSpec((ATT_TILE, FOX_HD), lambda h, i: (i, GATE_COL0 // FOX_HD + h)),
        ],
        out_specs=pl.BlockSpec((ATT_TILE, FOX_HD), lambda h, i: (i, h)),
        out_shape=jax.ShapeDtypeStruct((L_BUF, FOX_W), BF16),
        scratch_shapes=_ATT_SCRATCH,
        compiler_params=_cparams(("parallel", "arbitrary")),
        name="fox_attn",
    )(q, k, v, c, ct, proj)


def _mix_out_kernel(a_ref, b_ref, wa_ref, wb_ref, h_ref, g_ref, o_ref):
    mix = (jnp.dot(a_ref[...], wa_ref[...], preferred_element_type=F32)
           + jnp.dot(b_ref[...], wb_ref[...], preferred_element_type=F32))
    out = h_ref[...] + _rms(mix, g_ref[...])
    row = pl.program_id(0) * TM_OUT + lax.broadcasted_iota(jnp.int32, (TM_OUT, 1), 0)
    o_ref[...] = jnp.where(row >= ROW_PAD, out, 0.0)


def _mix_out(a, b, wa, wb, h, g):
    return pl.pallas_call(
        _mix_out_kernel,
        grid=(L_BUF // TM_OUT,),
        in_specs=[
            pl.BlockSpec((TM_OUT, MLA_HEADS * MLA_V), lambda i: (i, 0)),
            pl.BlockSpec((TM_OUT, FOX_W), lambda i: (i, 0)),
            pl.BlockSpec((MLA_HEADS * MLA_V, D_MODEL), lambda i: (0, 0)),
            pl.BlockSpec((FOX_W, D_MODEL), lambda i: (0, 0)),
            pl.BlockSpec((TM_OUT, D_MODEL), lambda i: (i, 0)),
            pl.BlockSpec((1, D_MODEL), lambda i: (0, 0)),
        ],
        out_specs=pl.BlockSpec((TM_OUT, D_MODEL), lambda i: (i, 0)),
        out_shape=jax.ShapeDtypeStruct((L_BUF, D_MODEL), F32),
        compiler_params=_cparams(("parallel",)),
        name="mix_out",
    )(a, b, wa, wb, h, g)


def _gelu_tanh(x):
    return 0.5 * x * (1.0 + jnp.tanh(np.sqrt(2.0 / np.pi).astype(np.float32) * (x + 0.044715 * (x * x * x))))


def _ffn_kernel(h_ref, halo_ref, gpre_ref, wg_ref, wu_ref, cwg_ref, cwu_ref, cbg_ref, cbu_ref,
                wd_ref, gpost_ref, o_ref, xn_ref, ug_ref, uu_ref):
    c = pl.program_id(1)

    @pl.when(c == 0)
    def _():
        xn_ref[0:HALO, :] = _rms(halo_ref[...], gpre_ref[...]).astype(BF16)

        def body(r, carry):
            src = pl.ds(pl.multiple_of(r * 64, 64), 64)
            dst = pl.ds(pl.multiple_of(HALO + r * 64, 16), 64)
            xn_ref[dst, :] = _rms(h_ref[src, :], gpre_ref[...]).astype(BF16)
            return carry
        lax.fori_loop(0, TM_FFN // 64, body, 0)
        o_ref[...] = jnp.zeros_like(o_ref)

    xn = xn_ref[...]
    ug_ref[...] = jnp.dot(xn, wg_ref[...], preferred_element_type=F32)
    uu_ref[...] = jnp.dot(xn, wu_ref[...], preferred_element_type=F32)

    def conv(u_ref, w_ref, b_ref):
        acc = b_ref[...] + w_ref[CONV_K - 1:CONV_K, :] * u_ref[HALO:HALO + TM_FFN, :]
        for t in range(1, CONV_K):
            acc = acc + w_ref[CONV_K - 1 - t:CONV_K - t, :] * u_ref[HALO - t:HALO - t + TM_FFN, :]
        return acc

    act = _gelu_tanh(conv(ug_ref, cwg_ref, cbg_ref)) * conv(uu_ref, cwu_ref, cbu_ref)
    o_ref[...] += jnp.dot(act.astype(BF16), wd_ref[...], preferred_element_type=F32)

    @pl.when(c == pl.num_programs(1) - 1)
    def _():
        out = h_ref[...] + _rms(o_ref[...], gpost_ref[...])
        row = pl.program_id(0) * TM_FFN + lax.broadcasted_iota(jnp.int32, (TM_FFN, 1), 0)
        o_ref[...] = jnp.where(row >= ROW_PAD, out, 0.0)


def _ffn(h, gpre, w_up, w_conv, b_conv, w_down, gpost):
    n_fc = D_FF // FC_FFN
    halo_blocks = TM_FFN // HALO
    return pl.pallas_call(
        _ffn_kernel,
        grid=(L_BUF // TM_FFN, n_fc),
        in_specs=[
            pl.BlockSpec((TM_FFN, D_MODEL), lambda i, c: (i, 0)),
            pl.BlockSpec((HALO, D_MODEL), lambda i, c: (jnp.maximum(i * halo_blocks - 1, 0), 0)),
            pl.BlockSpec((1, D_MODEL), lambda i, c: (0, 0)),
            pl.BlockSpec((D_MODEL, FC_FFN), lambda i, c: (0, c)),
            pl.BlockSpec((D_MODEL, FC_FFN), lambda i, c: (0, n_fc + c)),
            pl.BlockSpec((CONV_K, FC_FFN), lambda i, c: (0, c)),
            pl.BlockSpec((CONV_K, FC_FFN), lambda i, c: (0, n_fc + c)),
            pl.BlockSpec((1, FC_FFN), lambda i, c: (0, c)),
            pl.BlockSpec((1, FC_FFN), lambda i, c: (0, n_fc + c)),
            pl.BlockSpec((FC_FFN, D_MODEL), lambda i, c: (c, 0)),
            pl.BlockSpec((1, D_MODEL), lambda i, c: (0, 0)),
        ],
        out_specs=pl.BlockSpec((TM_FFN, D_MODEL), lambda i, c: (i, 0)),
        out_shape=jax.ShapeDtypeStruct((L_BUF, D_MODEL), F32),
        scratch_shapes=[pltpu.VMEM((HALO + TM_FFN, D_MODEL), BF16),
                        pltpu.VMEM((HALO + TM_FFN, FC_FFN), F32),
                        pltpu.VMEM((HALO + TM_FFN, FC_FFN), F32)],
        compiler_params=_cparams(("parallel", "arbitrary")),
        name="conv_ffn",
    )(h, h, gpre, w_up, w_up, w_conv, w_conv, b_conv, b_conv, w_down, gpost)


def _rotate_half_cols(w):
    half = w.shape[-1] // 2
    return jnp.concatenate([-w[..., half:], w[..., :half]], axis=-1)


def _pad_cols(w, width):
    return jnp.pad(w, ((0, 0), (0, width - w.shape[-1])))


def _pack_w_in(w):
    o = np.cumsum([0, MLA_Q_LORA, MLA_KV_LORA, MLA_ROPE, FOX_W, FOX_W, FOX_W, FOX_W, FOX_HEADS])
    c_q, c_kv, k_rope, fq, fk, fv, fg, ff = [w[:, o[n]:o[n + 1]] for n in range(8)]
    tail = jnp.concatenate([_pad_cols(k_rope, LANE), _pad_cols(_rotate_half_cols(k_rope), LANE),
                            _pad_cols(ff, 2 * LANE)], axis=1)
    return jnp.concatenate([c_q, c_kv, fq, fk, fv, fg, tail], axis=1).astype(BF16)


def _pack_w_q_up(w):
    w = w.reshape(MLA_Q_LORA, MLA_HEADS, MLA_NOPE + MLA_ROPE)
    nope, rope = w[..., :MLA_NOPE], w[..., MLA_NOPE:]
    pad = ((0, 0), (0, 0), (0, LANE - MLA_ROPE))
    packed = jnp.concatenate([nope, jnp.pad(rope, pad), jnp.pad(_rotate_half_cols(rope), pad)], axis=-1)
    return packed.reshape(MLA_Q_LORA, MLA_HEADS * Q_HEAD_W).astype(BF16)


def _rope_tables():
    pos = jnp.maximum(jnp.arange(L_BUF, dtype=jnp.int32) - ROW_PAD, 0).astype(F32)
    half = MLA_ROPE // 2
    inv_freq = ROPE_THETA ** (-jnp.arange(half, dtype=F32) / half)
    ang = pos[:, None] * inv_freq[None, :]
    zeros = jnp.zeros((L_BUF, LANE - MLA_ROPE), F32)
    cos, sin = jnp.cos(ang), jnp.sin(ang)
    return (jnp.concatenate([cos, cos, zeros], axis=1), jnp.concatenate([sin, sin, zeros], axis=1))


def kernel(x, meta_tokens, ln_mix_pre, w_in, b_forget, g_q_latent, g_kv_latent, w_q_up, w_kv_up,
           g_fox_q, g_fox_k, w_out, ln_mix_post, ln_ffn_pre, w_ffn_up, w_ffn_conv, b_ffn_conv,
           w_ffn_down, ln_ffn_post):
    assert x.shape == (1, SEQ, D_MODEL), x.shape
    h = jnp.concatenate([jnp.zeros((ROW_PAD, D_MODEL), x.dtype), meta_tokens.astype(x.dtype), x[0]], axis=0)
    cos_t, sin_t = _rope_tables()
    tri = (lax.broadcasted_iota(jnp.int32, (Q_TILE, Q_TILE), 0)
           >= lax.broadcasted_iota(jnp.int32, (Q_TILE, Q_TILE), 1)).astype(BF16)
    row2d = lambda v: v.reshape(1, -1).astype(F32)

    for l in range(DEPTH):
        proj = _proj_in(h, row2d(ln_mix_pre[l]), _pack_w_in(w_in[l]))
        q, k, v, qf, kf, vf, c, ct = _prep(
            proj, cos_t, sin_t, row2d(g_q_latent[l]), row2d(g_kv_latent[l]),
            row2d(g_fox_q[l]), row2d(g_fox_k[l]), _pad_cols(row2d(b_forget[l]), LANE),
            _pack_w_q_up(w_q_up[l]), w_kv_up[l].astype(BF16), tri)
        a = _mla_attn(q, k, v)
        b = _fox_attn(qf, kf, vf, c, ct.reshape(FOX_HEADS, N_AT, 1, ATT_TILE), proj)
        w_o = w_out[l].astype(BF16)
        h = _mix_out(a, b, w_o[:MLA_HEADS * MLA_V], w_o[MLA_HEADS * MLA_V:], h, row2d(ln_mix_post[l]))
        h = _ffn(h, row2d(ln_ffn_pre[l]), w_ffn_up[l].astype(BF16), w_ffn_conv[l].astype(F32),
                 row2d(b_ffn_conv[l]), w_ffn_down[l].astype(BF16), row2d(ln_ffn_post[l]))

    return h[ROW_PAD + N_META:][None]
```

```python
import functools

import jax
import jax.numpy as jnp
import numpy as np
from jax import lax
from jax.experimental import pallas as pl
from jax.experimental.pallas import tpu as pltpu

F32 = jnp.float32
BF16 = jnp.bfloat16

D_MODEL = 2048
SEQ = 8192
DEPTH = 4
CHUNK = 64
CHUNK_SHIFT = 6
N_META = 16
MLA_HEADS = 8
MLA_Q_LORA = 512
MLA_KV_LORA = 512
MLA_NOPE = 128
MLA_ROPE = 64
MLA_V = 128
ROPE_THETA = 10000.0
FOX_HEADS = 8
FOX_HD = 128
FOX_W = FOX_HEADS * FOX_HD
D_FF = 5632
CONV_K = 3
EPS = 1e-6
NEG = -1e30

LANE = 128
Q_TILE = 256
ROW_PAD = Q_TILE - N_META
L_BUF = ROW_PAD + N_META + SEQ
ATT_TILE = 768
ROW_BLK = 256
N_AT = L_BUF // ATT_TILE

LOG2E = float(np.log2(np.e))
MLA_SCALE = (MLA_NOPE + MLA_ROPE) ** -0.5 * LOG2E
FOX_SCALE = FOX_HD ** -0.5 * LOG2E

TAIL_W = 4 * LANE
IN_PACKED = MLA_Q_LORA + MLA_KV_LORA + 4 * FOX_W + TAIL_W
GATE_COL0 = MLA_Q_LORA + MLA_KV_LORA + 3 * FOX_W
Q_HEAD_W = 3 * LANE
QK_W = 2 * LANE

TM_PROJ = 768
TN_PROJ = 512
TM_OUT = 384
TM_FFN = 768
FC_FFN = 512
HALO = 16

VMEM_LIMIT = 56 * 1024 * 1024


def _rms(x, g):
    ms = jnp.mean(x * x, axis=-1, keepdims=True)
    return x * lax.rsqrt(ms + EPS) * g


def _split3(x):
    hi = x.astype(BF16).astype(F32)
    r = x - hi
    mid = r.astype(BF16).astype(F32)
    return hi, mid, r - mid


def _cparams(sem):
    return pltpu.CompilerParams(dimension_semantics=sem, vmem_limit_bytes=VMEM_LIMIT)


def _proj_in_kernel(x_ref, g_ref, w_ref, o_ref, xn_ref):
    @pl.when(pl.program_id(1) == 0)
    def _():
        def body(r, carry):
            rows = pl.ds(pl.multiple_of(r * 64, 64), 64)
            xn_ref[rows, :] = _rms(x_ref[rows, :], g_ref[...]).astype(BF16)
            return carry
        lax.fori_loop(0, TM_PROJ // 64, body, 0)

    o_ref[...] = jnp.dot(xn_ref[...], w_ref[...], preferred_element_type=F32)


def _proj_in(h, g, w):
    return pl.pallas_call(
        _proj_in_kernel,
        grid=(L_BUF // TM_PROJ, IN_PACKED // TN_PROJ),
        in_specs=[
            pl.BlockSpec((TM_PROJ, D_MODEL), lambda i, j: (i, 0)),
            pl.BlockSpec((1, D_MODEL), lambda i, j: (0, 0)),
            pl.BlockSpec((D_MODEL, TN_PROJ), lambda i, j: (0, j)),
        ],
        out_specs=pl.BlockSpec((TM_PROJ, TN_PROJ), lambda i, j: (i, j)),
        out_shape=jax.ShapeDtypeStruct((L_BUF, IN_PACKED), F32),
        scratch_shapes=[pltpu.VMEM((TM_PROJ, D_MODEL), BF16)],
        compiler_params=_cparams(("parallel", "arbitrary")),
        name="proj_in",
    )(h, g, w)


def _prep_kernel(cq_ref, ckv_ref, fq_ref, fk_ref, fv_ref, tail_ref, cos_ref, sin_ref,
                 gql_ref, gkvl_ref, gfq_ref, gfk_ref, bf_ref, wq_ref, wkv_ref, tri_ref,
                 q_ref, k_ref, v_ref, qf_ref, kf_ref, vf_ref, c_ref, carry_ref):
    i = pl.program_id(0)
    lane = lax.broadcasted_iota(jnp.int32, (Q_TILE, LANE), 1)
    is_pad = (i * Q_TILE + lax.broadcasted_iota(jnp.int32, (Q_TILE, 1), 0)) < ROW_PAD
    cos = cos_ref[...]
    sin = sin_ref[...]
    q_flag = jnp.where(lane == MLA_ROPE, 1.0, 0.0)
    k_flag = jnp.where((lane == MLA_ROPE) & is_pad, NEG, 0.0)
    kr = (tail_ref[:, 0:LANE] * cos + tail_ref[:, LANE:2 * LANE] * sin + k_flag).astype(BF16)
    cqn = _rms(cq_ref[...], gql_ref[...]).astype(BF16)
    ckvn = _rms(ckv_ref[...], gkvl_ref[...]).astype(BF16)
    for h in range(MLA_HEADS):
        qh = jnp.dot(cqn, wq_ref[:, Q_HEAD_W * h:Q_HEAD_W * (h + 1)], preferred_element_type=F32)
        qr = qh[:, LANE:2 * LANE] * cos + qh[:, 2 * LANE:3 * LANE] * sin
        q_ref[h, :, 0:LANE] = (qh[:, 0:LANE] * MLA_SCALE).astype(BF16)
        q_ref[h, :, LANE:QK_W] = (qr * MLA_SCALE + q_flag).astype(BF16)
        kvh = jnp.dot(ckvn, wkv_ref[:, 2 * LANE * h:2 * LANE * (h + 1)], preferred_element_type=F32)
        k_ref[h, :, 0:LANE] = kvh[:, 0:LANE].astype(BF16)
        k_ref[h, :, LANE:QK_W] = kr
        v_ref[h] = kvh[:, LANE:2 * LANE].astype(BF16)

    z = tail_ref[:, 2 * LANE:3 * LANE] + bf_ref[...]
    logf = (jnp.minimum(z, 0.0) - jnp.log(1.0 + jnp.exp(-jnp.abs(z)))) * LOG2E
    hi, mid, lo = _split3(logf)
    tri = tri_ref[...]
    cs = (jnp.dot(tri, hi.astype(BF16), preferred_element_type=F32)
          + jnp.dot(tri, mid.astype(BF16), preferred_element_type=F32)
          + jnp.dot(tri, lo.astype(BF16), preferred_element_type=F32))

    @pl.when(i == 0)
    def _():
        carry_ref[...] = jnp.zeros_like(carry_ref)

    c = cs + carry_ref[0:1, :]
    c_ref[...] = c
    carry_ref[...] = jnp.broadcast_to(c[Q_TILE - 1:Q_TILE, :], carry_ref.shape)

    qf_ext = jnp.where(lane < 3, -1.0, 0.0).astype(BF16)
    for h in range(FOX_HEADS):
        cols = slice(FOX_HD * h, FOX_HD * (h + 1))
        qf_ref[h, :, 0:FOX_HD] = (_rms(fq_ref[:, cols], gfq_ref[...]) * FOX_SCALE).astype(BF16)
        qf_ref[h, :, FOX_HD:QK_W] = qf_ext
        kf_ref[h, :, 0:FOX_HD] = _rms(fk_ref[:, cols], gfk_ref[...]).astype(BF16)
        ck = jnp.sum(jnp.where(lane == h, c, 0.0), axis=-1, keepdims=True)
        ck_hi, ck_mid, ck_lo = _split3(jnp.where(is_pad, -NEG, ck))
        ext = jnp.where(lane == 0, ck_hi, jnp.where(lane == 1, ck_mid, jnp.where(lane == 2, ck_lo, 0.0)))
        kf_ref[h, :, FOX_HD:QK_W] = ext.astype(BF16)
        vf_ref[h] = fv_ref[:, cols].astype(BF16)


def _prep(proj, cos_t, sin_t, gql, gkvl, gfq, gfk, bf_pad, wq, wkv, tri):
    tm = Q_TILE
    row = lambda i: (i, 0)
    const = lambda i: (0, 0)
    head_out = lambda w: pl.BlockSpec((MLA_HEADS, tm, w), lambda i: (0, i, 0))
    return pl.pallas_call(
        _prep_kernel,
        grid=(L_BUF // tm,),
        in_specs=[
            pl.BlockSpec((tm, MLA_Q_LORA), lambda i: (i, 0)),
            pl.BlockSpec((tm, MLA_KV_LORA), lambda i: (i, 1)),
            pl.BlockSpec((tm, FOX_W), lambda i: (i, 1)),
            pl.BlockSpec((tm, FOX_W), lambda i: (i, 2)),
            pl.BlockSpec((tm, FOX_W), lambda i: (i, 3)),
            pl.BlockSpec((tm, TAIL_W), lambda i: (i, IN_PACKED // TAIL_W - 1)),
            pl.BlockSpec((tm, LANE), row),
            pl.BlockSpec((tm, LANE), row),
            pl.BlockSpec((1, MLA_Q_LORA), const),
            pl.BlockSpec((1, MLA_KV_LORA), const),
            pl.BlockSpec((1, FOX_HD), const),
            pl.BlockSpec((1, FOX_HD), const),
            pl.BlockSpec((1, LANE), const),
            pl.BlockSpec((MLA_Q_LORA, MLA_HEADS * Q_HEAD_W), const),
            pl.BlockSpec((MLA_KV_LORA, MLA_HEADS * 2 * LANE), const),
            pl.BlockSpec((tm, tm), const),
        ],
        out_specs=[
            head_out(QK_W), head_out(QK_W), head_out(MLA_V),
            head_out(QK_W), head_out(QK_W), head_out(FOX_HD),
            pl.BlockSpec((tm, LANE), row),
        ],
        out_shape=[
            jax.ShapeDtypeStruct((MLA_HEADS, L_BUF, QK_W), BF16),
            jax.ShapeDtypeStruct((MLA_HEADS, L_BUF, QK_W), BF16),
            jax.ShapeDtypeStruct((MLA_HEADS, L_BUF, MLA_V), BF16),
            jax.ShapeDtypeStruct((FOX_HEADS, L_BUF, QK_W), BF16),
            jax.ShapeDtypeStruct((FOX_HEADS, L_BUF, QK_W), BF16),
            jax.ShapeDtypeStruct((FOX_HEADS, L_BUF, FOX_HD), BF16),
            jax.ShapeDtypeStruct((L_BUF, LANE), F32),
        ],
        scratch_shapes=[pltpu.VMEM((8, LANE), F32)],
        compiler_params=_cparams(("arbitrary",)),
        name="attn_prep",
    )(proj, proj, proj, proj, proj, proj, cos_t, sin_t, gql, gkvl, gfq, gfk, bf_pad, wq, wkv, tri)


def _scores(q, k):
    return lax.dot_general(q, k, (((1,), (1,)), ((), ())), preferred_element_type=F32)


def _flash_tile(i, q_ref, k_ref, v_ref, m_sc, acc_sc, p_sc, alpha_sc, p_pend, alpha_pend,
                diag_mask, row_shift=None):
    m_sc[...] = jnp.full_like(m_sc, NEG)
    acc_sc[...] = jnp.zeros_like(acc_sc)

    def chunk_start(j):
        return pl.multiple_of(j * ATT_TILE, ATT_TILE)

    def flush(j):
        v = v_ref[0, pl.ds(chunk_start(j), ATT_TILE), :]
        v_ones = jnp.concatenate([v, jnp.ones_like(v)], axis=1)
        for rb in range(ATT_TILE // ROW_BLK):
            rows = slice(rb * ROW_BLK, (rb + 1) * ROW_BLK)
            pv = jnp.dot(p_pend[rows, :], v_ones, preferred_element_type=F32)
            acc_sc[rows, :] = acc_sc[rows, :] * jnp.tile(alpha_pend[rows, :], (1, 2)) + pv

    def make_pending():
        p_pend[...] = p_sc[...]
        alpha_pend[...] = alpha_sc[...]

    def scores_to_p(j, diagonal):
        for rb in range(ATT_TILE // ROW_BLK):
            rows = slice(rb * ROW_BLK, (rb + 1) * ROW_BLK)
            ncols = (rb + 1) * ROW_BLK if diagonal else ATT_TILE
            s = _scores(q_ref[0, rows, :], k_ref[0, pl.ds(chunk_start(j), ncols), :])
            if diagonal:
                r = lax.broadcasted_iota(jnp.int32, s.shape, 0) + rb * ROW_BLK
                c = lax.broadcasted_iota(jnp.int32, s.shape, 1)
                s = jnp.where(diag_mask(r, c), s, NEG)
                if ncols < ATT_TILE:
                    p_sc[rows, ncols:] = jnp.zeros((ROW_BLK, ATT_TILE - ncols), BF16)
            m_prev = m_sc[rows, :]
            m_chunk = jnp.max(s, axis=-1, keepdims=True)
            if row_shift is not None:
                m_chunk = m_chunk + row_shift[rows, :]
            m_new = jnp.maximum(m_prev, m_chunk)
            alpha_sc[rows, :] = jnp.exp2(m_prev - m_new)
            sub = m_new if row_shift is None else m_new - row_shift[rows, :]
            p_sc[rows, 0:ncols] = jnp.exp2(s - jnp.tile(sub, (1, ncols // LANE))).astype(BF16)
            m_sc[rows, :] = m_new

    scores_to_p(i, diagonal=True)
    make_pending()

    def body(j, pending):
        flush(pending)
        scores_to_p(j, diagonal=False)
        make_pending()
        return j
    flush(lax.fori_loop(0, i, body, i))
    return acc_sc[:, 0:LANE] / acc_sc[:, LANE:2 * LANE]


_ATT_SCRATCH = [pltpu.VMEM((ATT_TILE, LANE), F32), pltpu.VMEM((ATT_TILE, 2 * LANE), F32),
                pltpu.VMEM((ATT_TILE, ATT_TILE), BF16), pltpu.VMEM((ATT_TILE, LANE), F32),
                pltpu.VMEM((ATT_TILE, ATT_TILE), BF16), pltpu.VMEM((ATT_TILE, LANE), F32)]


def _mla_attn_kernel(q_ref, k_ref, v_ref, o_ref, *scratch):
    mask = lambda r, c: jnp.right_shift(c, CHUNK_SHIFT) <= jnp.right_shift(r, CHUNK_SHIFT)
    out = _flash_tile(pl.program_id(1), q_ref, k_ref, v_ref, *scratch, diag_mask=mask)
    o_ref[...] = out.astype(o_ref.dtype)


def _mla_attn(q, k, v):
    return pl.pallas_call(
        _mla_attn_kernel,
        grid=(MLA_HEADS, N_AT),
        in_specs=[
            pl.BlockSpec((1, ATT_TILE, QK_W), lambda h, i: (h, i, 0)),
            pl.BlockSpec((1, L_BUF, QK_W), lambda h, i: (h, 0, 0)),
            pl.BlockSpec((1, L_BUF, MLA_V), lambda h, i: (h, 0, 0)),
        ],
        out_specs=pl.BlockSpec((ATT_TILE, MLA_V), lambda h, i: (i, h)),
        out_shape=jax.ShapeDtypeStruct((L_BUF, MLA_HEADS * MLA_V), BF16),
        scratch_shapes=_ATT_SCRATCH,
        compiler_params=_cparams(("parallel", "arbitrary")),
        name="mla_attn",
    )(q, k, v)


def _fox_attn_kernel(q_ref, k_ref, v_ref, c_ref, gate_ref, o_ref, *scratch):
    lane = lax.broadcasted_iota(jnp.int32, (ATT_TILE, LANE), 1)
    cq = jnp.sum(jnp.where(lane == pl.program_id(0), c_ref[...], 0.0), axis=-1, keepdims=True)
    out = _flash_tile(pl.program_id(1), q_ref, k_ref, v_ref, *scratch,
                      diag_mask=lambda r, c: c <= r, row_shift=cq)
    o_ref[...] = (out * jax.nn.sigmoid(gate_ref[...])).astype(o_ref.dtype)


def _fox_attn(q, k, v, c, proj):
    return pl.pallas_call(
        _fox_attn_kernel,
        grid=(FOX_HEADS, N_AT),
        in_specs=[
            pl.BlockSpec((1, ATT_TILE, QK_W), lambda h, i: (h, i, 0)),
            pl.BlockSpec((1, L_BUF, QK_W), lambda h, i: (h, 0, 0)),
            pl.BlockSpec((1, L_BUF, FOX_HD), lambda h, i: (h, 0, 0)),
            pl.BlockSpec((ATT_TILE, LANE), lambda h, i: (i, 0)),
            pl.BlockSpec((ATT_TILE, FOX_HD), lambda h, i: (i, GATE_COL0 // FOX_HD + h)),
        ],
        out_specs=pl.BlockSpec((ATT_TILE, FOX_HD), lambda h, i: (i, h)),
        out_shape=jax.ShapeDtypeStruct((L_BUF, FOX_W), BF16),
        scratch_shapes=_ATT_SCRATCH,
        compiler_params=_cparams(("parallel", "arbitrary")),
        name="fox_attn",
    )(q, k, v, c, proj)


def _mix_out_kernel(a_ref, b_ref, wa_ref, wb_ref, h_ref, g_ref, o_ref):
    mix = (jnp.dot(a_ref[...], wa_ref[...], preferred_element_type=F32)
           + jnp.dot(b_ref[...], wb_ref[...], preferred_element_type=F32))
    out = h_ref[...] + _rms(mix, g_ref[...])
    row = pl.program_id(0) * TM_OUT + lax.broadcasted_iota(jnp.int32, (TM_OUT, 1), 0)
    o_ref[...] = jnp.where(row >= ROW_PAD, out, 0.0)


def _mix_out(a, b, wa, wb, h, g):
    return pl.pallas_call(
        _mix_out_kernel,
        grid=(L_BUF // TM_OUT,),
        in_specs=[
            pl.BlockSpec((TM_OUT, MLA_HEADS * MLA_V), lambda i: (i, 0)),
            pl.BlockSpec((TM_OUT, FOX_W), lambda i: (i, 0)),
            pl.BlockSpec((MLA_HEADS * MLA_V, D_MODEL), lambda i: (0, 0)),
            pl.BlockSpec((FOX_W, D_MODEL), lambda i: (0, 0)),
            pl.BlockSpec((TM_OUT, D_MODEL), lambda i: (i, 0)),
            pl.BlockSpec((1, D_MODEL), lambda i: (0, 0)),
        ],
        out_specs=pl.BlockSpec((TM_OUT, D_MODEL), lambda i: (i, 0)),
        out_shape=jax.ShapeDtypeStruct((L_BUF, D_MODEL), F32),
        compiler_params=_cparams(("parallel",)),
        name="mix_out",
    )(a, b, wa, wb, h, g)


def _gelu_tanh(x):
    return 0.5 * x * (1.0 + jnp.tanh(np.sqrt(2.0 / np.pi).astype(np.float32) * (x + 0.044715 * (x * x * x))))


def _ffn_kernel(h_ref, halo_ref, gpre_ref, wg_ref, wu_ref, cwg_ref, cwu_ref, cbg_ref, cbu_ref,
                wd_ref, gpost_ref, o_ref, xn_ref, ug_ref, uu_ref):
    c = pl.program_id(1)

    @pl.when(c == 0)
    def _():
        xn_ref[0:HALO, :] = _rms(halo_ref[...], gpre_ref[...]).astype(BF16)

        def body(r, carry):
            src = pl.ds(pl.multiple_of(r * 64, 64), 64)
            dst = pl.ds(pl.multiple_of(HALO + r * 64, 16), 64)
            xn_ref[dst, :] = _rms(h_ref[src, :], gpre_ref[...]).astype(BF16)
            return carry
        lax.fori_loop(0, TM_FFN // 64, body, 0)
        o_ref[...] = jnp.zeros_like(o_ref)

    xn = xn_ref[...]
    ug_ref[...] = jnp.dot(xn, wg_ref[...], preferred_element_type=F32)
    uu_ref[...] = jnp.dot(xn, wu_ref[...], preferred_element_type=F32)

    def conv(u_ref, w_ref, b_ref):
        acc = b_ref[...] + w_ref[CONV_K - 1:CONV_K, :] * u_ref[HALO:HALO + TM_FFN, :]
        for t in range(1, CONV_K):
            acc = acc + w_ref[CONV_K - 1 - t:CONV_K - t, :] * u_ref[HALO - t:HALO - t + TM_FFN, :]
        return acc

    act = _gelu_tanh(conv(ug_ref, cwg_ref, cbg_ref)) * conv(uu_ref, cwu_ref, cbu_ref)
    o_ref[...] += jnp.dot(act.astype(BF16), wd_ref[...], preferred_element_type=F32)

    @pl.when(c == pl.num_programs(1) - 1)
    def _():
        out = h_ref[...] + _rms(o_ref[...], gpost_ref[...])
        row = pl.program_id(0) * TM_FFN + lax.broadcasted_iota(jnp.int32, (TM_FFN, 1), 0)
        o_ref[...] = jnp.where(row >= ROW_PAD, out, 0.0)


def _ffn(h, gpre, w_up, w_conv, b_conv, w_down, gpost):
    n_fc = D_FF // FC_FFN
    halo_blocks = TM_FFN // HALO
    return pl.pallas_call(
        _ffn_kernel,
        grid=(L_BUF // TM_FFN, n_fc),
        in_specs=[
            pl.BlockSpec((TM_FFN, D_MODEL), lambda i, c: (i, 0)),
            pl.BlockSpec((HALO, D_MODEL), lambda i, c: (jnp.maximum(i * halo_blocks - 1, 0), 0)),
            pl.BlockSpec((1, D_MODEL), lambda i, c: (0, 0)),
            pl.BlockSpec((D_MODEL, FC_FFN), lambda i, c: (0, c)),
            pl.BlockSpec((D_MODEL, FC_FFN), lambda i, c: (0, n_fc + c)),
            pl.BlockSpec((CONV_K, FC_FFN), lambda i, c: (0, c)),
            pl.BlockSpec((CONV_K, FC_FFN), lambda i, c: (0, n_fc + c)),
            pl.BlockSpec((1, FC_FFN), lambda i, c: (0, c)),
            pl.BlockSpec((1, FC_FFN), lambda i, c: (0, n_fc + c)),
            pl.BlockSpec((FC_FFN, D_MODEL), lambda i, c: (c, 0)),
            pl.BlockSpec((1, D_MODEL), lambda i, c: (0, 0)),
        ],
        out_specs=pl.BlockSpec((TM_FFN, D_MODEL), lambda i, c: (i, 0)),
        out_shape=jax.ShapeDtypeStruct((L_BUF, D_MODEL), F32),
        scratch_shapes=[pltpu.VMEM((HALO + TM_FFN, D_MODEL), BF16),
                        pltpu.VMEM((HALO + TM_FFN, FC_FFN), F32),
                        pltpu.VMEM((HALO + TM_FFN, FC_FFN), F32)],
        compiler_params=_cparams(("parallel", "arbitrary")),
        name="conv_ffn",
    )(h, h, gpre, w_up, w_up, w_conv, w_conv, b_conv, b_conv, w_down, gpost)


def _rotate_half_cols(w):
    half = w.shape[-1] // 2
    return jnp.concatenate([-w[..., half:], w[..., :half]], axis=-1)


def _pad_cols(w, width):
    return jnp.pad(w, ((0, 0), (0, width - w.shape[-1])))


def _pack_w_in(w):
    o = np.cumsum([0, MLA_Q_LORA, MLA_KV_LORA, MLA_ROPE, FOX_W, FOX_W, FOX_W, FOX_W, FOX_HEADS])
    c_q, c_kv, k_rope, fq, fk, fv, fg, ff = [w[:, o[n]:o[n + 1]] for n in range(8)]
    tail = jnp.concatenate([_pad_cols(k_rope, LANE), _pad_cols(_rotate_half_cols(k_rope), LANE),
                            _pad_cols(ff, 2 * LANE)], axis=1)
    return jnp.concatenate([c_q, c_kv, fq, fk, fv, fg, tail], axis=1).astype(BF16)


def _pack_w_q_up(w):
    w = w.reshape(MLA_Q_LORA, MLA_HEADS, MLA_NOPE + MLA_ROPE)
    nope, rope = w[..., :MLA_NOPE], w[..., MLA_NOPE:]
    pad = ((0, 0), (0, 0), (0, LANE - MLA_ROPE))
    packed = jnp.concatenate([nope, jnp.pad(rope, pad), jnp.pad(_rotate_half_cols(rope), pad)], axis=-1)
    return packed.reshape(MLA_Q_LORA, MLA_HEADS * Q_HEAD_W).astype(BF16)


def _rope_tables():
    pos = jnp.maximum(jnp.arange(L_BUF, dtype=jnp.int32) - ROW_PAD, 0).astype(F32)
    half = MLA_ROPE // 2
    inv_freq = ROPE_THETA ** (-jnp.arange(half, dtype=F32) / half)
    ang = pos[:, None] * inv_freq[None, :]
    zeros = jnp.zeros((L_BUF, LANE - MLA_ROPE), F32)
    cos, sin = jnp.cos(ang), jnp.sin(ang)
    return (jnp.concatenate([cos, cos, zeros], axis=1), jnp.concatenate([sin, sin, zeros], axis=1))


def kernel(x, meta_tokens, ln_mix_pre, w_in, b_forget, g_q_latent, g_kv_latent, w_q_up, w_kv_up,
           g_fox_q, g_fox_k, w_out, ln_mix_post, ln_ffn_pre, w_ffn_up, w_ffn_conv, b_ffn_conv,
           w_ffn_down, ln_ffn_post):
    assert x.shape == (1, SEQ, D_MODEL), x.shape
    h = jnp.concatenate([jnp.zeros((ROW_PAD, D_MODEL), x.dtype), meta_tokens.astype(x.dtype), x[0]], axis=0)
    cos_t, sin_t = _rope_tables()
    tri = (lax.broadcasted_iota(jnp.int32, (Q_TILE, Q_TILE), 0)
           >= lax.broadcasted_iota(jnp.int32, (Q_TILE, Q_TILE), 1)).astype(BF16)
    row2d = lambda v: v.reshape(1, -1).astype(F32)

    for l in range(DEPTH):
        proj = _proj_in(h, row2d(ln_mix_pre[l]), _pack_w_in(w_in[l]))
        q, k, v, qf, kf, vf, c = _prep(
            proj, cos_t, sin_t, row2d(g_q_latent[l]), row2d(g_kv_latent[l]),
            row2d(g_fox_q[l]), row2d(g_fox_k[l]), _pad_cols(row2d(b_forget[l]), LANE),
            _pack_w_q_up(w_q_up[l]), w_kv_up[l].astype(BF16), tri)
        a = _mla_attn(q, k, v)
        b = _fox_attn(qf, kf, vf, c, proj)
        w_o = w_out[l].astype(BF16)
        h = _mix_out(a, b, w_o[:MLA_HEADS * MLA_V], w_o[MLA_HEADS * MLA_V:], h, row2d(ln_mix_post[l]))
        h = _ffn(h, row2d(ln_ffn_pre[l]), w_ffn_up[l].astype(BF16), w_ffn_conv[l].astype(F32),
                 row2d(b_ffn_conv[l]), w_ffn_down[l].astype(BF16), row2d(ln_ffn_post[l]))

    return h[ROW_PAD + N_META:][None]
```

```python
import functools

import jax
import jax.numpy as jnp
import numpy as np
from jax import lax
from jax.experimental import pallas as pl
from jax.experimental.pallas import tpu as pltpu

F32 = jnp.float32
BF16 = jnp.bfloat16

D_MODEL = 2048
SEQ = 8192
DEPTH = 4
CHUNK = 64
CHUNK_SHIFT = 6
N_META = 16
MLA_HEADS = 8
MLA_Q_LORA = 512
MLA_KV_LORA = 512
MLA_NOPE = 128
MLA_ROPE = 64
MLA_V = 128
ROPE_THETA = 10000.0
FOX_HEADS = 8
FOX_HD = 128
FOX_W = FOX_HEADS * FOX_HD
D_FF = 5632
CONV_K = 3
EPS = 1e-6
NEG = -1e30

LANE = 128
Q_TILE = 256
ROW_PAD = Q_TILE - N_META
L_BUF = ROW_PAD + N_META + SEQ
ATT_TILE = 768
ROW_BLK = 256
N_AT = L_BUF // ATT_TILE

LOG2E = float(np.log2(np.e))
MLA_SCALE = (MLA_NOPE + MLA_ROPE) ** -0.5 * LOG2E
FOX_SCALE = FOX_HD ** -0.5 * LOG2E

TAIL_W = 4 * LANE
IN_PACKED = MLA_Q_LORA + MLA_KV_LORA + 4 * FOX_W + TAIL_W
GATE_COL0 = MLA_Q_LORA + MLA_KV_LORA + 3 * FOX_W
Q_HEAD_W = 3 * LANE
QK_W = 2 * LANE

TM_PROJ = 768
TN_PROJ = 512
TM_OUT = 384
TM_FFN = 768
FC_FFN = 512
HALO = 16

VMEM_LIMIT = 56 * 1024 * 1024


def _rms(x, g):
    ms = jnp.mean(x * x, axis=-1, keepdims=True)
    return x * lax.rsqrt(ms + EPS) * g


def _split3(x):
    hi = x.astype(BF16).astype(F32)
    r = x - hi
    mid = r.astype(BF16).astype(F32)
    return hi, mid, r - mid


def _cparams(sem, flags=None):
    return pltpu.CompilerParams(dimension_semantics=sem, vmem_limit_bytes=VMEM_LIMIT, flags=flags)


ATT_FLAGS = None


def _proj_in_kernel(x_ref, g_ref, w_ref, o_ref, xn_ref):
    @pl.when(pl.program_id(1) == 0)
    def _():
        def body(r, carry):
            rows = pl.ds(pl.multiple_of(r * 64, 64), 64)
            xn_ref[rows, :] = _rms(x_ref[rows, :], g_ref[...]).astype(BF16)
            return carry
        lax.fori_loop(0, TM_PROJ // 64, body, 0)

    o_ref[...] = jnp.dot(xn_ref[...], w_ref[...], preferred_element_type=F32)


def _proj_in(h, g, w):
    return pl.pallas_call(
        _proj_in_kernel,
        grid=(L_BUF // TM_PROJ, IN_PACKED // TN_PROJ),
        in_specs=[
            pl.BlockSpec((TM_PROJ, D_MODEL), lambda i, j: (i, 0)),
            pl.BlockSpec((1, D_MODEL), lambda i, j: (0, 0)),
            pl.BlockSpec((D_MODEL, TN_PROJ), lambda i, j: (0, j)),
        ],
        out_specs=pl.BlockSpec((TM_PROJ, TN_PROJ), lambda i, j: (i, j)),
        out_shape=jax.ShapeDtypeStruct((L_BUF, IN_PACKED), F32),
        scratch_shapes=[pltpu.VMEM((TM_PROJ, D_MODEL), BF16)],
        compiler_params=_cparams(("parallel", "arbitrary")),
        name="proj_in",
    )(h, g, w)


def _prep_kernel(cq_ref, ckv_ref, fq_ref, fk_ref, fv_ref, tail_ref, cos_ref, sin_ref,
                 gql_ref, gkvl_ref, gfq_ref, gfk_ref, bf_ref, wq_ref, wkv_ref, tri_ref,
                 q_ref, k_ref, v_ref, qf_ref, kf_ref, vf_ref, c_ref, carry_ref):
    i = pl.program_id(0)
    lane = lax.broadcasted_iota(jnp.int32, (Q_TILE, LANE), 1)
    is_pad = (i * Q_TILE + lax.broadcasted_iota(jnp.int32, (Q_TILE, 1), 0)) < ROW_PAD
    cos = cos_ref[...]
    sin = sin_ref[...]
    q_flag = jnp.where(lane == MLA_ROPE, 1.0, 0.0)
    k_flag = jnp.where((lane == MLA_ROPE) & is_pad, NEG, 0.0)
    kr = (tail_ref[:, 0:LANE] * cos + tail_ref[:, LANE:2 * LANE] * sin + k_flag).astype(BF16)
    cqn = _rms(cq_ref[...], gql_ref[...]).astype(BF16)
    ckvn = _rms(ckv_ref[...], gkvl_ref[...]).astype(BF16)
    for h in range(MLA_HEADS):
        qh = jnp.dot(cqn, wq_ref[:, Q_HEAD_W * h:Q_HEAD_W * (h + 1)], preferred_element_type=F32)
        qr = qh[:, LANE:2 * LANE] * cos + qh[:, 2 * LANE:3 * LANE] * sin
        q_ref[h, :, 0:LANE] = (qh[:, 0:LANE] * MLA_SCALE).astype(BF16)
        q_ref[h, :, LANE:QK_W] = (qr * MLA_SCALE + q_flag).astype(BF16)
        kvh = jnp.dot(ckvn, wkv_ref[:, 2 * LANE * h:2 * LANE * (h + 1)], preferred_element_type=F32)
        k_ref[h, :, 0:LANE] = kvh[:, 0:LANE].astype(BF16)
        k_ref[h, :, LANE:QK_W] = kr
        v_ref[h] = kvh[:, LANE:2 * LANE].astype(BF16)

    z = tail_ref[:, 2 * LANE:3 * LANE] + bf_ref[...]
    logf = (jnp.minimum(z, 0.0) - jnp.log(1.0 + jnp.exp(-jnp.abs(z)))) * LOG2E
    hi, mid, lo = _split3(logf)
    tri = tri_ref[...]
    cs = (jnp.dot(tri, hi.astype(BF16), preferred_element_type=F32)
          + jnp.dot(tri, mid.astype(BF16), preferred_element_type=F32)
          + jnp.dot(tri, lo.astype(BF16), preferred_element_type=F32))

    @pl.when(i == 0)
    def _():
        carry_ref[...] = jnp.zeros_like(carry_ref)

    c = cs + carry_ref[0:1, :]
    c_ref[...] = c
    carry_ref[...] = jnp.broadcast_to(c[Q_TILE - 1:Q_TILE, :], carry_ref.shape)

    qf_ext = jnp.where(lane < 3, -1.0, 0.0).astype(BF16)
    for h in range(FOX_HEADS):
        cols = slice(FOX_HD * h, FOX_HD * (h + 1))
        qf_ref[h, :, 0:FOX_HD] = (_rms(fq_ref[:, cols], gfq_ref[...]) * FOX_SCALE).astype(BF16)
        qf_ref[h, :, FOX_HD:QK_W] = qf_ext
        kf_ref[h, :, 0:FOX_HD] = _rms(fk_ref[:, cols], gfk_ref[...]).astype(BF16)
        ck = jnp.sum(jnp.where(lane == h, c, 0.0), axis=-1, keepdims=True)
        ck_hi, ck_mid, ck_lo = _split3(jnp.where(is_pad, -NEG, ck))
        ext = jnp.where(lane == 0, ck_hi, jnp.where(lane == 1, ck_mid, jnp.where(lane == 2, ck_lo, 0.0)))
        kf_ref[h, :, FOX_HD:QK_W] = ext.astype(BF16)
        vf_ref[h] = fv_ref[:, cols].astype(BF16)


def _prep(proj, cos_t, sin_t, gql, gkvl, gfq, gfk, bf_pad, wq, wkv, tri):
    tm = Q_TILE
    row = lambda i: (i, 0)
    const = lambda i: (0, 0)
    head_out = lambda w: pl.BlockSpec((MLA_HEADS, tm, w), lambda i: (0, i, 0))
    return pl.pallas_call(
        _prep_kernel,
        grid=(L_BUF // tm,),
        in_specs=[
            pl.BlockSpec((tm, MLA_Q_LORA), lambda i: (i, 0)),
            pl.BlockSpec((tm, MLA_KV_LORA), lambda i: (i, 1)),
            pl.BlockSpec((tm, FOX_W), lambda i: (i, 1)),
            pl.BlockSpec((tm, FOX_W), lambda i: (i, 2)),
            pl.BlockSpec((tm, FOX_W), lambda i: (i, 3)),
            pl.BlockSpec((tm, TAIL_W), lambda i: (i, IN_PACKED // TAIL_W - 1)),
            pl.BlockSpec((tm, LANE), row),
            pl.BlockSpec((tm, LANE), row),
            pl.BlockSpec((1, MLA_Q_LORA), const),
            pl.BlockSpec((1, MLA_KV_LORA), const),
            pl.BlockSpec((1, FOX_HD), const),
            pl.BlockSpec((1, FOX_HD), const),
            pl.BlockSpec((1, LANE), const),
            pl.BlockSpec((MLA_Q_LORA, MLA_HEADS * Q_HEAD_W), const),
            pl.BlockSpec((MLA_KV_LORA, MLA_HEADS * 2 * LANE), const),
            pl.BlockSpec((tm, tm), const),
        ],
        out_specs=[
            head_out(QK_W), head_out(QK_W), head_out(MLA_V),
            head_out(QK_W), head_out(QK_W), head_out(FOX_HD),
            pl.BlockSpec((tm, LANE), row),
        ],
        out_shape=[
            jax.ShapeDtypeStruct((MLA_HEADS, L_BUF, QK_W), BF16),
            jax.ShapeDtypeStruct((MLA_HEADS, L_BUF, QK_W), BF16),
            jax.ShapeDtypeStruct((MLA_HEADS, L_BUF, MLA_V), BF16),
            jax.ShapeDtypeStruct((FOX_HEADS, L_BUF, QK_W), BF16),
            jax.ShapeDtypeStruct((FOX_HEADS, L_BUF, QK_W), BF16),
            jax.ShapeDtypeStruct((FOX_HEADS, L_BUF, FOX_HD), BF16),
            jax.ShapeDtypeStruct((L_BUF, LANE), F32),
        ],
        scratch_shapes=[pltpu.VMEM((8, LANE), F32)],
        compiler_params=_cparams(("arbitrary",)),
        name="attn_prep",
    )(proj, proj, proj, proj, proj, proj, cos_t, sin_t, gql, gkvl, gfq, gfk, bf_pad, wq, wkv, tri)


def _scores(q, k):
    return lax.dot_general(q, k, (((1,), (1,)), ((), ())), preferred_element_type=F32)


def _flash_tile(i, q_ref, k_ref, v_ref, m_sc, acc_sc, p_sc, alpha_sc, s_sc, p_new, alpha_new,
                diag_mask, row_shift=None):
    m_sc[...] = jnp.full_like(m_sc, NEG)
    acc_sc[...] = jnp.zeros_like(acc_sc)
    row_blocks = [slice(rb * ROW_BLK, (rb + 1) * ROW_BLK) for rb in range(ATT_TILE // ROW_BLK)]

    def chunk_start(j):
        return pl.multiple_of(j * ATT_TILE, ATT_TILE)

    def flush(j):
        v = v_ref[0, pl.ds(chunk_start(j), ATT_TILE), :]
        v_ones = jnp.concatenate([v, jnp.ones_like(v)], axis=1)
        for rows in row_blocks:
            pv = acc_sc[rows, :] * jnp.tile(alpha_sc[rows, :], (1, 2))
            for kc in range(0, ATT_TILE, 2 * LANE):
                pv = pv + jnp.dot(p_sc[rows, kc:kc + 2 * LANE], v_ones[kc:kc + 2 * LANE, :],
                                  preferred_element_type=F32)
            acc_sc[rows, :] = pv

    def scores(j, rows, ncols):
        return _scores(q_ref[0, rows, :], k_ref[0, pl.ds(chunk_start(j), ncols), :])

    def softmax_rows(rows, s, p_dst, alpha_dst):
        ncols = s.shape[1]
        m_prev = m_sc[rows, :]
        m_chunk = jnp.max(s, axis=-1, keepdims=True)
        if row_shift is not None:
            m_chunk = m_chunk + row_shift[rows, :]
        m_new = jnp.maximum(m_prev, m_chunk)
        alpha_dst[rows, :] = jnp.exp2(m_prev - m_new)
        sub = m_new if row_shift is None else m_new - row_shift[rows, :]
        p_dst[rows, 0:ncols] = jnp.exp2(s - jnp.tile(sub, (1, ncols // LANE))).astype(BF16)
        m_sc[rows, :] = m_new

    def scores_to_s(j):
        for rows in row_blocks:
            s_sc[rows, :] = scores(j, rows, ATT_TILE)

    def softmax_from_s():
        for rows in row_blocks:
            softmax_rows(rows, s_sc[rows, :], p_new, alpha_new)

    def make_pending():
        p_sc[...] = p_new[...]
        alpha_sc[...] = alpha_new[...]

    for rows in row_blocks:
        ncols = rows.stop
        s = scores(i, rows, ncols)
        r = lax.broadcasted_iota(jnp.int32, s.shape, 0) + rows.start
        c = lax.broadcasted_iota(jnp.int32, s.shape, 1)
        if ncols < ATT_TILE:
            p_sc[rows, ncols:] = jnp.zeros((ROW_BLK, ATT_TILE - ncols), BF16)
        softmax_rows(rows, jnp.where(diag_mask(r, c), s, NEG), p_sc, alpha_sc)

    def body(j, pending):
        for rows in row_blocks:
            softmax_rows(rows, scores(j, rows, ATT_TILE), p_new, alpha_new)
        flush(pending)
        make_pending()
        return j
    flush(lax.fori_loop(0, i, body, i))

    return acc_sc[:, 0:LANE] / acc_sc[:, LANE:2 * LANE]


_ATT_SCRATCH = [pltpu.VMEM((ATT_TILE, LANE), F32), pltpu.VMEM((ATT_TILE, 2 * LANE), F32),
                pltpu.VMEM((ATT_TILE, ATT_TILE), BF16), pltpu.VMEM((ATT_TILE, LANE), F32),
                pltpu.VMEM((ATT_TILE, ATT_TILE), F32),
                pltpu.VMEM((ATT_TILE, ATT_TILE), BF16), pltpu.VMEM((ATT_TILE, LANE), F32)]


def _mla_attn_kernel(q_ref, k_ref, v_ref, o_ref, *scratch):
    mask = lambda r, c: jnp.right_shift(c, CHUNK_SHIFT) <= jnp.right_shift(r, CHUNK_SHIFT)
    out = _flash_tile(pl.program_id(1), q_ref, k_ref, v_ref, *scratch, diag_mask=mask)
    o_ref[...] = out.astype(o_ref.dtype)


def _mla_attn(q, k, v):
    return pl.pallas_call(
        _mla_attn_kernel,
        grid=(MLA_HEADS, N_AT),
        in_specs=[
            pl.BlockSpec((1, ATT_TILE, QK_W), lambda h, i: (h, i, 0)),
            pl.BlockSpec((1, L_BUF, QK_W), lambda h, i: (h, 0, 0)),
            pl.BlockSpec((1, L_BUF, MLA_V), lambda h, i: (h, 0, 0)),
        ],
        out_specs=pl.BlockSpec((ATT_TILE, MLA_V), lambda h, i: (i, h)),
        out_shape=jax.ShapeDtypeStruct((L_BUF, MLA_HEADS * MLA_V), BF16),
        scratch_shapes=_ATT_SCRATCH,
        compiler_params=_cparams(("parallel", "arbitrary"), ATT_FLAGS),
        name="mla_attn",
    )(q, k, v)


def _fox_attn_kernel(q_ref, k_ref, v_ref, c_ref, gate_ref, o_ref, *scratch):
    lane = lax.broadcasted_iota(jnp.int32, (ATT_TILE, LANE), 1)
    cq = jnp.sum(jnp.where(lane == pl.program_id(0), c_ref[...], 0.0), axis=-1, keepdims=True)
    out = _flash_tile(pl.program_id(1), q_ref, k_ref, v_ref, *scratch,
                      diag_mask=lambda r, c: c <= r, row_shift=cq)
    o_ref[...] = (out * jax.nn.sigmoid(gate_ref[...])).astype(o_ref.dtype)


def _fox_attn(q, k, v, c, proj):
    return pl.pallas_call(
        _fox_attn_kernel,
        grid=(FOX_HEADS, N_AT),
        in_specs=[
            pl.BlockSpec((1, ATT_TILE, QK_W), lambda h, i: (h, i, 0)),
            pl.BlockSpec((1, L_BUF, QK_W), lambda h, i: (h, 0, 0)),
            pl.BlockSpec((1, L_BUF, FOX_HD), lambda h, i: (h, 0, 0)),
            pl.BlockSpec((ATT_TILE, LANE), lambda h, i: (i, 0)),
            pl.BlockSpec((ATT_TILE, FOX_HD), lambda h, i: (i, GATE_COL0 // FOX_HD + h)),
        ],
        out_specs=pl.BlockSpec((ATT_TILE, FOX_HD), lambda h, i: (i, h)),
        out_shape=jax.ShapeDtypeStruct((L_BUF, FOX_W), BF16),
        scratch_shapes=_ATT_SCRATCH,
        compiler_params=_cparams(("parallel", "arbitrary"), ATT_FLAGS),
        name="fox_attn",
    )(q, k, v, c, proj)


def _mix_out_kernel(a_ref, b_ref, wa_ref, wb_ref, h_ref, g_ref, o_ref):
    mix = (jnp.dot(a_ref[...], wa_ref[...], preferred_element_type=F32)
           + jnp.dot(b_ref[...], wb_ref[...], preferred_element_type=F32))
    out = h_ref[...] + _rms(mix, g_ref[...])
    row = pl.program_id(0) * TM_OUT + lax.broadcasted_iota(jnp.int32, (TM_OUT, 1), 0)
    o_ref[...] = jnp.where(row >= ROW_PAD, out, 0.0)


def _mix_out(a, b, wa, wb, h, g):
    return pl.pallas_call(
        _mix_out_kernel,
        grid=(L_BUF // TM_OUT,),
        in_specs=[
            pl.BlockSpec((TM_OUT, MLA_HEADS * MLA_V), lambda i: (i, 0)),
            pl.BlockSpec((TM_OUT, FOX_W), lambda i: (i, 0)),
            pl.BlockSpec((MLA_HEADS * MLA_V, D_MODEL), lambda i: (0, 0)),
            pl.BlockSpec((FOX_W, D_MODEL), lambda i: (0, 0)),
            pl.BlockSpec((TM_OUT, D_MODEL), lambda i: (i, 0)),
            pl.BlockSpec((1, D_MODEL), lambda i: (0, 0)),
        ],
        out_specs=pl.BlockSpec((TM_OUT, D_MODEL), lambda i: (i, 0)),
        out_shape=jax.ShapeDtypeStruct((L_BUF, D_MODEL), F32),
        compiler_params=_cparams(("parallel",)),
        name="mix_out",
    )(a, b, wa, wb, h, g)


def _gelu_tanh(x):
    return 0.5 * x * (1.0 + jnp.tanh(np.sqrt(2.0 / np.pi).astype(np.float32) * (x + 0.044715 * (x * x * x))))


def _ffn_kernel(h_ref, halo_ref, gpre_ref, wg_ref, wu_ref, cwg_ref, cwu_ref, cbg_ref, cbu_ref,
                wd_ref, gpost_ref, o_ref, xn_ref, ug_ref, uu_ref):
    c = pl.program_id(1)

    @pl.when(c == 0)
    def _():
        xn_ref[0:HALO, :] = _rms(halo_ref[...], gpre_ref[...]).astype(BF16)

        def body(r, carry):
            src = pl.ds(pl.multiple_of(r * 64, 64), 64)
            dst = pl.ds(pl.multiple_of(HALO + r * 64, 16), 64)
            xn_ref[dst, :] = _rms(h_ref[src, :], gpre_ref[...]).astype(BF16)
            return carry
        lax.fori_loop(0, TM_FFN // 64, body, 0)
        o_ref[...] = jnp.zeros_like(o_ref)

    xn = xn_ref[...]
    ug_ref[...] = jnp.dot(xn, wg_ref[...], preferred_element_type=F32)
    uu_ref[...] = jnp.dot(xn, wu_ref[...], preferred_element_type=F32)

    def conv(u_ref, w_ref, b_ref):
        acc = b_ref[...] + w_ref[CONV_K - 1:CONV_K, :] * u_ref[HALO:HALO + TM_FFN, :]
        for t in range(1, CONV_K):
            acc = acc + w_ref[CONV_K - 1 - t:CONV_K - t, :] * u_ref[HALO - t:HALO - t + TM_FFN, :]
        return acc

    act = _gelu_tanh(conv(ug_ref, cwg_ref, cbg_ref)) * conv(uu_ref, cwu_ref, cbu_ref)
    o_ref[...] += jnp.dot(act.astype(BF16), wd_ref[...], preferred_element_type=F32)

    @pl.when(c == pl.num_programs(1) - 1)
    def _():
        out = h_ref[...] + _rms(o_ref[...], gpost_ref[...])
        row = pl.program_id(0) * TM_FFN + lax.broadcasted_iota(jnp.int32, (TM_FFN, 1), 0)
        o_ref[...] = jnp.where(row >= ROW_PAD, out, 0.0)


def _ffn(h, gpre, w_up, w_conv, b_conv, w_down, gpost):
    n_fc = D_FF // FC_FFN
    halo_blocks = TM_FFN // HALO
    return pl.pallas_call(
        _ffn_kernel,
        grid=(L_BUF // TM_FFN, n_fc),
        in_specs=[
            pl.BlockSpec((TM_FFN, D_MODEL), lambda i, c: (i, 0)),
            pl.BlockSpec((HALO, D_MODEL), lambda i, c: (jnp.maximum(i * halo_blocks - 1, 0), 0)),
            pl.BlockSpec((1, D_MODEL), lambda i, c: (0, 0)),
            pl.BlockSpec((D_MODEL, FC_FFN), lambda i, c: (0, c)),
            pl.BlockSpec((D_MODEL, FC_FFN), lambda i, c: (0, n_fc + c)),
            pl.BlockSpec((CONV_K, FC_FFN), lambda i, c: (0, c)),
            pl.BlockSpec((CONV_K, FC_FFN), lambda i, c: (0, n_fc + c)),
            pl.BlockSpec((1, FC_FFN), lambda i, c: (0, c)),
            pl.BlockSpec((1, FC_FFN), lambda i, c: (0, n_fc + c)),
            pl.BlockSpec((FC_FFN, D_MODEL), lambda i, c: (c, 0)),
            pl.BlockSpec((1, D_MODEL), lambda i, c: (0, 0)),
        ],
        out_specs=pl.BlockSpec((TM_FFN, D_MODEL), lambda i, c: (i, 0)),
        out_shape=jax.ShapeDtypeStruct((L_BUF, D_MODEL), F32),
        scratch_shapes=[pltpu.VMEM((HALO + TM_FFN, D_MODEL), BF16),
                        pltpu.VMEM((HALO + TM_FFN, FC_FFN), F32),
                        pltpu.VMEM((HALO + TM_FFN, FC_FFN), F32)],
        compiler_params=_cparams(("parallel", "arbitrary")),
        name="conv_ffn",
    )(h, h, gpre, w_up, w_up, w_conv, w_conv, b_conv, b_conv, w_down, gpost)


def _rotate_half_cols(w):
    half = w.shape[-1] // 2
    return jnp.concatenate([-w[..., half:], w[..., :half]], axis=-1)


def _pad_cols(w, width):
    return jnp.pad(w, ((0, 0), (0, width - w.shape[-1])))


def _pack_w_in(w):
    o = np.cumsum([0, MLA_Q_LORA, MLA_KV_LORA, MLA_ROPE, FOX_W, FOX_W, FOX_W, FOX_W, FOX_HEADS])
    c_q, c_kv, k_rope, fq, fk, fv, fg, ff = [w[:, o[n]:o[n + 1]] for n in range(8)]
    tail = jnp.concatenate([_pad_cols(k_rope, LANE), _pad_cols(_rotate_half_cols(k_rope), LANE),
                            _pad_cols(ff, 2 * LANE)], axis=1)
    return jnp.concatenate([c_q, c_kv, fq, fk, fv, fg, tail], axis=1).astype(BF16)


def _pack_w_q_up(w):
    w = w.reshape(MLA_Q_LORA, MLA_HEADS, MLA_NOPE + MLA_ROPE)
    nope, rope = w[..., :MLA_NOPE], w[..., MLA_NOPE:]
    pad = ((0, 0), (0, 0), (0, LANE - MLA_ROPE))
    packed = jnp.concatenate([nope, jnp.pad(rope, pad), jnp.pad(_rotate_half_cols(rope), pad)], axis=-1)
    return packed.reshape(MLA_Q_LORA, MLA_HEADS * Q_HEAD_W).astype(BF16)


def _rope_tables():
    pos = jnp.maximum(jnp.arange(L_BUF, dtype=jnp.int32) - ROW_PAD, 0).astype(F32)
    half = MLA_ROPE // 2
    inv_freq = ROPE_THETA ** (-jnp.arange(half, dtype=F32) / half)
    ang = pos[:, None] * inv_freq[None, :]
    zeros = jnp.zeros((L_BUF, LANE - MLA_ROPE), F32)
    cos, sin = jnp.cos(ang), jnp.sin(ang)
    return (jnp.concatenate([cos, cos, zeros], axis=1), jnp.concatenate([sin, sin, zeros], axis=1))


def kernel(x, meta_tokens, ln_mix_pre, w_in, b_forget, g_q_latent, g_kv_latent, w_q_up, w_kv_up,
           g_fox_q, g_fox_k, w_out, ln_mix_post, ln_ffn_pre, w_ffn_up, w_ffn_conv, b_ffn_conv,
           w_ffn_down, ln_ffn_post):
    assert x.shape == (1, SEQ, D_MODEL), x.shape
    h = jnp.concatenate([jnp.zeros((ROW_PAD, D_MODEL), x.dtype), meta_tokens.astype(x.dtype), x[0]], axis=0)
    cos_t, sin_t = _rope_tables()
    tri = (lax.broadcasted_iota(jnp.int32, (Q_TILE, Q_TILE), 0)
           >= lax.broadcasted_iota(jnp.int32, (Q_TILE, Q_TILE), 1)).astype(BF16)
    row2d = lambda v: v.reshape(1, -1).astype(F32)

    for l in range(DEPTH):
        proj = _proj_in(h, row2d(ln_mix_pre[l]), _pack_w_in(w_in[l]))
        q, k, v, qf, kf, vf, c = _prep(
            proj, cos_t, sin_t, row2d(g_q_latent[l]), row2d(g_kv_latent[l]),
            row2d(g_fox_q[l]), row2d(g_fox_k[l]), _pad_cols(row2d(b_forget[l]), LANE),
            _pack_w_q_up(w_q_up[l]), w_kv_up[l].astype(BF16), tri)
        a = _mla_attn(q, k, v)
        b = _fox_attn(qf, kf, vf, c, proj)
        w_o = w_out[l].astype(BF16)
        h = _mix_out(a, b, w_o[:MLA_HEADS * MLA_V], w_o[MLA_HEADS * MLA_V:], h, row2d(ln_mix_post[l]))
        h = _ffn(h, row2d(ln_ffn_pre[l]), w_ffn_up[l].astype(BF16), w_ffn_conv[l].astype(F32),
                 row2d(b_ffn_conv[l]), w_ffn_down[l].astype(BF16), row2d(ln_ffn_post[l]))

    return h[ROW_PAD + N_META:][None]
```

```python
import functools

import jax
import jax.numpy as jnp
import numpy as np
from jax import lax
from jax.experimental import pallas as pl
from jax.experimental.pallas import tpu as pltpu

F32 = jnp.float32
BF16 = jnp.bfloat16

D_MODEL = 2048
SEQ = 8192
DEPTH = 4
CHUNK = 64
CHUNK_SHIFT = 6
N_META = 16
MLA_HEADS = 8
MLA_Q_LORA = 512
MLA_KV_LORA = 512
MLA_NOPE = 128
MLA_ROPE = 64
MLA_V = 128
ROPE_THETA = 10000.0
FOX_HEADS = 8
FOX_HD = 128
FOX_W = FOX_HEADS * FOX_HD
D_FF = 5632
CONV_K = 3
EPS = 1e-6
NEG = -1e30

LANE = 128
Q_TILE = 256
ROW_PAD = Q_TILE - N_META
L_BUF = ROW_PAD + N_META + SEQ
ATT_TILE = 768
ROW_BLK = 256
N_AT = L_BUF // ATT_TILE

LOG2E = float(np.log2(np.e))
MLA_SCALE = (MLA_NOPE + MLA_ROPE) ** -0.5 * LOG2E
FOX_SCALE = FOX_HD ** -0.5 * LOG2E

TAIL_W = 4 * LANE
IN_PACKED = MLA_Q_LORA + MLA_KV_LORA + 4 * FOX_W + TAIL_W
GATE_COL0 = MLA_Q_LORA + MLA_KV_LORA + 3 * FOX_W
Q_HEAD_W = 3 * LANE
QK_W = 2 * LANE

TM_PROJ = 768
TN_PROJ = 512
TM_OUT = 384
TM_FFN = 768
FC_FFN = 512
HALO = 16

VMEM_LIMIT = 56 * 1024 * 1024


def _rms(x, g):
    ms = jnp.mean(x * x, axis=-1, keepdims=True)
    return x * lax.rsqrt(ms + EPS) * g


def _split3(x):
    hi = x.astype(BF16).astype(F32)
    r = x - hi
    mid = r.astype(BF16).astype(F32)
    return hi, mid, r - mid


def _cparams(sem, flags=None):
    return pltpu.CompilerParams(dimension_semantics=sem, vmem_limit_bytes=VMEM_LIMIT, flags=flags)


ATT_FLAGS = None


def _proj_in_kernel(x_ref, g_ref, w_ref, o_ref, xn_ref):
    @pl.when(pl.program_id(1) == 0)
    def _():
        def body(r, carry):
            rows = pl.ds(pl.multiple_of(r * 64, 64), 64)
            xn_ref[rows, :] = _rms(x_ref[rows, :], g_ref[...]).astype(BF16)
            return carry
        lax.fori_loop(0, TM_PROJ // 64, body, 0)

    o_ref[...] = jnp.dot(xn_ref[...], w_ref[...], preferred_element_type=F32)


def _proj_in(l, h, g, w):
    return pl.pallas_call(
        _proj_in_kernel,
        grid=(L_BUF // TM_PROJ, IN_PACKED // TN_PROJ),
        in_specs=[
            pl.BlockSpec((TM_PROJ, D_MODEL), lambda i, j: (i, 0)),
            pl.BlockSpec((1, D_MODEL), lambda i, j: (0, 0)),
            pl.BlockSpec((None, D_MODEL, TN_PROJ), lambda i, j: (l, 0, j)),
        ],
        out_specs=pl.BlockSpec((TM_PROJ, TN_PROJ), lambda i, j: (i, j)),
        out_shape=jax.ShapeDtypeStruct((L_BUF, IN_PACKED), F32),
        scratch_shapes=[pltpu.VMEM((TM_PROJ, D_MODEL), BF16)],
        compiler_params=_cparams(("parallel", "arbitrary")),
        name="proj_in",
    )(h, g, w)


def _prep_kernel(cq_ref, ckv_ref, fq_ref, fk_ref, fv_ref, tail_ref, cos_ref, sin_ref,
                 gql_ref, gkvl_ref, gfq_ref, gfk_ref, bf_ref, wq_ref, wkv_ref, tri_ref,
                 q_ref, k_ref, v_ref, qf_ref, kf_ref, vf_ref, c_ref, carry_ref):
    i = pl.program_id(0)
    lane = lax.broadcasted_iota(jnp.int32, (Q_TILE, LANE), 1)
    is_pad = (i * Q_TILE + lax.broadcasted_iota(jnp.int32, (Q_TILE, 1), 0)) < ROW_PAD
    cos = cos_ref[...]
    sin = sin_ref[...]
    q_flag = jnp.where(lane == MLA_ROPE, 1.0, 0.0)
    k_flag = jnp.where((lane == MLA_ROPE) & is_pad, NEG, 0.0)
    kr = (tail_ref[:, 0:LANE] * cos + tail_ref[:, LANE:2 * LANE] * sin + k_flag).astype(BF16)
    cqn = _rms(cq_ref[...], gql_ref[...]).astype(BF16)
    ckvn = _rms(ckv_ref[...], gkvl_ref[...]).astype(BF16)
    for h in range(MLA_HEADS):
        qh = jnp.dot(cqn, wq_ref[:, Q_HEAD_W * h:Q_HEAD_W * (h + 1)], preferred_element_type=F32)
        qr = qh[:, LANE:2 * LANE] * cos + qh[:, 2 * LANE:3 * LANE] * sin
        q_ref[h, :, 0:LANE] = (qh[:, 0:LANE] * MLA_SCALE).astype(BF16)
        q_ref[h, :, LANE:QK_W] = (qr * MLA_SCALE + q_flag).astype(BF16)
        kvh = jnp.dot(ckvn, wkv_ref[:, 2 * LANE * h:2 * LANE * (h + 1)], preferred_element_type=F32)
        k_ref[h, :, 0:LANE] = kvh[:, 0:LANE].astype(BF16)
        k_ref[h, :, LANE:QK_W] = kr
        v_ref[h] = kvh[:, LANE:2 * LANE].astype(BF16)

    z = tail_ref[:, 2 * LANE:3 * LANE] + bf_ref[...]
    logf = (jnp.minimum(z, 0.0) - jnp.log(1.0 + jnp.exp(-jnp.abs(z)))) * LOG2E
    hi, mid, lo = _split3(logf)
    tri = tri_ref[...]
    cs = (jnp.dot(tri, hi.astype(BF16), preferred_element_type=F32)
          + jnp.dot(tri, mid.astype(BF16), preferred_element_type=F32)
          + jnp.dot(tri, lo.astype(BF16), preferred_element_type=F32))

    @pl.when(i == 0)
    def _():
        carry_ref[...] = jnp.zeros_like(carry_ref)

    c = cs + carry_ref[0:1, :]
    c_ref[...] = c
    carry_ref[...] = jnp.broadcast_to(c[Q_TILE - 1:Q_TILE, :], carry_ref.shape)

    qf_ext = jnp.where(lane < 3, -1.0, 0.0).astype(BF16)
    for h in range(FOX_HEADS):
        cols = slice(FOX_HD * h, FOX_HD * (h + 1))
        qf_ref[h, :, 0:FOX_HD] = (_rms(fq_ref[:, cols], gfq_ref[...]) * FOX_SCALE).astype(BF16)
        qf_ref[h, :, FOX_HD:QK_W] = qf_ext
        kf_ref[h, :, 0:FOX_HD] = _rms(fk_ref[:, cols], gfk_ref[...]).astype(BF16)
        ck = jnp.sum(jnp.where(lane == h, c, 0.0), axis=-1, keepdims=True)
        ck_hi, ck_mid, ck_lo = _split3(jnp.where(is_pad, -NEG, ck))
        ext = jnp.where(lane == 0, ck_hi, jnp.where(lane == 1, ck_mid, jnp.where(lane == 2, ck_lo, 0.0)))
        kf_ref[h, :, FOX_HD:QK_W] = ext.astype(BF16)
        vf_ref[h] = fv_ref[:, cols].astype(BF16)


def _prep(l, proj, cos_t, sin_t, gql, gkvl, gfq, gfk, bf_pad, wq, wkv, tri):
    tm = Q_TILE
    row = lambda i: (i, 0)
    const = lambda i: (0, 0)
    head_out = lambda w: pl.BlockSpec((MLA_HEADS, tm, w), lambda i: (0, i, 0))
    return pl.pallas_call(
        _prep_kernel,
        grid=(L_BUF // tm,),
        in_specs=[
            pl.BlockSpec((tm, MLA_Q_LORA), lambda i: (i, 0)),
            pl.BlockSpec((tm, MLA_KV_LORA), lambda i: (i, 1)),
            pl.BlockSpec((tm, FOX_W), lambda i: (i, 1)),
            pl.BlockSpec((tm, FOX_W), lambda i: (i, 2)),
            pl.BlockSpec((tm, FOX_W), lambda i: (i, 3)),
            pl.BlockSpec((tm, TAIL_W), lambda i: (i, IN_PACKED // TAIL_W - 1)),
            pl.BlockSpec((tm, LANE), row),
            pl.BlockSpec((tm, LANE), row),
            pl.BlockSpec((1, MLA_Q_LORA), const),
            pl.BlockSpec((1, MLA_KV_LORA), const),
            pl.BlockSpec((1, FOX_HD), const),
            pl.BlockSpec((1, FOX_HD), const),
            pl.BlockSpec((1, LANE), const),
            pl.BlockSpec((None, MLA_Q_LORA, MLA_HEADS * Q_HEAD_W), lambda i: (l, 0, 0)),
            pl.BlockSpec((None, MLA_KV_LORA, MLA_HEADS * 2 * LANE), lambda i: (l, 0, 0)),
            pl.BlockSpec((tm, tm), const),
        ],
        out_specs=[
            head_out(QK_W), head_out(QK_W), head_out(MLA_V),
            head_out(QK_W), head_out(QK_W), head_out(FOX_HD),
            pl.BlockSpec((tm, LANE), row),
        ],
        out_shape=[
            jax.ShapeDtypeStruct((MLA_HEADS, L_BUF, QK_W), BF16),
            jax.ShapeDtypeStruct((MLA_HEADS, L_BUF, QK_W), BF16),
            jax.ShapeDtypeStruct((MLA_HEADS, L_BUF, MLA_V), BF16),
            jax.ShapeDtypeStruct((FOX_HEADS, L_BUF, QK_W), BF16),
            jax.ShapeDtypeStruct((FOX_HEADS, L_BUF, QK_W), BF16),
            jax.ShapeDtypeStruct((FOX_HEADS, L_BUF, FOX_HD), BF16),
            jax.ShapeDtypeStruct((L_BUF, LANE), F32),
        ],
        scratch_shapes=[pltpu.VMEM((8, LANE), F32)],
        compiler_params=_cparams(("arbitrary",)),
        name="attn_prep",
    )(proj, proj, proj, proj, proj, proj, cos_t, sin_t, gql, gkvl, gfq, gfk, bf_pad, wq, wkv, tri)


def _scores(q, k):
    return lax.dot_general(q, k, (((1,), (1,)), ((), ())), preferred_element_type=F32)


def _flash_tile(i, q_ref, k_ref, v_ref, m_sc, acc_sc, p_sc, alpha_sc, p_new, alpha_new,
                diag_mask, row_shift=None):
    m_sc[...] = jnp.full_like(m_sc, NEG)
    acc_sc[...] = jnp.zeros_like(acc_sc)
    row_blocks = [slice(rb * ROW_BLK, (rb + 1) * ROW_BLK) for rb in range(ATT_TILE // ROW_BLK)]

    def chunk_start(j):
        return pl.multiple_of(j * ATT_TILE, ATT_TILE)

    def flush(j):
        v = v_ref[0, pl.ds(chunk_start(j), ATT_TILE), :]
        v_ones = jnp.concatenate([v, jnp.ones_like(v)], axis=1)
        for rows in row_blocks:
            pv = acc_sc[rows, :] * jnp.tile(alpha_sc[rows, :], (1, 2))
            for kc in range(0, ATT_TILE, 2 * LANE):
                pv = pv + jnp.dot(p_sc[rows, kc:kc + 2 * LANE], v_ones[kc:kc + 2 * LANE, :],
                                  preferred_element_type=F32)
            acc_sc[rows, :] = pv

    def scores(j, rows, ncols):
        return _scores(q_ref[0, rows, :], k_ref[0, pl.ds(chunk_start(j), ncols), :])

    def softmax_rows(rows, s, p_dst, alpha_dst):
        ncols = s.shape[1]
        m_prev = m_sc[rows, :]
        m_chunk = jnp.max(s, axis=-1, keepdims=True)
        if row_shift is not None:
            m_chunk = m_chunk + row_shift[rows, :]
        m_new = jnp.maximum(m_prev, m_chunk)
        alpha_dst[rows, :] = jnp.exp2(m_prev - m_new)
        sub = m_new if row_shift is None else m_new - row_shift[rows, :]
        p_dst[rows, 0:ncols] = jnp.exp2(s - jnp.tile(sub, (1, ncols // LANE))).astype(BF16)
        m_sc[rows, :] = m_new

    def make_pending():
        p_sc[...] = p_new[...]
        alpha_sc[...] = alpha_new[...]

    for rows in row_blocks:
        ncols = rows.stop
        s = scores(i, rows, ncols)
        r = lax.broadcasted_iota(jnp.int32, s.shape, 0) + rows.start
        c = lax.broadcasted_iota(jnp.int32, s.shape, 1)
        if ncols < ATT_TILE:
            p_sc[rows, ncols:] = jnp.zeros((ROW_BLK, ATT_TILE - ncols), BF16)
        softmax_rows(rows, jnp.where(diag_mask(r, c), s, NEG), p_sc, alpha_sc)

    def body(j, pending):
        for rows in row_blocks:
            softmax_rows(rows, scores(j, rows, ATT_TILE), p_new, alpha_new)
        flush(pending)
        make_pending()
        return j
    flush(lax.fori_loop(0, i, body, i))

    return acc_sc[:, 0:LANE] / acc_sc[:, LANE:2 * LANE]


_ATT_SCRATCH = [pltpu.VMEM((ATT_TILE, LANE), F32), pltpu.VMEM((ATT_TILE, 2 * LANE), F32),
                pltpu.VMEM((ATT_TILE, ATT_TILE), BF16), pltpu.VMEM((ATT_TILE, LANE), F32),
                pltpu.VMEM((ATT_TILE, ATT_TILE), BF16), pltpu.VMEM((ATT_TILE, LANE), F32)]


def _mla_attn_kernel(q_ref, k_ref, v_ref, o_ref, *scratch):
    mask = lambda r, c: jnp.right_shift(c, CHUNK_SHIFT) <= jnp.right_shift(r, CHUNK_SHIFT)
    out = _flash_tile(pl.program_id(1), q_ref, k_ref, v_ref, *scratch, diag_mask=mask)
    o_ref[...] = out.astype(o_ref.dtype)


def _mla_attn(q, k, v):
    return pl.pallas_call(
        _mla_attn_kernel,
        grid=(MLA_HEADS, N_AT),
        in_specs=[
            pl.BlockSpec((1, ATT_TILE, QK_W), lambda h, i: (h, i, 0)),
            pl.BlockSpec((1, L_BUF, QK_W), lambda h, i: (h, 0, 0)),
            pl.BlockSpec((1, L_BUF, MLA_V), lambda h, i: (h, 0, 0)),
        ],
        out_specs=pl.BlockSpec((ATT_TILE, MLA_V), lambda h, i: (i, h)),
        out_shape=jax.ShapeDtypeStruct((L_BUF, MLA_HEADS * MLA_V), BF16),
        scratch_shapes=_ATT_SCRATCH,
        compiler_params=_cparams(("parallel", "arbitrary"), ATT_FLAGS),
        name="mla_attn",
    )(q, k, v)


def _fox_attn_kernel(q_ref, k_ref, v_ref, c_ref, gate_ref, o_ref, *scratch):
    lane = lax.broadcasted_iota(jnp.int32, (ATT_TILE, LANE), 1)
    cq = jnp.sum(jnp.where(lane == pl.program_id(0), c_ref[...], 0.0), axis=-1, keepdims=True)
    out = _flash_tile(pl.program_id(1), q_ref, k_ref, v_ref, *scratch,
                      diag_mask=lambda r, c: c <= r, row_shift=cq)
    o_ref[...] = (out * jax.nn.sigmoid(gate_ref[...])).astype(o_ref.dtype)


def _fox_attn(q, k, v, c, proj):
    return pl.pallas_call(
        _fox_attn_kernel,
        grid=(FOX_HEADS, N_AT),
        in_specs=[
            pl.BlockSpec((1, ATT_TILE, QK_W), lambda h, i: (h, i, 0)),
            pl.BlockSpec((1, L_BUF, QK_W), lambda h, i: (h, 0, 0)),
            pl.BlockSpec((1, L_BUF, FOX_HD), lambda h, i: (h, 0, 0)),
            pl.BlockSpec((ATT_TILE, LANE), lambda h, i: (i, 0)),
            pl.BlockSpec((ATT_TILE, FOX_HD), lambda h, i: (i, GATE_COL0 // FOX_HD + h)),
        ],
        out_specs=pl.BlockSpec((ATT_TILE, FOX_HD), lambda h, i: (i, h)),
        out_shape=jax.ShapeDtypeStruct((L_BUF, FOX_W), BF16),
        scratch_shapes=_ATT_SCRATCH,
        compiler_params=_cparams(("parallel", "arbitrary"), ATT_FLAGS),
        name="fox_attn",
    )(q, k, v, c, proj)


def _mix_out_kernel(a_ref, b_ref, wa_ref, wb_ref, h_ref, g_ref, o_ref):
    mix = (jnp.dot(a_ref[...], wa_ref[...], preferred_element_type=F32)
           + jnp.dot(b_ref[...], wb_ref[...], preferred_element_type=F32))
    out = h_ref[...] + _rms(mix, g_ref[...])
    row = pl.program_id(0) * TM_OUT + lax.broadcasted_iota(jnp.int32, (TM_OUT, 1), 0)
    o_ref[...] = jnp.where(row >= ROW_PAD, out, 0.0)


def _mix_out(l, a, b, w_o, h, g):
    return pl.pallas_call(
        _mix_out_kernel,
        grid=(L_BUF // TM_OUT,),
        in_specs=[
            pl.BlockSpec((TM_OUT, MLA_HEADS * MLA_V), lambda i: (i, 0)),
            pl.BlockSpec((TM_OUT, FOX_W), lambda i: (i, 0)),
            pl.BlockSpec((None, MLA_HEADS * MLA_V, D_MODEL), lambda i: (l, 0, 0)),
            pl.BlockSpec((None, FOX_W, D_MODEL), lambda i: (l, 1, 0)),
            pl.BlockSpec((TM_OUT, D_MODEL), lambda i: (i, 0)),
            pl.BlockSpec((1, D_MODEL), lambda i: (0, 0)),
        ],
        out_specs=pl.BlockSpec((TM_OUT, D_MODEL), lambda i: (i, 0)),
        out_shape=jax.ShapeDtypeStruct((L_BUF, D_MODEL), F32),
        compiler_params=_cparams(("parallel",)),
        name="mix_out",
    )(a, b, w_o, w_o, h, g)


def _gelu_tanh(x):
    return 0.5 * x * (1.0 + jnp.tanh(np.sqrt(2.0 / np.pi).astype(np.float32) * (x + 0.044715 * (x * x * x))))


def _ffn_kernel(h_ref, halo_ref, gpre_ref, wg_ref, wu_ref, cwg_ref, cwu_ref, cbg_ref, cbu_ref,
                wd_ref, gpost_ref, o_ref, xn_ref, ug_ref, uu_ref):
    c = pl.program_id(1)

    @pl.when(c == 0)
    def _():
        xn_ref[0:HALO, :] = _rms(halo_ref[...], gpre_ref[...]).astype(BF16)

        def body(r, carry):
            src = pl.ds(pl.multiple_of(r * 64, 64), 64)
            dst = pl.ds(pl.multiple_of(HALO + r * 64, 16), 64)
            xn_ref[dst, :] = _rms(h_ref[src, :], gpre_ref[...]).astype(BF16)
            return carry
        lax.fori_loop(0, TM_FFN // 64, body, 0)
        o_ref[...] = jnp.zeros_like(o_ref)

    xn = xn_ref[...]
    ug_ref[...] = jnp.dot(xn, wg_ref[...], preferred_element_type=F32)
    uu_ref[...] = jnp.dot(xn, wu_ref[...], preferred_element_type=F32)

    def conv(u_ref, w_ref, b_ref):
        acc = b_ref[...] + w_ref[CONV_K - 1:CONV_K, :] * u_ref[HALO:HALO + TM_FFN, :]
        for t in range(1, CONV_K):
            acc = acc + w_ref[CONV_K - 1 - t:CONV_K - t, :] * u_ref[HALO - t:HALO - t + TM_FFN, :]
        return acc

    act = _gelu_tanh(conv(ug_ref, cwg_ref, cbg_ref)) * conv(uu_ref, cwu_ref, cbu_ref)
    o_ref[...] += jnp.dot(act.astype(BF16), wd_ref[...], preferred_element_type=F32)

    @pl.when(c == pl.num_programs(1) - 1)
    def _():
        out = h_ref[...] + _rms(o_ref[...], gpost_ref[...])
        row = pl.program_id(0) * TM_FFN + lax.broadcasted_iota(jnp.int32, (TM_FFN, 1), 0)
        o_ref[...] = jnp.where(row >= ROW_PAD, out, 0.0)


def _ffn(l, h, gpre, w_up, w_conv, b_conv, w_down, gpost):
    n_fc = D_FF // FC_FFN
    halo_blocks = TM_FFN // HALO
    return pl.pallas_call(
        _ffn_kernel,
        grid=(L_BUF // TM_FFN, n_fc),
        in_specs=[
            pl.BlockSpec((TM_FFN, D_MODEL), lambda i, c: (i, 0)),
            pl.BlockSpec((HALO, D_MODEL), lambda i, c: (jnp.maximum(i * halo_blocks - 1, 0), 0)),
            pl.BlockSpec((1, D_MODEL), lambda i, c: (0, 0)),
            pl.BlockSpec((None, D_MODEL, FC_FFN), lambda i, c: (l, 0, c)),
            pl.BlockSpec((None, D_MODEL, FC_FFN), lambda i, c: (l, 0, n_fc + c)),
            pl.BlockSpec((CONV_K, FC_FFN), lambda i, c: (0, c)),
            pl.BlockSpec((CONV_K, FC_FFN), lambda i, c: (0, n_fc + c)),
            pl.BlockSpec((1, FC_FFN), lambda i, c: (0, c)),
            pl.BlockSpec((1, FC_FFN), lambda i, c: (0, n_fc + c)),
            pl.BlockSpec((None, FC_FFN, D_MODEL), lambda i, c: (l, c, 0)),
            pl.BlockSpec((1, D_MODEL), lambda i, c: (0, 0)),
        ],
        out_specs=pl.BlockSpec((TM_FFN, D_MODEL), lambda i, c: (i, 0)),
        out_shape=jax.ShapeDtypeStruct((L_BUF, D_MODEL), F32),
        scratch_shapes=[pltpu.VMEM((HALO + TM_FFN, D_MODEL), BF16),
                        pltpu.VMEM((HALO + TM_FFN, FC_FFN), F32),
                        pltpu.VMEM((HALO + TM_FFN, FC_FFN), F32)],
        compiler_params=_cparams(("parallel", "arbitrary")),
        name="conv_ffn",
    )(h, h, gpre, w_up, w_up, w_conv, w_conv, b_conv, b_conv, w_down, gpost)


def _rotate_half_cols(w):
    half = w.shape[-1] // 2
    return jnp.concatenate([-w[..., half:], w[..., :half]], axis=-1)


def _pad_cols(w, width):
    return jnp.pad(w, [(0, 0)] * (w.ndim - 1) + [(0, width - w.shape[-1])])


def _pack_w_in(w):
    o = np.cumsum([0, MLA_Q_LORA, MLA_KV_LORA, MLA_ROPE, FOX_W, FOX_W, FOX_W, FOX_W, FOX_HEADS])
    c_q, c_kv, k_rope, fq, fk, fv, fg, ff = [w[..., o[n]:o[n + 1]].astype(BF16) for n in range(8)]
    tail = jnp.concatenate([_pad_cols(k_rope, LANE), _pad_cols(_rotate_half_cols(k_rope), LANE),
                            _pad_cols(ff, 2 * LANE)], axis=-1)
    return jnp.concatenate([c_q, c_kv, fq, fk, fv, fg, tail], axis=-1)


def _pack_w_q_up(w):
    w = w.astype(BF16).reshape(DEPTH, MLA_Q_LORA, MLA_HEADS, MLA_NOPE + MLA_ROPE)
    nope, rope = w[..., :MLA_NOPE], w[..., MLA_NOPE:]
    packed = jnp.concatenate([nope, _pad_cols(rope, LANE), _pad_cols(_rotate_half_cols(rope), LANE)], axis=-1)
    return packed.reshape(DEPTH, MLA_Q_LORA, MLA_HEADS * Q_HEAD_W)


def _rope_tables():
    pos = jnp.maximum(jnp.arange(L_BUF, dtype=jnp.int32) - ROW_PAD, 0).astype(F32)
    half = MLA_ROPE // 2
    inv_freq = ROPE_THETA ** (-jnp.arange(half, dtype=F32) / half)
    ang = pos[:, None] * inv_freq[None, :]
    zeros = jnp.zeros((L_BUF, LANE - MLA_ROPE), F32)
    cos, sin = jnp.cos(ang), jnp.sin(ang)
    return (jnp.concatenate([cos, cos, zeros], axis=1), jnp.concatenate([sin, sin, zeros], axis=1))


def kernel(x, meta_tokens, ln_mix_pre, w_in, b_forget, g_q_latent, g_kv_latent, w_q_up, w_kv_up,
           g_fox_q, g_fox_k, w_out, ln_mix_post, ln_ffn_pre, w_ffn_up, w_ffn_conv, b_ffn_conv,
           w_ffn_down, ln_ffn_post):
    assert x.shape == (1, SEQ, D_MODEL), x.shape
    h = jnp.concatenate([jnp.zeros((ROW_PAD, D_MODEL), x.dtype), meta_tokens.astype(x.dtype), x[0]], axis=0)
    cos_t, sin_t = _rope_tables()
    tri = (lax.broadcasted_iota(jnp.int32, (Q_TILE, Q_TILE), 0)
           >= lax.broadcasted_iota(jnp.int32, (Q_TILE, Q_TILE), 1)).astype(BF16)
    row2d = lambda v: v.reshape(1, -1).astype(F32)
    w_in_p, w_q_p = _pack_w_in(w_in), _pack_w_q_up(w_q_up)
    w_kv_b, w_o_b = w_kv_up.astype(BF16), w_out.astype(BF16)
    w_up_b, w_down_b = w_ffn_up.astype(BF16), w_ffn_down.astype(BF16)
    assert MLA_HEADS * MLA_V == FOX_W

    for l in range(DEPTH):
        proj = _proj_in(l, h, row2d(ln_mix_pre[l]), w_in_p)
        q, k, v, qf, kf, vf, c = _prep(
            l, proj, cos_t, sin_t, row2d(g_q_latent[l]), row2d(g_kv_latent[l]),
            row2d(g_fox_q[l]), row2d(g_fox_k[l]), _pad_cols(row2d(b_forget[l]), LANE),
            w_q_p, w_kv_b, tri)
        a = _mla_attn(q, k, v)
        b = _fox_attn(qf, kf, vf, c, proj)
        h = _mix_out(l, a, b, w_o_b, h, row2d(ln_mix_post[l]))
        h = _ffn(l, h, row2d(ln_ffn_pre[l]), w_up_b, w_ffn_conv[l].astype(F32),
                 row2d(b_ffn_conv[l]), w_down_b, row2d(ln_ffn_post[l]))

    return h[ROW_PAD + N_META:][None]
```

```python
import jax
import jax.numpy as jnp
import numpy as np
from jax import lax
from jax.experimental import pallas as pl
from jax.experimental.pallas import tpu as pltpu

F32 = jnp.float32
BF16 = jnp.bfloat16

D_MODEL = 2048
SEQ = 8192
DEPTH = 4
CHUNK = 64
CHUNK_SHIFT = 6
N_META = 16
MLA_HEADS = 8
MLA_Q_LORA = 512
MLA_KV_LORA = 512
MLA_NOPE = 128
MLA_ROPE = 64
MLA_V = 128
ROPE_THETA = 10000.0
FOX_HEADS = 8
FOX_HD = 128
FOX_W = FOX_HEADS * FOX_HD
D_FF = 5632
CONV_K = 3
EPS = 1e-6
NEG = -1e30

LANE = 128
Q_TILE = 256
ROW_PAD = Q_TILE - N_META
L_BUF = ROW_PAD + N_META + SEQ
ATT_TILE = 768
ROW_BLK = 256
N_AT = L_BUF // ATT_TILE

LOG2E = float(np.log2(np.e))
MLA_SCALE = (MLA_NOPE + MLA_ROPE) ** -0.5 * LOG2E
FOX_SCALE = FOX_HD ** -0.5 * LOG2E

TN_PROJ = 512
TAIL_W = 4 * LANE
IN_PACKED = TAIL_W + MLA_Q_LORA + MLA_KV_LORA + 4 * FOX_W
J_TAIL, J_CQ, J_CKV, J_FQ, J_FK, J_FV, J_FG = 0, 1, 2, 3, 5, 7, 9
HEADS_PER_TILE = TN_PROJ // FOX_HD
Q_HEAD_W = 3 * LANE
QK_W = 2 * LANE

TM_PROJ = 528
TRI_K = -(-TM_PROJ // LANE) * LANE
RMS_ROWS = 48
TM_OUT = 384
TM_FFN = 768
FC_FFN = 512
HALO = 16

VMEM_LIMIT = 56 * 1024 * 1024


def _rms(x, g):
    ms = jnp.mean(x * x, axis=-1, keepdims=True)
    return x * lax.rsqrt(ms + EPS) * g


def _split3(x):
    hi = x.astype(BF16).astype(F32)
    r = x - hi
    mid = r.astype(BF16).astype(F32)
    return hi, mid, r - mid


def _cparams(sem):
    return pltpu.CompilerParams(dimension_semantics=sem, vmem_limit_bytes=VMEM_LIMIT)


def _proj_kernel(x_ref, g_ref, w_ref, cos_ref, sin_ref, gql_ref, gkvl_ref, gfq_ref, gfk_ref, bf_ref,
                 wq_ref, wkv_ref, tri_ref,
                 qm_ref, km_ref, vm_ref, qf_ref, kf_ref, vf_ref, c_ref, gate_ref,
                 xn_sc, t_sc, kr_sc, c_sc, carry_sc, lf_sc):
    i = pl.program_id(0)
    j = pl.program_id(1)

    @pl.when(j == 0)
    def _():
        def body(r, carry):
            rows = pl.ds(pl.multiple_of(r * RMS_ROWS, 16), RMS_ROWS)
            xn_sc[rows, :] = _rms(x_ref[rows, :], g_ref[...]).astype(BF16)
            return carry
        lax.fori_loop(0, TM_PROJ // RMS_ROWS, body, 0)

    t_sc[...] = jnp.dot(xn_sc[...], w_ref[...], preferred_element_type=F32)
    lane = lax.broadcasted_iota(jnp.int32, (TM_PROJ, LANE), 1)
    is_pad = (i * TM_PROJ + lax.broadcasted_iota(jnp.int32, (TM_PROJ, 1), 0)) < ROW_PAD

    @pl.when(j == J_TAIL)
    def _():
        k_flag = jnp.where((lane == MLA_ROPE) & is_pad, NEG, 0.0)
        kr_sc[...] = (t_sc[:, 0:LANE] * cos_ref[...] + t_sc[:, LANE:2 * LANE] * sin_ref[...]
                      + k_flag).astype(BF16)
        z = t_sc[:, 2 * LANE:3 * LANE] + bf_ref[...]
        logf = (jnp.minimum(z, 0.0) - jnp.log(1.0 + jnp.exp(-jnp.abs(z)))) * LOG2E
        lf_sc[:, TM_PROJ:TRI_K, :] = jnp.zeros((3, TRI_K - TM_PROJ, LANE), BF16)
        for n, part in enumerate(_split3(logf)):
            lf_sc[n, 0:TM_PROJ, :] = part.astype(BF16)
        tri = tri_ref[...]
        cs = (jnp.dot(tri, lf_sc[0], preferred_element_type=F32)
              + jnp.dot(tri, lf_sc[1], preferred_element_type=F32)
              + jnp.dot(tri, lf_sc[2], preferred_element_type=F32))

        @pl.when(i == 0)
        def _():
            carry_sc[...] = jnp.zeros_like(carry_sc)

        c = cs + carry_sc[0:1, :]
        c_sc[...] = c
        c_ref[...] = c
        carry_sc[...] = jnp.broadcast_to(c[TM_PROJ - 1:TM_PROJ, :], carry_sc.shape)

    @pl.when(j == J_CQ)
    def _():
        cos = cos_ref[...]
        sin = sin_ref[...]
        q_flag = jnp.where(lane == MLA_ROPE, 1.0, 0.0)
        cqn = _rms(t_sc[...], gql_ref[...]).astype(BF16)
        for h in range(MLA_HEADS):
            qh = jnp.dot(cqn, wq_ref[:, Q_HEAD_W * h:Q_HEAD_W * (h + 1)], preferred_element_type=F32)
            qr = qh[:, LANE:2 * LANE] * cos + qh[:, 2 * LANE:3 * LANE] * sin
            qm_ref[:, QK_W * h:QK_W * h + LANE] = (qh[:, 0:LANE] * MLA_SCALE).astype(BF16)
            qm_ref[:, QK_W * h + LANE:QK_W * (h + 1)] = (qr * MLA_SCALE + q_flag).astype(BF16)

    @pl.when(j == J_CKV)
    def _():
        ckvn = _rms(t_sc[...], gkvl_ref[...]).astype(BF16)
        for h in range(MLA_HEADS):
            kvh = jnp.dot(ckvn, wkv_ref[:, 2 * LANE * h:2 * LANE * (h + 1)], preferred_element_type=F32)
            km_ref[:, QK_W * h:QK_W * h + LANE] = kvh[:, 0:LANE].astype(BF16)
            km_ref[:, QK_W * h + LANE:QK_W * (h + 1)] = kr_sc[...]
            vm_ref[:, MLA_V * h:MLA_V * (h + 1)] = kvh[:, LANE:2 * LANE].astype(BF16)

    @pl.when((j == J_FQ) | (j == J_FQ + 1))
    def _():
        qf_ext = jnp.where(lane < 3, -1.0, 0.0).astype(BF16)
        for hh in range(HEADS_PER_TILE):
            qn = _rms(t_sc[:, FOX_HD * hh:FOX_HD * (hh + 1)], gfq_ref[...])
            qf_ref[:, QK_W * hh:QK_W * hh + FOX_HD] = (qn * FOX_SCALE).astype(BF16)
            qf_ref[:, QK_W * hh + FOX_HD:QK_W * (hh + 1)] = qf_ext

    @pl.when((j == J_FK) | (j == J_FK + 1))
    def _():
        c = c_sc[...]
        for hh in range(HEADS_PER_TILE):
            head = (j - J_FK) * HEADS_PER_TILE + hh
            kn = _rms(t_sc[:, FOX_HD * hh:FOX_HD * (hh + 1)], gfk_ref[...])
            kf_ref[:, QK_W * hh:QK_W * hh + FOX_HD] = kn.astype(BF16)
            ck = jnp.sum(jnp.where(lane == head, c, 0.0), axis=-1, keepdims=True)
            ck_hi, ck_mid, ck_lo = _split3(jnp.where(is_pad, -NEG, ck))
            ext = jnp.where(lane == 0, ck_hi, jnp.where(lane == 1, ck_mid, jnp.where(lane == 2, ck_lo, 0.0)))
            kf_ref[:, QK_W * hh + FOX_HD:QK_W * (hh + 1)] = ext.astype(BF16)

    @pl.when((j == J_FV) | (j == J_FV + 1))
    def _():
        vf_ref[...] = t_sc[...].astype(BF16)

    @pl.when((j == J_FG) | (j == J_FG + 1))
    def _():
        gate_ref[...] = t_sc[...]


def _proj(l, h, g, w, cos_t, sin_t, gql, gkvl, gfq, gfk, bf_pad, wq, wkv, tri):
    row = lambda i, j: (i, 0)
    const = lambda i, j: (0, 0)
    pair = lambda j0: (lambda i, j: (i, jnp.clip(j - j0, 0, 1)))
    return pl.pallas_call(
        _proj_kernel,
        grid=(L_BUF // TM_PROJ, IN_PACKED // TN_PROJ),
        in_specs=[
            pl.BlockSpec((TM_PROJ, D_MODEL), row),
            pl.BlockSpec((1, D_MODEL), const),
            pl.BlockSpec((None, D_MODEL, TN_PROJ), lambda i, j: (l, 0, j)),
            pl.BlockSpec((TM_PROJ, LANE), row),
            pl.BlockSpec((TM_PROJ, LANE), row),
            pl.BlockSpec((1, MLA_Q_LORA), const),
            pl.BlockSpec((1, MLA_KV_LORA), const),
            pl.BlockSpec((1, FOX_HD), const),
            pl.BlockSpec((1, FOX_HD), const),
            pl.BlockSpec((1, LANE), const),
            pl.BlockSpec((None, MLA_Q_LORA, MLA_HEADS * Q_HEAD_W), lambda i, j: (l, 0, 0)),
            pl.BlockSpec((None, MLA_KV_LORA, MLA_HEADS * 2 * LANE), lambda i, j: (l, 0, 0)),
            pl.BlockSpec((TM_PROJ, TRI_K), const),
        ],
        out_specs=[
            pl.BlockSpec((TM_PROJ, MLA_HEADS * QK_W), row),
            pl.BlockSpec((TM_PROJ, MLA_HEADS * QK_W), row),
            pl.BlockSpec((TM_PROJ, MLA_HEADS * MLA_V), row),
            pl.BlockSpec((TM_PROJ, HEADS_PER_TILE * QK_W), pair(J_FQ)),
            pl.BlockSpec((TM_PROJ, HEADS_PER_TILE * QK_W), pair(J_FK)),
            pl.BlockSpec((TM_PROJ, TN_PROJ), pair(J_FV)),
            pl.BlockSpec((TM_PROJ, LANE), row),
            pl.BlockSpec((TM_PROJ, TN_PROJ), pair(J_FG)),
        ],
        out_shape=[
            jax.ShapeDtypeStruct((L_BUF, MLA_HEADS * QK_W), BF16),
            jax.ShapeDtypeStruct((L_BUF, MLA_HEADS * QK_W), BF16),
            jax.ShapeDtypeStruct((L_BUF, MLA_HEADS * MLA_V), BF16),
            jax.ShapeDtypeStruct((L_BUF, FOX_HEADS * QK_W), BF16),
            jax.ShapeDtypeStruct((L_BUF, FOX_HEADS * QK_W), BF16),
            jax.ShapeDtypeStruct((L_BUF, FOX_W), BF16),
            jax.ShapeDtypeStruct((L_BUF, LANE), F32),
            jax.ShapeDtypeStruct((L_BUF, FOX_W), F32),
        ],
        scratch_shapes=[pltpu.VMEM((TM_PROJ, D_MODEL), BF16), pltpu.VMEM((TM_PROJ, TN_PROJ), F32),
                        pltpu.VMEM((TM_PROJ, LANE), BF16), pltpu.VMEM((TM_PROJ, LANE), F32),
                        pltpu.VMEM((8, LANE), F32), pltpu.VMEM((3, TRI_K, LANE), BF16)],
        compiler_params=_cparams(("arbitrary", "arbitrary")),
        name="proj_in",
    )(h, g, w, cos_t, sin_t, gql, gkvl, gfq, gfk, bf_pad, wq, wkv, tri)


def _scores(q, k):
    return lax.dot_general(q, k, (((1,), (1,)), ((), ())), preferred_element_type=F32)


def _flash_tile(i, q_ref, k_ref, v_ref, m_sc, acc_sc, p_sc, alpha_sc, p_new, alpha_new,
                diag_mask, row_shift=None):
    m_sc[...] = jnp.full_like(m_sc, NEG)
    acc_sc[...] = jnp.zeros_like(acc_sc)
    row_blocks = [slice(rb * ROW_BLK, (rb + 1) * ROW_BLK) for rb in range(ATT_TILE // ROW_BLK)]

    def chunk_start(j):
        return pl.multiple_of(j * ATT_TILE, ATT_TILE)

    def flush(j):
        v = v_ref[pl.ds(chunk_start(j), ATT_TILE), :]
        v_ones = jnp.concatenate([v, jnp.ones_like(v)], axis=1)
        for rows in row_blocks:
            pv = acc_sc[rows, :] * jnp.tile(alpha_sc[rows, :], (1, 2))
            for kc in range(0, ATT_TILE, 2 * LANE):
                pv = pv + jnp.dot(p_sc[rows, kc:kc + 2 * LANE], v_ones[kc:kc + 2 * LANE, :],
                                  preferred_element_type=F32)
            acc_sc[rows, :] = pv

    def scores(j, rows, ncols):
        return _scores(q_ref[rows, :], k_ref[pl.ds(chunk_start(j), ncols), :])

    def softmax_rows(rows, s, p_dst, alpha_dst):
        ncols = s.shape[1]
        m_prev = m_sc[rows, :]
        m_chunk = jnp.max(s, axis=-1, keepdims=True)
        if row_shift is not None:
            m_chunk = m_chunk + row_shift[rows, :]
        m_new = jnp.maximum(m_prev, m_chunk)
        alpha_dst[rows, :] = jnp.exp2(m_prev - m_new)
        sub = m_new if row_shift is None else m_new - row_shift[rows, :]
        p_dst[rows, 0:ncols] = jnp.exp2(s - jnp.tile(sub, (1, ncols // LANE))).astype(BF16)
        m_sc[rows, :] = m_new

    def make_pending():
        p_sc[...] = p_new[...]
        alpha_sc[...] = alpha_new[...]

    for rows in row_blocks:
        ncols = rows.stop
        s = scores(i, rows, ncols)
        r = lax.broadcasted_iota(jnp.int32, s.shape, 0) + rows.start
        c = lax.broadcasted_iota(jnp.int32, s.shape, 1)
        if ncols < ATT_TILE:
            p_sc[rows, ncols:] = jnp.zeros((ROW_BLK, ATT_TILE - ncols), BF16)
        softmax_rows(rows, jnp.where(diag_mask(r, c), s, NEG), p_sc, alpha_sc)

    def body(j, pending):
        for rows in row_blocks:
            softmax_rows(rows, scores(j, rows, ATT_TILE), p_new, alpha_new)
        flush(pending)
        make_pending()
        return j
    flush(lax.fori_loop(0, i, body, i))

    return acc_sc[:, 0:LANE] / acc_sc[:, LANE:2 * LANE]


_ATT_SCRATCH = [pltpu.VMEM((ATT_TILE, LANE), F32), pltpu.VMEM((ATT_TILE, 2 * LANE), F32),
                pltpu.VMEM((ATT_TILE, ATT_TILE), BF16), pltpu.VMEM((ATT_TILE, LANE), F32),
                pltpu.VMEM((ATT_TILE, ATT_TILE), BF16), pltpu.VMEM((ATT_TILE, LANE), F32)]


def _att_specs(v_width):
    return [pl.BlockSpec((ATT_TILE, QK_W), lambda h, i: (i, h)),
            pl.BlockSpec((L_BUF, QK_W), lambda h, i: (0, h)),
            pl.BlockSpec((L_BUF, v_width), lambda h, i: (0, h))]


def _mla_attn_kernel(q_ref, k_ref, v_ref, o_ref, *scratch):
    mask = lambda r, c: jnp.right_shift(c, CHUNK_SHIFT) <= jnp.right_shift(r, CHUNK_SHIFT)
    out = _flash_tile(pl.program_id(1), q_ref, k_ref, v_ref, *scratch, diag_mask=mask)
    o_ref[...] = out.astype(o_ref.dtype)


def _mla_attn(q, k, v):
    return pl.pallas_call(
        _mla_attn_kernel,
        grid=(MLA_HEADS, N_AT),
        in_specs=_att_specs(MLA_V),
        out_specs=pl.BlockSpec((ATT_TILE, MLA_V), lambda h, i: (i, h)),
        out_shape=jax.ShapeDtypeStruct((L_BUF, MLA_HEADS * MLA_V), BF16),
        scratch_shapes=_ATT_SCRATCH,
        compiler_params=_cparams(("parallel", "arbitrary")),
        name="mla_attn",
    )(q, k, v)


def _fox_attn_kernel(q_ref, k_ref, v_ref, c_ref, gate_ref, o_ref, *scratch):
    lane = lax.broadcasted_iota(jnp.int32, (ATT_TILE, LANE), 1)
    cq = jnp.sum(jnp.where(lane == pl.program_id(0), c_ref[...], 0.0), axis=-1, keepdims=True)
    out = _flash_tile(pl.program_id(1), q_ref, k_ref, v_ref, *scratch,
                      diag_mask=lambda r, c: c <= r, row_shift=cq)
    o_ref[...] = (out * jax.nn.sigmoid(gate_ref[...])).astype(o_ref.dtype)


def _fox_attn(q, k, v, c, gate):
    return pl.pallas_call(
        _fox_attn_kernel,
        grid=(FOX_HEADS, N_AT),
        in_specs=_att_specs(FOX_HD) + [
            pl.BlockSpec((ATT_TILE, LANE), lambda h, i: (i, 0)),
            pl.BlockSpec((ATT_TILE, FOX_HD), lambda h, i: (i, h)),
        ],
        out_specs=pl.BlockSpec((ATT_TILE, FOX_HD), lambda h, i: (i, h)),
        out_shape=jax.ShapeDtypeStruct((L_BUF, FOX_W), BF16),
        scratch_shapes=_ATT_SCRATCH,
        compiler_params=_cparams(("parallel", "arbitrary")),
        name="fox_attn",
    )(q, k, v, c, gate)


def _mix_out_kernel(a_ref, b_ref, wa_ref, wb_ref, h_ref, g_ref, o_ref):
    mix = (jnp.dot(a_ref[...], wa_ref[...], preferred_element_type=F32)
           + jnp.dot(b_ref[...], wb_ref[...], preferred_element_type=F32))
    out = h_ref[...] + _rms(mix, g_ref[...])
    row = pl.program_id(0) * TM_OUT + lax.broadcasted_iota(jnp.int32, (TM_OUT, 1), 0)
    o_ref[...] = jnp.where(row >= ROW_PAD, out, 0.0)


def _mix_out(l, a, b, w_o, h, g):
    return pl.pallas_call(
        _mix_out_kernel,
        grid=(L_BUF // TM_OUT,),
        in_specs=[
            pl.BlockSpec((TM_OUT, MLA_HEADS * MLA_V), lambda i: (i, 0)),
            pl.BlockSpec((TM_OUT, FOX_W), lambda i: (i, 0)),
            pl.BlockSpec((None, MLA_HEADS * MLA_V, D_MODEL), lambda i: (l, 0, 0)),
            pl.BlockSpec((None, FOX_W, D_MODEL), lambda i: (l, 1, 0)),
            pl.BlockSpec((TM_OUT, D_MODEL), lambda i: (i, 0)),
            pl.BlockSpec((1, D_MODEL), lambda i: (0, 0)),
        ],
        out_specs=pl.BlockSpec((TM_OUT, D_MODEL), lambda i: (i, 0)),
        out_shape=jax.ShapeDtypeStruct((L_BUF, D_MODEL), F32),
        compiler_params=_cparams(("parallel",)),
        name="mix_out",
    )(a, b, w_o, w_o, h, g)


def _gelu_tanh(x):
    return 0.5 * x * (1.0 + jnp.tanh(np.sqrt(2.0 / np.pi).astype(np.float32) * (x + 0.044715 * (x * x * x))))


def _ffn_kernel(h_ref, halo_ref, gpre_ref, wg_ref, wu_ref, cwg_ref, cwu_ref, cbg_ref, cbu_ref,
                wd_ref, gpost_ref, o_ref, xn_ref, ug_ref, uu_ref):
    c = pl.program_id(1)

    @pl.when(c == 0)
    def _():
        xn_ref[0:HALO, :] = _rms(halo_ref[...], gpre_ref[...]).astype(BF16)

        def body(r, carry):
            src = pl.ds(pl.multiple_of(r * 64, 64), 64)
            dst = pl.ds(pl.multiple_of(HALO + r * 64, 16), 64)
            xn_ref[dst, :] = _rms(h_ref[src, :], gpre_ref[...]).astype(BF16)
            return carry
        lax.fori_loop(0, TM_FFN // 64, body, 0)
        o_ref[...] = jnp.zeros_like(o_ref)

    xn = xn_ref[...]
    ug_ref[...] = jnp.dot(xn, wg_ref[...], preferred_element_type=F32)
    uu_ref[...] = jnp.dot(xn, wu_ref[...], preferred_element_type=F32)

    def conv(u_ref, w_ref, b_ref):
        acc = b_ref[...] + w_ref[CONV_K - 1:CONV_K, :] * u_ref[HALO:HALO + TM_FFN, :]
        for t in range(1, CONV_K):
            acc = acc + w_ref[CONV_K - 1 - t:CONV_K - t, :] * u_ref[HALO - t:HALO - t + TM_FFN, :]
        return acc

    act = _gelu_tanh(conv(ug_ref, cwg_ref, cbg_ref)) * conv(uu_ref, cwu_ref, cbu_ref)
    o_ref[...] += jnp.dot(act.astype(BF16), wd_ref[...], preferred_element_type=F32)

    @pl.when(c == pl.num_programs(1) - 1)
    def _():
        out = h_ref[...] + _rms(o_ref[...], gpost_ref[...])
        row = pl.program_id(0) * TM_FFN + lax.broadcasted_iota(jnp.int32, (TM_FFN, 1), 0)
        o_ref[...] = jnp.where(row >= ROW_PAD, out, 0.0)


def _ffn(l, h, gpre, w_up, w_conv, b_conv, w_down, gpost):
    n_fc = D_FF // FC_FFN
    halo_blocks = TM_FFN // HALO
    return pl.pallas_call(
        _ffn_kernel,
        grid=(L_BUF // TM_FFN, n_fc),
        in_specs=[
            pl.BlockSpec((TM_FFN, D_MODEL), lambda i, c: (i, 0)),
            pl.BlockSpec((HALO, D_MODEL), lambda i, c: (jnp.maximum(i * halo_blocks - 1, 0), 0)),
            pl.BlockSpec((1, D_MODEL), lambda i, c: (0, 0)),
            pl.BlockSpec((None, D_MODEL, FC_FFN), lambda i, c: (l, 0, c)),
            pl.BlockSpec((None, D_MODEL, FC_FFN), lambda i, c: (l, 0, n_fc + c)),
            pl.BlockSpec((CONV_K, FC_FFN), lambda i, c: (0, c)),
            pl.BlockSpec((CONV_K, FC_FFN), lambda i, c: (0, n_fc + c)),
            pl.BlockSpec((1, FC_FFN), lambda i, c: (0, c)),
            pl.BlockSpec((1, FC_FFN), lambda i, c: (0, n_fc + c)),
            pl.BlockSpec((None, FC_FFN, D_MODEL), lambda i, c: (l, c, 0)),
            pl.BlockSpec((1, D_MODEL), lambda i, c: (0, 0)),
        ],
        out_specs=pl.BlockSpec((TM_FFN, D_MODEL), lambda i, c: (i, 0)),
        out_shape=jax.ShapeDtypeStruct((L_BUF, D_MODEL), F32),
        scratch_shapes=[pltpu.VMEM((HALO + TM_FFN, D_MODEL), BF16),
                        pltpu.VMEM((HALO + TM_FFN, FC_FFN), F32),
                        pltpu.VMEM((HALO + TM_FFN, FC_FFN), F32)],
        compiler_params=_cparams(("parallel", "arbitrary")),
        name="conv_ffn",
    )(h, h, gpre, w_up, w_up, w_conv, w_conv, b_conv, b_conv, w_down, gpost)


def _rotate_half_cols(w):
    half = w.shape[-1] // 2
    return jnp.concatenate([-w[..., half:], w[..., :half]], axis=-1)


def _pad_cols(w, width):
    return jnp.pad(w, [(0, 0)] * (w.ndim - 1) + [(0, width - w.shape[-1])])


def _pack_w_in(w):
    o = np.cumsum([0, MLA_Q_LORA, MLA_KV_LORA, MLA_ROPE, FOX_W, FOX_W, FOX_W, FOX_W, FOX_HEADS])
    c_q, c_kv, k_rope, fq, fk, fv, fg, ff = [w[..., o[n]:o[n + 1]].astype(BF16) for n in range(8)]
    tail = jnp.concatenate([_pad_cols(k_rope, LANE), _pad_cols(_rotate_half_cols(k_rope), LANE),
                            _pad_cols(ff, 2 * LANE)], axis=-1)
    return jnp.concatenate([tail, c_q, c_kv, fq, fk, fv, fg], axis=-1)


def _pack_w_q_up(w):
    w = w.astype(BF16).reshape(DEPTH, MLA_Q_LORA, MLA_HEADS, MLA_NOPE + MLA_ROPE)
    nope, rope = w[..., :MLA_NOPE], w[..., MLA_NOPE:]
    packed = jnp.concatenate([nope, _pad_cols(rope, LANE), _pad_cols(_rotate_half_cols(rope), LANE)], axis=-1)
    return packed.reshape(DEPTH, MLA_Q_LORA, MLA_HEADS * Q_HEAD_W)


def _rope_tables():
    pos = jnp.maximum(jnp.arange(L_BUF, dtype=jnp.int32) - ROW_PAD, 0).astype(F32)
    half = MLA_ROPE // 2
    inv_freq = ROPE_THETA ** (-jnp.arange(half, dtype=F32) / half)
    ang = pos[:, None] * inv_freq[None, :]
    zeros = jnp.zeros((L_BUF, LANE - MLA_ROPE), F32)
    cos, sin = jnp.cos(ang), jnp.sin(ang)
    return (jnp.concatenate([cos, cos, zeros], axis=1), jnp.concatenate([sin, sin, zeros], axis=1))


def kernel(x, meta_tokens, ln_mix_pre, w_in, b_forget, g_q_latent, g_kv_latent, w_q_up, w_kv_up,
           g_fox_q, g_fox_k, w_out, ln_mix_post, ln_ffn_pre, w_ffn_up, w_ffn_conv, b_ffn_conv,
           w_ffn_down, ln_ffn_post):
    assert x.shape == (1, SEQ, D_MODEL), x.shape
    h = jnp.concatenate([jnp.zeros((ROW_PAD, D_MODEL), x.dtype), meta_tokens.astype(x.dtype), x[0]], axis=0)
    cos_t, sin_t = _rope_tables()
    tri = (lax.broadcasted_iota(jnp.int32, (TM_PROJ, TRI_K), 0)
           >= lax.broadcasted_iota(jnp.int32, (TM_PROJ, TRI_K), 1)).astype(BF16)
    row2d = lambda v: v.reshape(1, -1).astype(F32)
    w_in_p, w_q_p = _pack_w_in(w_in), _pack_w_q_up(w_q_up)
    w_kv_b, w_o_b = w_kv_up.astype(BF16), w_out.astype(BF16)
    w_up_b, w_down_b = w_ffn_up.astype(BF16), w_ffn_down.astype(BF16)
    assert MLA_HEADS * MLA_V == FOX_W

    for l in range(DEPTH):
        q, k, v, qf, kf, vf, c, gate = _proj(
            l, h, row2d(ln_mix_pre[l]), w_in_p, cos_t, sin_t, row2d(g_q_latent[l]), row2d(g_kv_latent[l]),
            row2d(g_fox_q[l]), row2d(g_fox_k[l]), _pad_cols(row2d(b_forget[l]), LANE), w_q_p, w_kv_b, tri)
        a = _mla_attn(q, k, v)
        b = _fox_attn(qf, kf, vf, c, gate)
        h = _mix_out(l, a, b, w_o_b, h, row2d(ln_mix_post[l]))
        h = _ffn(l, h, row2d(ln_ffn_pre[l]), w_up_b, w_ffn_conv[l].astype(F32),
                 row2d(b_ffn_conv[l]), w_down_b, row2d(ln_ffn_post[l]))

    return h[ROW_PAD + N_META:][None]
```

```python
import jax
import jax.numpy as jnp
import numpy as np
from jax import lax
from jax.experimental import pallas as pl
from jax.experimental.pallas import tpu as pltpu

F32 = jnp.float32
BF16 = jnp.bfloat16

D_MODEL = 2048
SEQ = 8192
DEPTH = 4
CHUNK = 64
CHUNK_SHIFT = 6
N_META = 16
MLA_HEADS = 8
MLA_Q_LORA = 512
MLA_KV_LORA = 512
MLA_NOPE = 128
MLA_ROPE = 64
MLA_V = 128
ROPE_THETA = 10000.0
FOX_HEADS = 8
FOX_HD = 128
FOX_W = FOX_HEADS * FOX_HD
D_FF = 5632
CONV_K = 3
EPS = 1e-6
NEG = -1e30

LANE = 128
Q_TILE = 256
ROW_PAD = Q_TILE - N_META
L_BUF = ROW_PAD + N_META + SEQ
ATT_TILE = 768
ROW_BLK = 256
N_AT = L_BUF // ATT_TILE

LOG2E = float(np.log2(np.e))
MLA_SCALE = (MLA_NOPE + MLA_ROPE) ** -0.5 * LOG2E
FOX_SCALE = FOX_HD ** -0.5 * LOG2E

TAIL_W = 4 * LANE
IN_PACKED = MLA_Q_LORA + MLA_KV_LORA + 4 * FOX_W + TAIL_W
GATE_COL0 = MLA_Q_LORA + MLA_KV_LORA + 3 * FOX_W
Q_HEAD_W = 3 * LANE
QK_W = 2 * LANE

TM_PROJ = 1408
TN_PROJ = 512
TM_OUT = 384
TM_FFN = 768
FC_FFN = 512
HALO = 16

VMEM_LIMIT = 56 * 1024 * 1024


def _rms(x, g):
    ms = jnp.mean(x * x, axis=-1, keepdims=True)
    return x * lax.rsqrt(ms + EPS) * g


def _split3(x):
    hi = x.astype(BF16).astype(F32)
    r = x - hi
    mid = r.astype(BF16).astype(F32)
    return hi, mid, r - mid


def _cparams(sem):
    return pltpu.CompilerParams(dimension_semantics=sem, vmem_limit_bytes=VMEM_LIMIT)


def _proj_in_kernel(x_ref, g_ref, w_ref, o_ref, xn_ref):
    @pl.when(pl.program_id(1) == 0)
    def _():
        def body(r, carry):
            rows = pl.ds(pl.multiple_of(r * 64, 64), 64)
            xn_ref[rows, :] = _rms(x_ref[rows, :], g_ref[...]).astype(BF16)
            return carry
        lax.fori_loop(0, TM_PROJ // 64, body, 0)

    o_ref[...] = jnp.dot(xn_ref[...], w_ref[...], preferred_element_type=F32)


def _proj_in(l, h, g, w):
    return pl.pallas_call(
        _proj_in_kernel,
        grid=(L_BUF // TM_PROJ, IN_PACKED // TN_PROJ),
        in_specs=[
            pl.BlockSpec((TM_PROJ, D_MODEL), lambda i, j: (i, 0)),
            pl.BlockSpec((1, D_MODEL), lambda i, j: (0, 0)),
            pl.BlockSpec((None, D_MODEL, TN_PROJ), lambda i, j: (l, 0, j)),
        ],
        out_specs=pl.BlockSpec((TM_PROJ, TN_PROJ), lambda i, j: (i, j)),
        out_shape=jax.ShapeDtypeStruct((L_BUF, IN_PACKED), F32),
        scratch_shapes=[pltpu.VMEM((TM_PROJ, D_MODEL), BF16)],
        compiler_params=_cparams(("parallel", "arbitrary")),
        name="proj_in",
    )(h, g, w)


def _prep_kernel(cq_ref, ckv_ref, fq_ref, fk_ref, fv_ref, tail_ref, cos_ref, sin_ref,
                 gql_ref, gkvl_ref, gfq_ref, gfk_ref, bf_ref, wq_ref, wkv_ref, tri_ref,
                 q_ref, k_ref, v_ref, qf_ref, kf_ref, vf_ref, c_ref, carry_ref):
    i = pl.program_id(0)
    lane = lax.broadcasted_iota(jnp.int32, (Q_TILE, LANE), 1)
    is_pad = (i * Q_TILE + lax.broadcasted_iota(jnp.int32, (Q_TILE, 1), 0)) < ROW_PAD
    cos = cos_ref[...]
    sin = sin_ref[...]
    q_flag = jnp.where(lane == MLA_ROPE, 1.0, 0.0)
    k_flag = jnp.where((lane == MLA_ROPE) & is_pad, NEG, 0.0)
    kr = (tail_ref[:, 0:LANE] * cos + tail_ref[:, LANE:2 * LANE] * sin + k_flag).astype(BF16)
    cqn = _rms(cq_ref[...], gql_ref[...]).astype(BF16)
    ckvn = _rms(ckv_ref[...], gkvl_ref[...]).astype(BF16)
    for h in range(MLA_HEADS):
        qh = jnp.dot(cqn, wq_ref[:, Q_HEAD_W * h:Q_HEAD_W * (h + 1)], preferred_element_type=F32)
        qr = qh[:, LANE:2 * LANE] * cos + qh[:, 2 * LANE:3 * LANE] * sin
        q_ref[h, :, 0:LANE] = (qh[:, 0:LANE] * MLA_SCALE).astype(BF16)
        q_ref[h, :, LANE:QK_W] = (qr * MLA_SCALE + q_flag).astype(BF16)
        kvh = jnp.dot(ckvn, wkv_ref[:, 2 * LANE * h:2 * LANE * (h + 1)], preferred_element_type=F32)
        k_ref[h, :, 0:LANE] = kvh[:, 0:LANE].astype(BF16)
        k_ref[h, :, LANE:QK_W] = kr
        v_ref[h] = kvh[:, LANE:2 * LANE].astype(BF16)

    z = tail_ref[:, 2 * LANE:3 * LANE] + bf_ref[...]
    logf = (jnp.minimum(z, 0.0) - jnp.log(1.0 + jnp.exp(-jnp.abs(z)))) * LOG2E
    hi, mid, lo = _split3(logf)
    tri = tri_ref[...]
    cs = (jnp.dot(tri, hi.astype(BF16), preferred_element_type=F32)
          + jnp.dot(tri, mid.astype(BF16), preferred_element_type=F32)
          + jnp.dot(tri, lo.astype(BF16), preferred_element_type=F32))

    @pl.when(i == 0)
    def _():
        carry_ref[...] = jnp.zeros_like(carry_ref)

    c = cs + carry_ref[0:1, :]
    c_ref[...] = c
    carry_ref[...] = jnp.broadcast_to(c[Q_TILE - 1:Q_TILE, :], carry_ref.shape)

    qf_ext = jnp.where(lane < 3, -1.0, 0.0).astype(BF16)
    for h in range(FOX_HEADS):
        cols = slice(FOX_HD * h, FOX_HD * (h + 1))
        qf_ref[h, :, 0:FOX_HD] = (_rms(fq_ref[:, cols], gfq_ref[...]) * FOX_SCALE).astype(BF16)
        qf_ref[h, :, FOX_HD:QK_W] = qf_ext
        kf_ref[h, :, 0:FOX_HD] = _rms(fk_ref[:, cols], gfk_ref[...]).astype(BF16)
        ck = jnp.sum(jnp.where(lane == h, c, 0.0), axis=-1, keepdims=True)
        ck_hi, ck_mid, ck_lo = _split3(jnp.where(is_pad, -NEG, ck))
        ext = jnp.where(lane == 0, ck_hi, jnp.where(lane == 1, ck_mid, jnp.where(lane == 2, ck_lo, 0.0)))
        kf_ref[h, :, FOX_HD:QK_W] = ext.astype(BF16)
        vf_ref[h] = fv_ref[:, cols].astype(BF16)


def _prep(l, proj, cos_t, sin_t, gql, gkvl, gfq, gfk, bf_pad, wq, wkv, tri):
    tm = Q_TILE
    row = lambda i: (i, 0)
    const = lambda i: (0, 0)
    head_out = lambda w: pl.BlockSpec((MLA_HEADS, tm, w), lambda i: (0, i, 0))
    return pl.pallas_call(
        _prep_kernel,
        grid=(L_BUF // tm,),
        in_specs=[
            pl.BlockSpec((tm, MLA_Q_LORA), lambda i: (i, 0)),
            pl.BlockSpec((tm, MLA_KV_LORA), lambda i: (i, 1)),
            pl.BlockSpec((tm, FOX_W), lambda i: (i, 1)),
            pl.BlockSpec((tm, FOX_W), lambda i: (i, 2)),
            pl.BlockSpec((tm, FOX_W), lambda i: (i, 3)),
            pl.BlockSpec((tm, TAIL_W), lambda i: (i, IN_PACKED // TAIL_W - 1)),
            pl.BlockSpec((tm, LANE), row),
            pl.BlockSpec((tm, LANE), row),
            pl.BlockSpec((1, MLA_Q_LORA), const),
            pl.BlockSpec((1, MLA_KV_LORA), const),
            pl.BlockSpec((1, FOX_HD), const),
            pl.BlockSpec((1, FOX_HD), const),
            pl.BlockSpec((1, LANE), const),
            pl.BlockSpec((None, MLA_Q_LORA, MLA_HEADS * Q_HEAD_W), lambda i: (l, 0, 0)),
            pl.BlockSpec((None, MLA_KV_LORA, MLA_HEADS * 2 * LANE), lambda i: (l, 0, 0)),
            pl.BlockSpec((tm, tm), const),
        ],
        out_specs=[
            head_out(QK_W), head_out(QK_W), head_out(MLA_V),
            head_out(QK_W), head_out(QK_W), head_out(FOX_HD),
            pl.BlockSpec((tm, LANE), row),
        ],
        out_shape=[
            jax.ShapeDtypeStruct((MLA_HEADS, L_BUF, QK_W), BF16),
            jax.ShapeDtypeStruct((MLA_HEADS, L_BUF, QK_W), BF16),
            jax.ShapeDtypeStruct((MLA_HEADS, L_BUF, MLA_V), BF16),
            jax.ShapeDtypeStruct((FOX_HEADS, L_BUF, QK_W), BF16),
            jax.ShapeDtypeStruct((FOX_HEADS, L_BUF, QK_W), BF16),
            jax.ShapeDtypeStruct((FOX_HEADS, L_BUF, FOX_HD), BF16),
            jax.ShapeDtypeStruct((L_BUF, LANE), F32),
        ],
        scratch_shapes=[pltpu.VMEM((8, LANE), F32)],
        compiler_params=_cparams(("arbitrary",)),
        name="attn_prep",
    )(proj, proj, proj, proj, proj, proj, cos_t, sin_t, gql, gkvl, gfq, gfk, bf_pad, wq, wkv, tri)


def _scores(q, k):
    return lax.dot_general(q, k, (((1,), (1,)), ((), ())), preferred_element_type=F32)


def _flash_tile(i, q_ref, k_ref, v_ref, m_sc, acc_sc, p_sc, alpha_sc, p_new, alpha_new,
                diag_mask, row_shift=None):
    m_sc[...] = jnp.full_like(m_sc, NEG)
    acc_sc[...] = jnp.zeros_like(acc_sc)
    row_blocks = [slice(rb * ROW_BLK, (rb + 1) * ROW_BLK) for rb in range(ATT_TILE // ROW_BLK)]

    def chunk_start(j):
        return pl.multiple_of(j * ATT_TILE, ATT_TILE)

    def flush(j):
        v = v_ref[0, pl.ds(chunk_start(j), ATT_TILE), :]
        v_ones = jnp.concatenate([v, jnp.ones_like(v)], axis=1)
        for rows in row_blocks:
            pv = acc_sc[rows, :] * jnp.tile(alpha_sc[rows, :], (1, 2))
            for kc in range(0, ATT_TILE, 2 * LANE):
                pv = pv + jnp.dot(p_sc[rows, kc:kc + 2 * LANE], v_ones[kc:kc + 2 * LANE, :],
                                  preferred_element_type=F32)
            acc_sc[rows, :] = pv

    def scores(j, rows, ncols):
        return _scores(q_ref[0, rows, :], k_ref[0, pl.ds(chunk_start(j), ncols), :])

    def softmax_rows(rows, s, p_dst, alpha_dst):
        ncols = s.shape[1]
        m_prev = m_sc[rows, :]
        m_chunk = jnp.max(s, axis=-1, keepdims=True)
        if row_shift is not None:
            m_chunk = m_chunk + row_shift[rows, :]
        m_new = jnp.maximum(m_prev, m_chunk)
        alpha_dst[rows, :] = jnp.exp2(m_prev - m_new)
        sub = m_new if row_shift is None else m_new - row_shift[rows, :]
        p_dst[rows, 0:ncols] = jnp.exp2(s - jnp.tile(sub, (1, ncols // LANE))).astype(BF16)
        m_sc[rows, :] = m_new

    def make_pending():
        p_sc[...] = p_new[...]
        alpha_sc[...] = alpha_new[...]

    for rows in row_blocks:
        ncols = rows.stop
        s = scores(i, rows, ncols)
        r = lax.broadcasted_iota(jnp.int32, s.shape, 0) + rows.start
        c = lax.broadcasted_iota(jnp.int32, s.shape, 1)
        if ncols < ATT_TILE:
            p_sc[rows, ncols:] = jnp.zeros((ROW_BLK, ATT_TILE - ncols), BF16)
        softmax_rows(rows, jnp.where(diag_mask(r, c), s, NEG), p_sc, alpha_sc)

    def body(j, pending):
        for rows in row_blocks:
            softmax_rows(rows, scores(j, rows, ATT_TILE), p_new, alpha_new)
        flush(pending)
        make_pending()
        return j
    flush(lax.fori_loop(0, i, body, i))

    return acc_sc[:, 0:LANE] / acc_sc[:, LANE:2 * LANE]


_ATT_SCRATCH = [pltpu.VMEM((ATT_TILE, LANE), F32), pltpu.VMEM((ATT_TILE, 2 * LANE), F32),
                pltpu.VMEM((ATT_TILE, ATT_TILE), BF16), pltpu.VMEM((ATT_TILE, LANE), F32),
                pltpu.VMEM((ATT_TILE, ATT_TILE), BF16), pltpu.VMEM((ATT_TILE, LANE), F32)]


def _mla_attn_kernel(q_ref, k_ref, v_ref, o_ref, *scratch):
    mask = lambda r, c: jnp.right_shift(c, CHUNK_SHIFT) <= jnp.right_shift(r, CHUNK_SHIFT)
    out = _flash_tile(pl.program_id(1), q_ref, k_ref, v_ref, *scratch, diag_mask=mask)
    o_ref[...] = out.astype(o_ref.dtype)


def _mla_attn(q, k, v):
    return pl.pallas_call(
        _mla_attn_kernel,
        grid=(MLA_HEADS, N_AT),
        in_specs=[
            pl.BlockSpec((1, ATT_TILE, QK_W), lambda h, i: (h, i, 0)),
            pl.BlockSpec((1, L_BUF, QK_W), lambda h, i: (h, 0, 0)),
            pl.BlockSpec((1, L_BUF, MLA_V), lambda h, i: (h, 0, 0)),
        ],
        out_specs=pl.BlockSpec((ATT_TILE, MLA_V), lambda h, i: (i, h)),
        out_shape=jax.ShapeDtypeStruct((L_BUF, MLA_HEADS * MLA_V), BF16),
        scratch_shapes=_ATT_SCRATCH,
        compiler_params=_cparams(("parallel", "arbitrary")),
        name="mla_attn",
    )(q, k, v)


def _fox_attn_kernel(q_ref, k_ref, v_ref, c_ref, gate_ref, o_ref, *scratch):
    lane = lax.broadcasted_iota(jnp.int32, (ATT_TILE, LANE), 1)
    cq = jnp.sum(jnp.where(lane == pl.program_id(0), c_ref[...], 0.0), axis=-1, keepdims=True)
    out = _flash_tile(pl.program_id(1), q_ref, k_ref, v_ref, *scratch,
                      diag_mask=lambda r, c: c <= r, row_shift=cq)
    o_ref[...] = (out * jax.nn.sigmoid(gate_ref[...])).astype(o_ref.dtype)


def _fox_attn(q, k, v, c, proj):
    return pl.pallas_call(
        _fox_attn_kernel,
        grid=(FOX_HEADS, N_AT),
        in_specs=[
            pl.BlockSpec((1, ATT_TILE, QK_W), lambda h, i: (h, i, 0)),
            pl.BlockSpec((1, L_BUF, QK_W), lambda h, i: (h, 0, 0)),
            pl.BlockSpec((1, L_BUF, FOX_HD), lambda h, i: (h, 0, 0)),
            pl.BlockSpec((ATT_TILE, LANE), lambda h, i: (i, 0)),
            pl.BlockSpec((ATT_TILE, FOX_HD), lambda h, i: (i, GATE_COL0 // FOX_HD + h)),
        ],
        out_specs=pl.BlockSpec((ATT_TILE, FOX_HD), lambda h, i: (i, h)),
        out_shape=jax.ShapeDtypeStruct((L_BUF, FOX_W), BF16),
        scratch_shapes=_ATT_SCRATCH,
        compiler_params=_cparams(("parallel", "arbitrary")),
        name="fox_attn",
    )(q, k, v, c, proj)


def _mix_out_kernel(a_ref, b_ref, wa_ref, wb_ref, h_ref, g_ref, o_ref):
    mix = (jnp.dot(a_ref[...], wa_ref[...], preferred_element_type=F32)
           + jnp.dot(b_ref[...], wb_ref[...], preferred_element_type=F32))
    out = h_ref[...] + _rms(mix, g_ref[...])
    row = pl.program_id(0) * TM_OUT + lax.broadcasted_iota(jnp.int32, (TM_OUT, 1), 0)
    o_ref[...] = jnp.where(row >= ROW_PAD, out, 0.0)


def _mix_out(l, a, b, w_o, h, g):
    return pl.pallas_call(
        _mix_out_kernel,
        grid=(L_BUF // TM_OUT,),
        in_specs=[
            pl.BlockSpec((TM_OUT, MLA_HEADS * MLA_V), lambda i: (i, 0)),
            pl.BlockSpec((TM_OUT, FOX_W), lambda i: (i, 0)),
            pl.BlockSpec((None, MLA_HEADS * MLA_V, D_MODEL), lambda i: (l, 0, 0)),
            pl.BlockSpec((None, FOX_W, D_MODEL), lambda i: (l, 1, 0)),
            pl.BlockSpec((TM_OUT, D_MODEL), lambda i: (i, 0)),
            pl.BlockSpec((1, D_MODEL), lambda i: (0, 0)),
        ],
        out_specs=pl.BlockSpec((TM_OUT, D_MODEL), lambda i: (i, 0)),
        out_shape=jax.ShapeDtypeStruct((L_BUF, D_MODEL), F32),
        compiler_params=_cparams(("parallel",)),
        name="mix_out",
    )(a, b, w_o, w_o, h, g)


def _gelu_tanh(x):
    return 0.5 * x * (1.0 + jnp.tanh(np.sqrt(2.0 / np.pi).astype(np.float32) * (x + 0.044715 * (x * x * x))))


def _ffn_kernel(h_ref, halo_ref, gpre_ref, wg_ref, wu_ref, cwg_ref, cwu_ref, cbg_ref, cbu_ref,
                wd_ref, gpost_ref, o_ref, xn_ref, ug_ref, uu_ref):
    c = pl.program_id(1)

    @pl.when(c == 0)
    def _():
        xn_ref[0:HALO, :] = _rms(halo_ref[...], gpre_ref[...]).astype(BF16)

        def body(r, carry):
            src = pl.ds(pl.multiple_of(r * 64, 64), 64)
            dst = pl.ds(pl.multiple_of(HALO + r * 64, 16), 64)
            xn_ref[dst, :] = _rms(h_ref[src, :], gpre_ref[...]).astype(BF16)
            return carry
        lax.fori_loop(0, TM_FFN // 64, body, 0)
        o_ref[...] = jnp.zeros_like(o_ref)

    xn = xn_ref[...]
    ug_ref[...] = jnp.dot(xn, wg_ref[...], preferred_element_type=F32)
    uu_ref[...] = jnp.dot(xn, wu_ref[...], preferred_element_type=F32)

    def conv(u_ref, w_ref, b_ref):
        acc = b_ref[...] + w_ref[CONV_K - 1:CONV_K, :] * u_ref[HALO:HALO + TM_FFN, :]
        for t in range(1, CONV_K):
            acc = acc + w_ref[CONV_K - 1 - t:CONV_K - t, :] * u_ref[HALO - t:HALO - t + TM_FFN, :]
        return acc

    act = _gelu_tanh(conv(ug_ref, cwg_ref, cbg_ref)) * conv(uu_ref, cwu_ref, cbu_ref)
    o_ref[...] += jnp.dot(act.astype(BF16), wd_ref[...], preferred_element_type=F32)

    @pl.when(c == pl.num_programs(1) - 1)
    def _():
        out = h_ref[...] + _rms(o_ref[...], gpost_ref[...])
        row = pl.program_id(0) * TM_FFN + lax.broadcasted_iota(jnp.int32, (TM_FFN, 1), 0)
        o_ref[...] = jnp.where(row >= ROW_PAD, out, 0.0)


def _ffn(l, h, gpre, w_up, w_conv, b_conv, w_down, gpost):
    n_fc = D_FF // FC_FFN
    halo_blocks = TM_FFN // HALO
    return pl.pallas_call(
        _ffn_kernel,
        grid=(L_BUF // TM_FFN, n_fc),
        in_specs=[
            pl.BlockSpec((TM_FFN, D_MODEL), lambda i, c: (i, 0)),
            pl.BlockSpec((HALO, D_MODEL), lambda i, c: (jnp.maximum(i * halo_blocks - 1, 0), 0)),
            pl.BlockSpec((1, D_MODEL), lambda i, c: (0, 0)),
            pl.BlockSpec((None, D_MODEL, FC_FFN), lambda i, c: (l, 0, c)),
            pl.BlockSpec((None, D_MODEL, FC_FFN), lambda i, c: (l, 0, n_fc + c)),
            pl.BlockSpec((CONV_K, FC_FFN), lambda i, c: (0, c)),
            pl.BlockSpec((CONV_K, FC_FFN), lambda i, c: (0, n_fc + c)),
            pl.BlockSpec((1, FC_FFN), lambda i, c: (0, c)),
            pl.BlockSpec((1, FC_FFN), lambda i, c: (0, n_fc + c)),
            pl.BlockSpec((None, FC_FFN, D_MODEL), lambda i, c: (l, c, 0)),
            pl.BlockSpec((1, D_MODEL), lambda i, c: (0, 0)),
        ],
        out_specs=pl.BlockSpec((TM_FFN, D_MODEL), lambda i, c: (i, 0)),
        out_shape=jax.ShapeDtypeStruct((L_BUF, D_MODEL), F32),
        scratch_shapes=[pltpu.VMEM((HALO + TM_FFN, D_MODEL), BF16),
                        pltpu.VMEM((HALO + TM_FFN, FC_FFN), F32),
                        pltpu.VMEM((HALO + TM_FFN, FC_FFN), F32)],
        compiler_params=_cparams(("parallel", "arbitrary")),
        name="conv_ffn",
    )(h, h, gpre, w_up, w_up, w_conv, w_conv, b_conv, b_conv, w_down, gpost)


def _rotate_half_cols(w):
    half = w.shape[-1] // 2
    return jnp.concatenate([-w[..., half:], w[..., :half]], axis=-1)


def _pad_cols(w, width):
    return jnp.pad(w, [(0, 0)] * (w.ndim - 1) + [(0, width - w.shape[-1])])


def _pack_w_in(w):
    o = np.cumsum([0, MLA_Q_LORA, MLA_KV_LORA, MLA_ROPE, FOX_W, FOX_W, FOX_W, FOX_W, FOX_HEADS])
    c_q, c_kv, k_rope, fq, fk, fv, fg, ff = [w[..., o[n]:o[n + 1]].astype(BF16) for n in range(8)]
    tail = jnp.concatenate([_pad_cols(k_rope, LANE), _pad_cols(_rotate_half_cols(k_rope), LANE),
                            _pad_cols(ff, 2 * LANE)], axis=-1)
    return jnp.concatenate([c_q, c_kv, fq, fk, fv, fg, tail], axis=-1)


def _pack_w_q_up(w):
    w = w.astype(BF16).reshape(DEPTH, MLA_Q_LORA, MLA_HEADS, MLA_NOPE + MLA_ROPE)
    nope, rope = w[..., :MLA_NOPE], w[..., MLA_NOPE:]
    packed = jnp.concatenate([nope, _pad_cols(rope, LANE), _pad_cols(_rotate_half_cols(rope), LANE)], axis=-1)
    return packed.reshape(DEPTH, MLA_Q_LORA, MLA_HEADS * Q_HEAD_W)


def _rope_tables():
    pos = jnp.maximum(jnp.arange(L_BUF, dtype=jnp.int32) - ROW_PAD, 0).astype(F32)
    half = MLA_ROPE // 2
    inv_freq = ROPE_THETA ** (-jnp.arange(half, dtype=F32) / half)
    ang = pos[:, None] * inv_freq[None, :]
    zeros = jnp.zeros((L_BUF, LANE - MLA_ROPE), F32)
    cos, sin = jnp.cos(ang), jnp.sin(ang)
    return (jnp.concatenate([cos, cos, zeros], axis=1), jnp.concatenate([sin, sin, zeros], axis=1))


def kernel(x, meta_tokens, ln_mix_pre, w_in, b_forget, g_q_latent, g_kv_latent, w_q_up, w_kv_up,
           g_fox_q, g_fox_k, w_out, ln_mix_post, ln_ffn_pre, w_ffn_up, w_ffn_conv, b_ffn_conv,
           w_ffn_down, ln_ffn_post):
    assert x.shape == (1, SEQ, D_MODEL), x.shape
    h = jnp.concatenate([jnp.zeros((ROW_PAD, D_MODEL), x.dtype), meta_tokens.astype(x.dtype), x[0]], axis=0)
    cos_t, sin_t = _rope_tables()
    tri = (lax.broadcasted_iota(jnp.int32, (Q_TILE, Q_TILE), 0)
           >= lax.broadcasted_iota(jnp.int32, (Q_TILE, Q_TILE), 1)).astype(BF16)
    row2d = lambda v: v.reshape(1, -1).astype(F32)
    w_in_p, w_q_p = _pack_w_in(w_in), _pack_w_q_up(w_q_up)
    w_kv_b, w_o_b = w_kv_up.astype(BF16), w_out.astype(BF16)
    w_up_b, w_down_b = w_ffn_up.astype(BF16), w_ffn_down.astype(BF16)
    assert MLA_HEADS * MLA_V == FOX_W

    for l in range(DEPTH):
        proj = _proj_in(l, h, row2d(ln_mix_pre[l]), w_in_p)
        q, k, v, qf, kf, vf, c = _prep(
            l, proj, cos_t, sin_t, row2d(g_q_latent[l]), row2d(g_kv_latent[l]),
            row2d(g_fox_q[l]), row2d(g_fox_k[l]), _pad_cols(row2d(b_forget[l]), LANE),
            w_q_p, w_kv_b, tri)
        a = _mla_attn(q, k, v)
        b = _fox_attn(qf, kf, vf, c, proj)
        h = _mix_out(l, a, b, w_o_b, h, row2d(ln_mix_post[l]))
        h = _ffn(l, h, row2d(ln_ffn_pre[l]), w_up_b, w_ffn_conv[l].astype(F32),
                 row2d(b_ffn_conv[l]), w_down_b, row2d(ln_ffn_post[l]))

    return h[ROW_PAD + N_META:][None]
```

```python
import jax
import jax.numpy as jnp
import numpy as np
from jax import lax
from jax.experimental import pallas as pl
from jax.experimental.pallas import tpu as pltpu

F32 = jnp.float32
BF16 = jnp.bfloat16

D_MODEL = 2048
SEQ = 8192
DEPTH = 4
CHUNK = 64
CHUNK_SHIFT = 6
N_META = 16
MLA_HEADS = 8
MLA_Q_LORA = 512
MLA_KV_LORA = 512
MLA_NOPE = 128
MLA_ROPE = 64
MLA_V = 128
ROPE_THETA = 10000.0
FOX_HEADS = 8
FOX_HD = 128
FOX_W = FOX_HEADS * FOX_HD
D_FF = 5632
CONV_K = 3
EPS = 1e-6
NEG = -1e30

LANE = 128
Q_TILE = 256
ROW_PAD = Q_TILE - N_META
L_BUF = ROW_PAD + N_META + SEQ
ATT_TILE = 768
ROW_BLK = 256
N_AT = L_BUF // ATT_TILE

LOG2E = float(np.log2(np.e))
MLA_SCALE = (MLA_NOPE + MLA_ROPE) ** -0.5 * LOG2E
FOX_SCALE = FOX_HD ** -0.5 * LOG2E

TAIL_W = 4 * LANE
IN_MAIN = MLA_Q_LORA + MLA_KV_LORA + 3 * FOX_W + TAIL_W
IN_PACKED = IN_MAIN + FOX_W
GATE_COL0 = MLA_Q_LORA + MLA_KV_LORA + 2 * FOX_W
Q_HEAD_W = 3 * LANE
QK_W = 2 * LANE

TM_PROJ = 1408
TN_PROJ = 512
TM_PREP = 384
TM_OUT = 768
TM_FFN = 768
FC_FFN = 512
HALO = 16

VMEM_LIMIT = 56 * 1024 * 1024


def _rms(x, g):
    ms = jnp.mean(x * x, axis=-1, keepdims=True)
    return x * lax.rsqrt(ms + EPS) * g


def _split3(x):
    hi = x.astype(BF16).astype(F32)
    r = x - hi
    mid = r.astype(BF16).astype(F32)
    return hi, mid, r - mid


def _cparams(sem):
    return pltpu.CompilerParams(dimension_semantics=sem, vmem_limit_bytes=VMEM_LIMIT)


def _proj_in_kernel(x_ref, g_ref, w_ref, o_ref, ov_ref, xn_ref):
    j = pl.program_id(1)

    @pl.when(j == 0)
    def _():
        def body(r, carry):
            rows = pl.ds(pl.multiple_of(r * 64, 64), 64)
            xn_ref[rows, :] = _rms(x_ref[rows, :], g_ref[...]).astype(BF16)
            return carry
        lax.fori_loop(0, TM_PROJ // 64, body, 0)

    res = jnp.dot(xn_ref[...], w_ref[...], preferred_element_type=F32)

    @pl.when(j < IN_MAIN // TN_PROJ)
    def _():
        o_ref[...] = res

    @pl.when(j >= IN_MAIN // TN_PROJ)
    def _():
        ov_ref[...] = res.astype(BF16)


def _proj_in(l, h, g, w):
    n_main = IN_MAIN // TN_PROJ
    return pl.pallas_call(
        _proj_in_kernel,
        grid=(L_BUF // TM_PROJ, IN_PACKED // TN_PROJ),
        in_specs=[
            pl.BlockSpec((TM_PROJ, D_MODEL), lambda i, j: (i, 0)),
            pl.BlockSpec((1, D_MODEL), lambda i, j: (0, 0)),
            pl.BlockSpec((None, D_MODEL, TN_PROJ), lambda i, j: (l, 0, j)),
        ],
        out_specs=[
            pl.BlockSpec((TM_PROJ, TN_PROJ), lambda i, j: (i, jnp.minimum(j, n_main - 1))),
            pl.BlockSpec((None, TM_PROJ, TN_PROJ), lambda i, j: (0, i, jnp.maximum(j - n_main, 0))),
        ],
        out_shape=[jax.ShapeDtypeStruct((L_BUF, IN_MAIN), F32),
                   jax.ShapeDtypeStruct((1, L_BUF, FOX_W), BF16)],
        scratch_shapes=[pltpu.VMEM((TM_PROJ, D_MODEL), BF16)],
        compiler_params=_cparams(("parallel", "arbitrary")),
        name="proj_in",
    )(h, g, w)


def _prep_kernel(cq_ref, ckv_ref, fq_ref, fk_ref, tail_ref, cos_ref, sin_ref,
                 gql_ref, gkvl_ref, gfq_ref, gfk_ref, bf_ref, wq_ref, wkv_ref, tri_ref,
                 q_ref, k_ref, v_ref, qf_ref, kf_ref, c_ref, carry_ref):
    i = pl.program_id(0)
    lane = lax.broadcasted_iota(jnp.int32, (TM_PREP, LANE), 1)
    is_pad = (i * TM_PREP + lax.broadcasted_iota(jnp.int32, (TM_PREP, 1), 0)) < ROW_PAD
    cos = cos_ref[...]
    sin = sin_ref[...]
    q_flag = jnp.where(lane == MLA_ROPE, 1.0, 0.0)
    k_flag = jnp.where((lane == MLA_ROPE) & is_pad, NEG, 0.0)
    kr = (tail_ref[:, 0:LANE] * cos + tail_ref[:, LANE:2 * LANE] * sin + k_flag).astype(BF16)
    cqn = _rms(cq_ref[...], gql_ref[...]).astype(BF16)
    ckvn = _rms(ckv_ref[...], gkvl_ref[...]).astype(BF16)
    for h in range(MLA_HEADS):
        qh = jnp.dot(cqn, wq_ref[:, Q_HEAD_W * h:Q_HEAD_W * (h + 1)], preferred_element_type=F32)
        qr = qh[:, LANE:2 * LANE] * cos + qh[:, 2 * LANE:3 * LANE] * sin
        q_ref[h, :, 0:LANE] = (qh[:, 0:LANE] * MLA_SCALE).astype(BF16)
        q_ref[h, :, LANE:QK_W] = (qr * MLA_SCALE + q_flag).astype(BF16)
        kvh = jnp.dot(ckvn, wkv_ref[:, 2 * LANE * h:2 * LANE * (h + 1)], preferred_element_type=F32)
        k_ref[h, :, 0:LANE] = kvh[:, 0:LANE].astype(BF16)
        k_ref[h, :, LANE:QK_W] = kr
        v_ref[h] = kvh[:, LANE:2 * LANE].astype(BF16)

    z = tail_ref[:, 2 * LANE:3 * LANE] + bf_ref[...]
    logf = (jnp.minimum(z, 0.0) - jnp.log(1.0 + jnp.exp(-jnp.abs(z)))) * LOG2E
    hi, mid, lo = _split3(logf)
    tri = tri_ref[...]
    cs = (jnp.dot(tri, hi.astype(BF16), preferred_element_type=F32)
          + jnp.dot(tri, mid.astype(BF16), preferred_element_type=F32)
          + jnp.dot(tri, lo.astype(BF16), preferred_element_type=F32))

    @pl.when(i == 0)
    def _():
        carry_ref[...] = jnp.zeros_like(carry_ref)

    c = cs + carry_ref[0:1, :]
    c_ref[...] = c
    carry_ref[...] = jnp.broadcast_to(c[TM_PREP - 1:TM_PREP, :], carry_ref.shape)

    qf_ext = jnp.where(lane < 3, -1.0, 0.0).astype(BF16)
    for h in range(FOX_HEADS):
        cols = slice(FOX_HD * h, FOX_HD * (h + 1))
        qf_ref[h, :, 0:FOX_HD] = (_rms(fq_ref[:, cols], gfq_ref[...]) * FOX_SCALE).astype(BF16)
        qf_ref[h, :, FOX_HD:QK_W] = qf_ext
        kf_ref[h, :, 0:FOX_HD] = _rms(fk_ref[:, cols], gfk_ref[...]).astype(BF16)
        ck = jnp.sum(jnp.where(lane == h, c, 0.0), axis=-1, keepdims=True)
        ck_hi, ck_mid, ck_lo = _split3(jnp.where(is_pad, -NEG, ck))
        ext = jnp.where(lane == 0, ck_hi, jnp.where(lane == 1, ck_mid, jnp.where(lane == 2, ck_lo, 0.0)))
        kf_ref[h, :, FOX_HD:QK_W] = ext.astype(BF16)


def _prep(l, proj, cos_t, sin_t, gql, gkvl, gfq, gfk, bf_pad, wq, wkv, tri):
    tm = TM_PREP
    row = lambda i: (i, 0)
    const = lambda i: (0, 0)
    head_out = lambda w: pl.BlockSpec((MLA_HEADS, tm, w), lambda i: (0, i, 0))
    return pl.pallas_call(
        _prep_kernel,
        grid=(L_BUF // tm,),
        in_specs=[
            pl.BlockSpec((tm, MLA_Q_LORA), lambda i: (i, 0)),
            pl.BlockSpec((tm, MLA_KV_LORA), lambda i: (i, 1)),
            pl.BlockSpec((tm, FOX_W), lambda i: (i, 1)),
            pl.BlockSpec((tm, FOX_W), lambda i: (i, 2)),
            pl.BlockSpec((tm, TAIL_W), lambda i: (i, IN_MAIN // TAIL_W - 1)),
            pl.BlockSpec((tm, LANE), row),
            pl.BlockSpec((tm, LANE), row),
            pl.BlockSpec((1, MLA_Q_LORA), const),
            pl.BlockSpec((1, MLA_KV_LORA), const),
            pl.BlockSpec((1, FOX_HD), const),
            pl.BlockSpec((1, FOX_HD), const),
            pl.BlockSpec((1, LANE), const),
            pl.BlockSpec((None, MLA_Q_LORA, MLA_HEADS * Q_HEAD_W), lambda i: (l, 0, 0)),
            pl.BlockSpec((None, MLA_KV_LORA, MLA_HEADS * 2 * LANE), lambda i: (l, 0, 0)),
            pl.BlockSpec((tm, tm), const),
        ],
        out_specs=[
            head_out(QK_W), head_out(QK_W), head_out(MLA_V),
            head_out(QK_W), head_out(QK_W),
            pl.BlockSpec((tm, LANE), row),
        ],
        out_shape=[
            jax.ShapeDtypeStruct((MLA_HEADS, L_BUF, QK_W), BF16),
            jax.ShapeDtypeStruct((MLA_HEADS, L_BUF, QK_W), BF16),
            jax.ShapeDtypeStruct((MLA_HEADS, L_BUF, MLA_V), BF16),
            jax.ShapeDtypeStruct((FOX_HEADS, L_BUF, QK_W), BF16),
            jax.ShapeDtypeStruct((FOX_HEADS, L_BUF, QK_W), BF16),
            jax.ShapeDtypeStruct((L_BUF, LANE), F32),
        ],
        scratch_shapes=[pltpu.VMEM((8, LANE), F32)],
        compiler_params=_cparams(("arbitrary",)),
        name="attn_prep",
    )(proj, proj, proj, proj, proj, cos_t, sin_t, gql, gkvl, gfq, gfk, bf_pad, wq, wkv, tri)


def _scores(q, k):
    return lax.dot_general(q, k, (((1,), (1,)), ((), ())), preferred_element_type=F32)


def _flash_tile(i, q_ref, k_ref, v_ref, m_sc, acc_sc, p_sc, alpha_sc, p_new, alpha_new,
                diag_mask, row_shift=None):
    m_sc[...] = jnp.full_like(m_sc, NEG)
    acc_sc[...] = jnp.zeros_like(acc_sc)
    row_blocks = [slice(rb * ROW_BLK, (rb + 1) * ROW_BLK) for rb in range(ATT_TILE // ROW_BLK)]

    def chunk_start(j):
        return pl.multiple_of(j * ATT_TILE, ATT_TILE)

    def flush(j):
        v = v_ref[0, pl.ds(chunk_start(j), ATT_TILE), :]
        v_ones = jnp.concatenate([v, jnp.ones_like(v)], axis=1)
        for rows in row_blocks:
            pv = acc_sc[rows, :] * jnp.tile(alpha_sc[rows, :], (1, 2))
            for kc in range(0, ATT_TILE, 2 * LANE):
                pv = pv + jnp.dot(p_sc[rows, kc:kc + 2 * LANE], v_ones[kc:kc + 2 * LANE, :],
                                  preferred_element_type=F32)
            acc_sc[rows, :] = pv

    def scores(j, rows, ncols):
        return _scores(q_ref[0, rows, :], k_ref[0, pl.ds(chunk_start(j), ncols), :])

    def softmax_rows(rows, s, p_dst, alpha_dst):
        ncols = s.shape[1]
        m_prev = m_sc[rows, :]
        m_chunk = jnp.max(s, axis=-1, keepdims=True)
        if row_shift is not None:
            m_chunk = m_chunk + row_shift[rows, :]
        m_new = jnp.maximum(m_prev, m_chunk)
        alpha_dst[rows, :] = jnp.exp2(m_prev - m_new)
        sub = m_new if row_shift is None else m_new - row_shift[rows, :]
        p_dst[rows, 0:ncols] = jnp.exp2(s - jnp.tile(sub, (1, ncols // LANE))).astype(BF16)
        m_sc[rows, :] = m_new

    def make_pending():
        p_sc[...] = p_new[...]
        alpha_sc[...] = alpha_new[...]

    for rows in row_blocks:
        ncols = rows.stop
        s = scores(i, rows, ncols)
        r = lax.broadcasted_iota(jnp.int32, s.shape, 0) + rows.start
        c = lax.broadcasted_iota(jnp.int32, s.shape, 1)
        if ncols < ATT_TILE:
            p_sc[rows, ncols:] = jnp.zeros((ROW_BLK, ATT_TILE - ncols), BF16)
        softmax_rows(rows, jnp.where(diag_mask(r, c), s, NEG), p_sc, alpha_sc)

    def body(j, pending):
        for rows in row_blocks:
            softmax_rows(rows, scores(j, rows, ATT_TILE), p_new, alpha_new)
        flush(pending)
        make_pending()
        return j
    flush(lax.fori_loop(0, i, body, i))

    return acc_sc[:, 0:LANE] / acc_sc[:, LANE:2 * LANE]


_ATT_SCRATCH = [pltpu.VMEM((ATT_TILE, LANE), F32), pltpu.VMEM((ATT_TILE, 2 * LANE), F32),
                pltpu.VMEM((ATT_TILE, ATT_TILE), BF16), pltpu.VMEM((ATT_TILE, LANE), F32),
                pltpu.VMEM((ATT_TILE, ATT_TILE), BF16), pltpu.VMEM((ATT_TILE, LANE), F32)]


def _mla_attn_kernel(q_ref, k_ref, v_ref, o_ref, *scratch):
    mask = lambda r, c: jnp.right_shift(c, CHUNK_SHIFT) <= jnp.right_shift(r, CHUNK_SHIFT)
    out = _flash_tile(pl.program_id(1), q_ref, k_ref, v_ref, *scratch, diag_mask=mask)
    o_ref[...] = out.astype(o_ref.dtype)


def _mla_attn(q, k, v):
    return pl.pallas_call(
        _mla_attn_kernel,
        grid=(MLA_HEADS, N_AT),
        in_specs=[
            pl.BlockSpec((1, ATT_TILE, QK_W), lambda h, i: (h, i, 0)),
            pl.BlockSpec((1, L_BUF, QK_W), lambda h, i: (h, 0, 0)),
            pl.BlockSpec((1, L_BUF, MLA_V), lambda h, i: (h, 0, 0)),
        ],
        out_specs=pl.BlockSpec((ATT_TILE, MLA_V), lambda h, i: (i, h)),
        out_shape=jax.ShapeDtypeStruct((L_BUF, MLA_HEADS * MLA_V), BF16),
        scratch_shapes=_ATT_SCRATCH,
        compiler_params=_cparams(("parallel", "arbitrary")),
        name="mla_attn",
    )(q, k, v)


def _fox_attn_kernel(q_ref, k_ref, v_ref, c_ref, gate_ref, o_ref, *scratch):
    lane = lax.broadcasted_iota(jnp.int32, (ATT_TILE, LANE), 1)
    cq = jnp.sum(jnp.where(lane == pl.program_id(0), c_ref[...], 0.0), axis=-1, keepdims=True)
    out = _flash_tile(pl.program_id(1), q_ref, k_ref, v_ref, *scratch,
                      diag_mask=lambda r, c: c <= r, row_shift=cq)
    o_ref[...] = (out * jax.nn.sigmoid(gate_ref[...])).astype(o_ref.dtype)


def _fox_attn(q, k, v, c, proj):
    return pl.pallas_call(
        _fox_attn_kernel,
        grid=(FOX_HEADS, N_AT),
        in_specs=[
            pl.BlockSpec((1, ATT_TILE, QK_W), lambda h, i: (h, i, 0)),
            pl.BlockSpec((1, L_BUF, QK_W), lambda h, i: (h, 0, 0)),
            pl.BlockSpec((1, L_BUF, FOX_HD), lambda h, i: (0, 0, h)),
            pl.BlockSpec((ATT_TILE, LANE), lambda h, i: (i, 0)),
            pl.BlockSpec((ATT_TILE, FOX_HD), lambda h, i: (i, GATE_COL0 // FOX_HD + h)),
        ],
        out_specs=pl.BlockSpec((ATT_TILE, FOX_HD), lambda h, i: (i, h)),
        out_shape=jax.ShapeDtypeStruct((L_BUF, FOX_W), BF16),
        scratch_shapes=_ATT_SCRATCH,
        compiler_params=_cparams(("parallel", "arbitrary")),
        name="fox_attn",
    )(q, k, v, c, proj)


def _mix_out_kernel(a_ref, b_ref, wa_ref, wb_ref, h_ref, g_ref, o_ref):
    mix = (jnp.dot(a_ref[...], wa_ref[...], preferred_element_type=F32)
           + jnp.dot(b_ref[...], wb_ref[...], preferred_element_type=F32))
    out = h_ref[...] + _rms(mix, g_ref[...])
    row = pl.program_id(0) * TM_OUT + lax.broadcasted_iota(jnp.int32, (TM_OUT, 1), 0)
    o_ref[...] = jnp.where(row >= ROW_PAD, out, 0.0)


def _mix_out(l, a, b, w_o, h, g):
    return pl.pallas_call(
        _mix_out_kernel,
        grid=(L_BUF // TM_OUT,),
        in_specs=[
            pl.BlockSpec((TM_OUT, MLA_HEADS * MLA_V), lambda i: (i, 0)),
            pl.BlockSpec((TM_OUT, FOX_W), lambda i: (i, 0)),
            pl.BlockSpec((None, MLA_HEADS * MLA_V, D_MODEL), lambda i: (l, 0, 0)),
            pl.BlockSpec((None, FOX_W, D_MODEL), lambda i: (l, 1, 0)),
            pl.BlockSpec((TM_OUT, D_MODEL), lambda i: (i, 0)),
            pl.BlockSpec((1, D_MODEL), lambda i: (0, 0)),
        ],
        out_specs=pl.BlockSpec((TM_OUT, D_MODEL), lambda i: (i, 0)),
        out_shape=jax.ShapeDtypeStruct((L_BUF, D_MODEL), F32),
        compiler_params=_cparams(("parallel",)),
        name="mix_out",
    )(a, b, w_o, w_o, h, g)


def _gelu_tanh(x):
    return 0.5 * x * (1.0 + jnp.tanh(np.sqrt(2.0 / np.pi).astype(np.float32) * (x + 0.044715 * (x * x * x))))


def _ffn_kernel(h_ref, halo_ref, gpre_ref, wg_ref, wu_ref, cwg_ref, cwu_ref, cbg_ref, cbu_ref,
                wd_ref, gpost_ref, o_ref, xn_ref, ug_ref, uu_ref):
    c = pl.program_id(1)

    @pl.when(c == 0)
    def _():
        xn_ref[0:HALO, :] = _rms(halo_ref[...], gpre_ref[...]).astype(BF16)

        def body(r, carry):
            src = pl.ds(pl.multiple_of(r * 64, 64), 64)
            dst = pl.ds(pl.multiple_of(HALO + r * 64, 16), 64)
            xn_ref[dst, :] = _rms(h_ref[src, :], gpre_ref[...]).astype(BF16)
            return carry
        lax.fori_loop(0, TM_FFN // 64, body, 0)
        o_ref[...] = jnp.zeros_like(o_ref)

    xn = xn_ref[...]
    ug_ref[...] = jnp.dot(xn, wg_ref[...], preferred_element_type=F32)
    uu_ref[...] = jnp.dot(xn, wu_ref[...], preferred_element_type=F32)

    def conv(u_ref, w_ref, b_ref):
        acc = b_ref[...] + w_ref[CONV_K - 1:CONV_K, :] * u_ref[HALO:HALO + TM_FFN, :]
        for t in range(1, CONV_K):
            acc = acc + w_ref[CONV_K - 1 - t:CONV_K - t, :] * u_ref[HALO - t:HALO - t + TM_FFN, :]
        return acc

    act = _gelu_tanh(conv(ug_ref, cwg_ref, cbg_ref)) * conv(uu_ref, cwu_ref, cbu_ref)
    o_ref[...] += jnp.dot(act.astype(BF16), wd_ref[...], preferred_element_type=F32)

    @pl.when(c == pl.num_programs(1) - 1)
    def _():
        out = h_ref[...] + _rms(o_ref[...], gpost_ref[...])
        row = pl.program_id(0) * TM_FFN + lax.broadcasted_iota(jnp.int32, (TM_FFN, 1), 0)
        o_ref[...] = jnp.where(row >= ROW_PAD, out, 0.0)


def _ffn(l, h, gpre, w_up, w_conv, b_conv, w_down, gpost):
    n_fc = D_FF // FC_FFN
    halo_blocks = TM_FFN // HALO
    return pl.pallas_call(
        _ffn_kernel,
        grid=(L_BUF // TM_FFN, n_fc),
        in_specs=[
            pl.BlockSpec((TM_FFN, D_MODEL), lambda i, c: (i, 0)),
            pl.BlockSpec((HALO, D_MODEL), lambda i, c: (jnp.maximum(i * halo_blocks - 1, 0), 0)),
            pl.BlockSpec((1, D_MODEL), lambda i, c: (0, 0)),
            pl.BlockSpec((None, D_MODEL, FC_FFN), lambda i, c: (l, 0, c)),
            pl.BlockSpec((None, D_MODEL, FC_FFN), lambda i, c: (l, 0, n_fc + c)),
            pl.BlockSpec((CONV_K, FC_FFN), lambda i, c: (0, c)),
            pl.BlockSpec((CONV_K, FC_FFN), lambda i, c: (0, n_fc + c)),
            pl.BlockSpec((1, FC_FFN), lambda i, c: (0, c)),
            pl.BlockSpec((1, FC_FFN), lambda i, c: (0, n_fc + c)),
            pl.BlockSpec((None, FC_FFN, D_MODEL), lambda i, c: (l, c, 0)),
            pl.BlockSpec((1, D_MODEL), lambda i, c: (0, 0)),
        ],
        out_specs=pl.BlockSpec((TM_FFN, D_MODEL), lambda i, c: (i, 0)),
        out_shape=jax.ShapeDtypeStruct((L_BUF, D_MODEL), F32),
        scratch_shapes=[pltpu.VMEM((HALO + TM_FFN, D_MODEL), BF16),
                        pltpu.VMEM((HALO + TM_FFN, FC_FFN), F32),
                        pltpu.VMEM((HALO + TM_FFN, FC_FFN), F32)],
        compiler_params=_cparams(("parallel", "arbitrary")),
        name="conv_ffn",
    )(h, h, gpre, w_up, w_up, w_conv, w_conv, b_conv, b_conv, w_down, gpost)


def _rotate_half_cols(w):
    half = w.shape[-1] // 2
    return jnp.concatenate([-w[..., half:], w[..., :half]], axis=-1)


def _pad_cols(w, width):
    return jnp.pad(w, [(0, 0)] * (w.ndim - 1) + [(0, width - w.shape[-1])])


def _pack_w_in(w):
    o = np.cumsum([0, MLA_Q_LORA, MLA_KV_LORA, MLA_ROPE, FOX_W, FOX_W, FOX_W, FOX_W, FOX_HEADS])
    c_q, c_kv, k_rope, fq, fk, fv, fg, ff = [w[..., o[n]:o[n + 1]].astype(BF16) for n in range(8)]
    tail = jnp.concatenate([_pad_cols(k_rope, LANE), _pad_cols(_rotate_half_cols(k_rope), LANE),
                            _pad_cols(ff, 2 * LANE)], axis=-1)
    return jnp.concatenate([c_q, c_kv, fq, fk, fg, tail, fv], axis=-1)


def _pack_w_q_up(w):
    w = w.astype(BF16).reshape(DEPTH, MLA_Q_LORA, MLA_HEADS, MLA_NOPE + MLA_ROPE)
    nope, rope = w[..., :MLA_NOPE], w[..., MLA_NOPE:]
    packed = jnp.concatenate([nope, _pad_cols(rope, LANE), _pad_cols(_rotate_half_cols(rope), LANE)], axis=-1)
    return packed.reshape(DEPTH, MLA_Q_LORA, MLA_HEADS * Q_HEAD_W)


def _rope_tables():
    pos = jnp.maximum(jnp.arange(L_BUF, dtype=jnp.int32) - ROW_PAD, 0).astype(F32)
    half = MLA_ROPE // 2
    inv_freq = ROPE_THETA ** (-jnp.arange(half, dtype=F32) / half)
    ang = pos[:, None] * inv_freq[None, :]
    zeros = jnp.zeros((L_BUF, LANE - MLA_ROPE), F32)
    cos, sin = jnp.cos(ang), jnp.sin(ang)
    return (jnp.concatenate([cos, cos, zeros], axis=1), jnp.concatenate([sin, sin, zeros], axis=1))


def kernel(x, meta_tokens, ln_mix_pre, w_in, b_forget, g_q_latent, g_kv_latent, w_q_up, w_kv_up,
           g_fox_q, g_fox_k, w_out, ln_mix_post, ln_ffn_pre, w_ffn_up, w_ffn_conv, b_ffn_conv,
           w_ffn_down, ln_ffn_post):
    assert x.shape == (1, SEQ, D_MODEL), x.shape
    h = jnp.concatenate([jnp.zeros((ROW_PAD, D_MODEL), x.dtype), meta_tokens.astype(x.dtype), x[0]], axis=0)
    cos_t, sin_t = _rope_tables()
    tri = (lax.broadcasted_iota(jnp.int32, (TM_PREP, TM_PREP), 0)
           >= lax.broadcasted_iota(jnp.int32, (TM_PREP, TM_PREP), 1)).astype(BF16)
    row2d = lambda v: v.reshape(1, -1).astype(F32)
    w_in_p, w_q_p = _pack_w_in(w_in), _pack_w_q_up(w_q_up)
    w_kv_b, w_o_b = w_kv_up.astype(BF16), w_out.astype(BF16)
    w_up_b, w_down_b = w_ffn_up.astype(BF16), w_ffn_down.astype(BF16)
    assert MLA_HEADS * MLA_V == FOX_W

    for l in range(DEPTH):
        proj, vf = _proj_in(l, h, row2d(ln_mix_pre[l]), w_in_p)
        q, k, v, qf, kf, c = _prep(
            l, proj, cos_t, sin_t, row2d(g_q_latent[l]), row2d(g_kv_latent[l]),
            row2d(g_fox_q[l]), row2d(g_fox_k[l]), _pad_cols(row2d(b_forget[l]), LANE),
            w_q_p, w_kv_b, tri)
        a = _mla_attn(q, k, v)
        b = _fox_attn(qf, kf, vf, c, proj)
        h = _mix_out(l, a, b, w_o_b, h, row2d(ln_mix_post[l]))
        h = _ffn(l, h, row2d(ln_ffn_pre[l]), w_up_b, w_ffn_conv[l].astype(F32),
                 row2d(b_ffn_conv[l]), w_down_b, row2d(ln_ffn_post[l]))

    return h[ROW_PAD + N_META:][None]
```

```python
import jax
import jax.numpy as jnp
import numpy as np
from jax import lax
from jax.experimental import pallas as pl
from jax.experimental.pallas import tpu as pltpu

F32 = jnp.float32
BF16 = jnp.bfloat16

D_MODEL = 2048
SEQ = 8192
DEPTH = 4
CHUNK = 64
CHUNK_SHIFT = 6
N_META = 16
MLA_HEADS = 8
MLA_Q_LORA = 512
MLA_KV_LORA = 512
MLA_NOPE = 128
MLA_ROPE = 64
MLA_V = 128
ROPE_THETA = 10000.0
FOX_HEADS = 8
FOX_HD = 128
FOX_W = FOX_HEADS * FOX_HD
D_FF = 5632
CONV_K = 3
EPS = 1e-6
NEG = -1e30

LANE = 128
Q_TILE = 256
ROW_PAD = Q_TILE - N_META
L_BUF = ROW_PAD + N_META + SEQ
ATT_TILE = 768
ROW_BLK = 256
N_AT = L_BUF // ATT_TILE

LOG2E = float(np.log2(np.e))
MLA_SCALE = (MLA_NOPE + MLA_ROPE) ** -0.5 * LOG2E
FOX_SCALE = FOX_HD ** -0.5 * LOG2E

TAIL_W = 4 * LANE
IN_MAIN = MLA_Q_LORA + MLA_KV_LORA + 3 * FOX_W + TAIL_W
IN_PACKED = IN_MAIN + FOX_W
GATE_COL0 = MLA_Q_LORA + MLA_KV_LORA + 2 * FOX_W
Q_HEAD_W = 3 * LANE
QK_W = 2 * LANE

TM_PROJ = 1408
TN_PROJ = 512
TM_OUT = 384
TM_FFN = 768
FC_FFN = 512
HALO = 16
CVT_STEPS = 88

VMEM_LIMIT = 56 * 1024 * 1024
VMEM_LIMIT_FFN = 58 * 1024 * 1024


def _rms(x, g):
    ms = jnp.mean(x * x, axis=-1, keepdims=True)
    return x * lax.rsqrt(ms + EPS) * g


def _split3(x):
    hi = x.astype(BF16).astype(F32)
    r = x - hi
    mid = r.astype(BF16).astype(F32)
    return hi, mid, r - mid


def _cparams(sem, vmem_limit=VMEM_LIMIT):
    return pltpu.CompilerParams(dimension_semantics=sem, vmem_limit_bytes=vmem_limit)


def _proj_in_kernel(x_ref, g_ref, w_ref, o_ref, ov_ref, xn_ref):
    j = pl.program_id(1)

    @pl.when(j == 0)
    def _():
        def body(r, carry):
            rows = pl.ds(pl.multiple_of(r * 64, 64), 64)
            xn_ref[rows, :] = _rms(x_ref[rows, :], g_ref[...]).astype(BF16)
            return carry
        lax.fori_loop(0, TM_PROJ // 64, body, 0)

    res = jnp.dot(xn_ref[...], w_ref[...], preferred_element_type=F32)

    @pl.when(j < IN_MAIN // TN_PROJ)
    def _():
        o_ref[...] = res

    @pl.when(j >= IN_MAIN // TN_PROJ)
    def _():
        ov_ref[...] = res.astype(BF16)


def _proj_in(l, h, g, w):
    n_main = IN_MAIN // TN_PROJ
    return pl.pallas_call(
        _proj_in_kernel,
        grid=(L_BUF // TM_PROJ, IN_PACKED // TN_PROJ),
        in_specs=[
            pl.BlockSpec((TM_PROJ, D_MODEL), lambda i, j: (i, 0)),
            pl.BlockSpec((1, D_MODEL), lambda i, j: (0, 0)),
            pl.BlockSpec((None, D_MODEL, TN_PROJ), lambda i, j: (l, 0, j)),
        ],
        out_specs=[
            pl.BlockSpec((TM_PROJ, TN_PROJ), lambda i, j: (i, jnp.minimum(j, n_main - 1))),
            pl.BlockSpec((None, TM_PROJ, TN_PROJ), lambda i, j: (0, i, jnp.maximum(j - n_main, 0))),
        ],
        out_shape=[jax.ShapeDtypeStruct((L_BUF, IN_MAIN), F32),
                   jax.ShapeDtypeStruct((1, L_BUF, FOX_W), BF16)],
        scratch_shapes=[pltpu.VMEM((TM_PROJ, D_MODEL), BF16)],
        compiler_params=_cparams(("parallel", "arbitrary")),
        name="proj_in",
    )(h, g, w)


def _prep_kernel(cq_ref, ckv_ref, fq_ref, fk_ref, tail_ref, cos_ref, sin_ref,
                 gql_ref, gkvl_ref, gfq_ref, gfk_ref, bf_ref, wq_ref, wkv_ref, tri_ref,
                 q_ref, k_ref, v_ref, qf_ref, kf_ref, c_ref, carry_ref):
    i = pl.program_id(0)
    lane = lax.broadcasted_iota(jnp.int32, (Q_TILE, LANE), 1)
    is_pad = (i * Q_TILE + lax.broadcasted_iota(jnp.int32, (Q_TILE, 1), 0)) < ROW_PAD
    cos = cos_ref[...]
    sin = sin_ref[...]
    q_flag = jnp.where(lane == MLA_ROPE, 1.0, 0.0)
    k_flag = jnp.where((lane == MLA_ROPE) & is_pad, NEG, 0.0)
    kr = (tail_ref[:, 0:LANE] * cos + tail_ref[:, LANE:2 * LANE] * sin + k_flag).astype(BF16)
    cqn = _rms(cq_ref[...], gql_ref[...]).astype(BF16)
    ckvn = _rms(ckv_ref[...], gkvl_ref[...]).astype(BF16)
    for h in range(MLA_HEADS):
        qh = jnp.dot(cqn, wq_ref[:, Q_HEAD_W * h:Q_HEAD_W * (h + 1)], preferred_element_type=F32)
        qr = qh[:, LANE:2 * LANE] * cos + qh[:, 2 * LANE:3 * LANE] * sin
        q_ref[h, :, 0:LANE] = (qh[:, 0:LANE] * MLA_SCALE).astype(BF16)
        q_ref[h, :, LANE:QK_W] = (qr * MLA_SCALE + q_flag).astype(BF16)
        kvh = jnp.dot(ckvn, wkv_ref[:, 2 * LANE * h:2 * LANE * (h + 1)], preferred_element_type=F32)
        k_ref[h, :, 0:LANE] = kvh[:, 0:LANE].astype(BF16)
        k_ref[h, :, LANE:QK_W] = kr
        v_ref[h] = kvh[:, LANE:2 * LANE].astype(BF16)

    z = tail_ref[:, 2 * LANE:3 * LANE] + bf_ref[...]
    logf = (jnp.minimum(z, 0.0) - jnp.log(1.0 + jnp.exp(-jnp.abs(z)))) * LOG2E
    hi, mid, lo = _split3(logf)
    tri = tri_ref[...]
    cs = (jnp.dot(tri, hi.astype(BF16), preferred_element_type=F32)
          + jnp.dot(tri, mid.astype(BF16), preferred_element_type=F32)
          + jnp.dot(tri, lo.astype(BF16), preferred_element_type=F32))

    @pl.when(i == 0)
    def _():
        carry_ref[...] = jnp.zeros_like(carry_ref)

    c = cs + carry_ref[0:1, :]
    c_ref[...] = c
    carry_ref[...] = jnp.broadcast_to(c[Q_TILE - 1:Q_TILE, :], carry_ref.shape)

    qf_ext = jnp.where(lane < 3, -1.0, 0.0).astype(BF16)
    for h in range(FOX_HEADS):
        cols = slice(FOX_HD * h, FOX_HD * (h + 1))
        qf_ref[h, :, 0:FOX_HD] = (_rms(fq_ref[:, cols], gfq_ref[...]) * FOX_SCALE).astype(BF16)
        qf_ref[h, :, FOX_HD:QK_W] = qf_ext
        kf_ref[h, :, 0:FOX_HD] = _rms(fk_ref[:, cols], gfk_ref[...]).astype(BF16)
        ck = jnp.sum(jnp.where(lane == h, c, 0.0), axis=-1, keepdims=True)
        ck_hi, ck_mid, ck_lo = _split3(jnp.where(is_pad, -NEG, ck))
        ext = jnp.where(lane == 0, ck_hi, jnp.where(lane == 1, ck_mid, jnp.where(lane == 2, ck_lo, 0.0)))
        kf_ref[h, :, FOX_HD:QK_W] = ext.astype(BF16)


def _prep(l, proj, cos_t, sin_t, gql, gkvl, gfq, gfk, bf_pad, wq, wkv, tri):
    tm = Q_TILE
    row = lambda i: (i, 0)
    const = lambda i: (0, 0)
    head_out = lambda w: pl.BlockSpec((MLA_HEADS, tm, w), lambda i: (0, i, 0))
    return pl.pallas_call(
        _prep_kernel,
        grid=(L_BUF // tm,),
        in_specs=[
            pl.BlockSpec((tm, MLA_Q_LORA), lambda i: (i, 0)),
            pl.BlockSpec((tm, MLA_KV_LORA), lambda i: (i, 1)),
            pl.BlockSpec((tm, FOX_W), lambda i: (i, 1)),
            pl.BlockSpec((tm, FOX_W), lambda i: (i, 2)),
            pl.BlockSpec((tm, TAIL_W), lambda i: (i, IN_MAIN // TAIL_W - 1)),
            pl.BlockSpec((tm, LANE), row),
            pl.BlockSpec((tm, LANE), row),
            pl.BlockSpec((1, MLA_Q_LORA), const),
            pl.BlockSpec((1, MLA_KV_LORA), const),
            pl.BlockSpec((1, FOX_HD), const),
            pl.BlockSpec((1, FOX_HD), const),
            pl.BlockSpec((1, LANE), const),
            pl.BlockSpec((None, MLA_Q_LORA, MLA_HEADS * Q_HEAD_W), lambda i: (l, 0, 0)),
            pl.BlockSpec((None, MLA_KV_LORA, MLA_HEADS * 2 * LANE), lambda i: (l, 0, 0)),
            pl.BlockSpec((tm, tm), const),
        ],
        out_specs=[
            head_out(QK_W), head_out(QK_W), head_out(MLA_V),
            head_out(QK_W), head_out(QK_W),
            pl.BlockSpec((tm, LANE), row),
        ],
        out_shape=[
            jax.ShapeDtypeStruct((MLA_HEADS, L_BUF, QK_W), BF16),
            jax.ShapeDtypeStruct((MLA_HEADS, L_BUF, QK_W), BF16),
            jax.ShapeDtypeStruct((MLA_HEADS, L_BUF, MLA_V), BF16),
            jax.ShapeDtypeStruct((FOX_HEADS, L_BUF, QK_W), BF16),
            jax.ShapeDtypeStruct((FOX_HEADS, L_BUF, QK_W), BF16),
            jax.ShapeDtypeStruct((L_BUF, LANE), F32),
        ],
        scratch_shapes=[pltpu.VMEM((8, LANE), F32)],
        compiler_params=_cparams(("arbitrary",)),
        name="attn_prep",
    )(proj, proj, proj, proj, proj, cos_t, sin_t, gql, gkvl, gfq, gfk, bf_pad, wq, wkv, tri)


def _scores(q, k):
    return lax.dot_general(q, k, (((1,), (1,)), ((), ())), preferred_element_type=F32)


def _flash_tile(i, q_ref, k_ref, v_ref, m_sc, acc_sc, p_sc, alpha_sc, p_new, alpha_new,
                diag_mask, row_shift=None):
    m_sc[...] = jnp.full_like(m_sc, NEG)
    acc_sc[...] = jnp.zeros_like(acc_sc)
    row_blocks = [slice(rb * ROW_BLK, (rb + 1) * ROW_BLK) for rb in range(ATT_TILE // ROW_BLK)]

    def chunk_start(j):
        return pl.multiple_of(j * ATT_TILE, ATT_TILE)

    def flush(j):
        v = v_ref[0, pl.ds(chunk_start(j), ATT_TILE), :]
        v_ones = jnp.concatenate([v, jnp.ones_like(v)], axis=1)
        for rows in row_blocks:
            pv = acc_sc[rows, :] * jnp.tile(alpha_sc[rows, :], (1, 2))
            for kc in range(0, ATT_TILE, 2 * LANE):
                pv = pv + jnp.dot(p_sc[rows, kc:kc + 2 * LANE], v_ones[kc:kc + 2 * LANE, :],
                                  preferred_element_type=F32)
            acc_sc[rows, :] = pv

    def scores(j, rows, ncols):
        return _scores(q_ref[0, rows, :], k_ref[0, pl.ds(chunk_start(j), ncols), :])

    def softmax_rows(rows, s, p_dst, alpha_dst):
        ncols = s.shape[1]
        m_prev = m_sc[rows, :]
        m_chunk = jnp.max(s, axis=-1, keepdims=True)
        if row_shift is not None:
            m_chunk = m_chunk + row_shift[rows, :]
        m_new = jnp.maximum(m_prev, m_chunk)
        alpha_dst[rows, :] = jnp.exp2(m_prev - m_new)
        sub = m_new if row_shift is None else m_new - row_shift[rows, :]
        p_dst[rows, 0:ncols] = jnp.exp2(s - jnp.tile(sub, (1, ncols // LANE))).astype(BF16)
        m_sc[rows, :] = m_new

    def make_pending():
        p_sc[...] = p_new[...]
        alpha_sc[...] = alpha_new[...]

    for rows in row_blocks:
        ncols = rows.stop
        s = scores(i, rows, ncols)
        r = lax.broadcasted_iota(jnp.int32, s.shape, 0) + rows.start
        c = lax.broadcasted_iota(jnp.int32, s.shape, 1)
        if ncols < ATT_TILE:
            p_sc[rows, ncols:] = jnp.zeros((ROW_BLK, ATT_TILE - ncols), BF16)
        softmax_rows(rows, jnp.where(diag_mask(r, c), s, NEG), p_sc, alpha_sc)

    def body(j, pending):
        for rows in row_blocks:
            softmax_rows(rows, scores(j, rows, ATT_TILE), p_new, alpha_new)
        flush(pending)
        make_pending()
        return j
    flush(lax.fori_loop(0, i, body, i))

    return acc_sc[:, 0:LANE] / acc_sc[:, LANE:2 * LANE]


_ATT_SCRATCH = [pltpu.VMEM((ATT_TILE, LANE), F32), pltpu.VMEM((ATT_TILE, 2 * LANE), F32),
                pltpu.VMEM((ATT_TILE, ATT_TILE), BF16), pltpu.VMEM((ATT_TILE, LANE), F32),
                pltpu.VMEM((ATT_TILE, ATT_TILE), BF16), pltpu.VMEM((ATT_TILE, LANE), F32)]


def _mla_attn_kernel(q_ref, k_ref, v_ref, o_ref, *scratch):
    mask = lambda r, c: jnp.right_shift(c, CHUNK_SHIFT) <= jnp.right_shift(r, CHUNK_SHIFT)
    out = _flash_tile(pl.program_id(1), q_ref, k_ref, v_ref, *scratch, diag_mask=mask)
    o_ref[...] = out.astype(o_ref.dtype)


def _mla_attn(q, k, v):
    return pl.pallas_call(
        _mla_attn_kernel,
        grid=(MLA_HEADS, N_AT),
        in_specs=[
            pl.BlockSpec((1, ATT_TILE, QK_W), lambda h, i: (h, i, 0)),
            pl.BlockSpec((1, L_BUF, QK_W), lambda h, i: (h, 0, 0)),
            pl.BlockSpec((1, L_BUF, MLA_V), lambda h, i: (h, 0, 0)),
        ],
        out_specs=pl.BlockSpec((ATT_TILE, MLA_V), lambda h, i: (i, h)),
        out_shape=jax.ShapeDtypeStruct((L_BUF, MLA_HEADS * MLA_V), BF16),
        scratch_shapes=_ATT_SCRATCH,
        compiler_params=_cparams(("parallel", "arbitrary")),
        name="mla_attn",
    )(q, k, v)


def _fox_attn_kernel(q_ref, k_ref, v_ref, c_ref, gate_ref, o_ref, *scratch):
    lane = lax.broadcasted_iota(jnp.int32, (ATT_TILE, LANE), 1)
    cq = jnp.sum(jnp.where(lane == pl.program_id(0), c_ref[...], 0.0), axis=-1, keepdims=True)
    out = _flash_tile(pl.program_id(1), q_ref, k_ref, v_ref, *scratch,
                      diag_mask=lambda r, c: c <= r, row_shift=cq)
    o_ref[...] = (out * jax.nn.sigmoid(gate_ref[...])).astype(o_ref.dtype)


def _fox_attn(q, k, v, c, proj):
    return pl.pallas_call(
        _fox_attn_kernel,
        grid=(FOX_HEADS, N_AT),
        in_specs=[
            pl.BlockSpec((1, ATT_TILE, QK_W), lambda h, i: (h, i, 0)),
            pl.BlockSpec((1, L_BUF, QK_W), lambda h, i: (h, 0, 0)),
            pl.BlockSpec((1, L_BUF, FOX_HD), lambda h, i: (0, 0, h)),
            pl.BlockSpec((ATT_TILE, LANE), lambda h, i: (i, 0)),
            pl.BlockSpec((ATT_TILE, FOX_HD), lambda h, i: (i, GATE_COL0 // FOX_HD + h)),
        ],
        out_specs=pl.BlockSpec((ATT_TILE, FOX_HD), lambda h, i: (i, h)),
        out_shape=jax.ShapeDtypeStruct((L_BUF, FOX_W), BF16),
        scratch_shapes=_ATT_SCRATCH,
        compiler_params=_cparams(("parallel", "arbitrary")),
        name="fox_attn",
    )(q, k, v, c, proj)


def _mix_out_kernel(a_ref, b_ref, wa_ref, wb_ref, h_ref, g_ref, o_ref):
    mix = (jnp.dot(a_ref[...], wa_ref[...], preferred_element_type=F32)
           + jnp.dot(b_ref[...], wb_ref[...], preferred_element_type=F32))
    out = h_ref[...] + _rms(mix, g_ref[...])
    row = pl.program_id(0) * TM_OUT + lax.broadcasted_iota(jnp.int32, (TM_OUT, 1), 0)
    o_ref[...] = jnp.where(row >= ROW_PAD, out, 0.0)


def _mix_out(l, a, b, w_o, h, g):
    return pl.pallas_call(
        _mix_out_kernel,
        grid=(L_BUF // TM_OUT,),
        in_specs=[
            pl.BlockSpec((TM_OUT, MLA_HEADS * MLA_V), lambda i: (i, 0)),
            pl.BlockSpec((TM_OUT, FOX_W), lambda i: (i, 0)),
            pl.BlockSpec((None, MLA_HEADS * MLA_V, D_MODEL), lambda i: (l, 0, 0)),
            pl.BlockSpec((None, FOX_W, D_MODEL), lambda i: (l, 1, 0)),
            pl.BlockSpec((TM_OUT, D_MODEL), lambda i: (i, 0)),
            pl.BlockSpec((1, D_MODEL), lambda i: (0, 0)),
        ],
        out_specs=pl.BlockSpec((TM_OUT, D_MODEL), lambda i: (i, 0)),
        out_shape=jax.ShapeDtypeStruct((L_BUF, D_MODEL), F32),
        compiler_params=_cparams(("parallel",)),
        name="mix_out",
    )(a, b, w_o, w_o, h, g)


def _gelu_tanh(x):
    return 0.5 * x * (1.0 + jnp.tanh(np.sqrt(2.0 / np.pi).astype(np.float32) * (x + 0.044715 * (x * x * x))))


def _ffn_kernel(h_ref, halo_ref, gpre_ref, wg_ref, wu_ref, cwg_ref, cwu_ref, cbg_ref, cbu_ref,
                wd_ref, gpost_ref, *rest):
    c = pl.program_id(1)
    if len(rest) == 4:
        o_ref, xn_ref, ug_ref, uu_ref = rest
    else:
        nu_ref, nd_ref, o_ref, nu_out, nd_out, xn_ref, ug_ref, uu_ref = rest

        @pl.when(pl.program_id(0) * pl.num_programs(1) + c < CVT_STEPS)
        def _():
            nu_out[...] = nu_ref[...].astype(BF16)
            nd_out[...] = nd_ref[...].astype(BF16)

    @pl.when(c == 0)
    def _():
        xn_ref[0:HALO, :] = _rms(halo_ref[...], gpre_ref[...]).astype(BF16)

        def body(r, carry):
            src = pl.ds(pl.multiple_of(r * 64, 64), 64)
            dst = pl.ds(pl.multiple_of(HALO + r * 64, 16), 64)
            xn_ref[dst, :] = _rms(h_ref[src, :], gpre_ref[...]).astype(BF16)
            return carry
        lax.fori_loop(0, TM_FFN // 64, body, 0)
        o_ref[...] = jnp.zeros_like(o_ref)

    xn = xn_ref[...]
    ug_ref[...] = jnp.dot(xn, wg_ref[...], preferred_element_type=F32)
    uu_ref[...] = jnp.dot(xn, wu_ref[...], preferred_element_type=F32)

    def conv(u_ref, w_ref, b_ref):
        acc = b_ref[...] + w_ref[CONV_K - 1:CONV_K, :] * u_ref[HALO:HALO + TM_FFN, :]
        for t in range(1, CONV_K):
            acc = acc + w_ref[CONV_K - 1 - t:CONV_K - t, :] * u_ref[HALO - t:HALO - t + TM_FFN, :]
        return acc

    act = _gelu_tanh(conv(ug_ref, cwg_ref, cbg_ref)) * conv(uu_ref, cwu_ref, cbu_ref)
    o_ref[...] += jnp.dot(act.astype(BF16), wd_ref[...], preferred_element_type=F32)

    @pl.when(c == pl.num_programs(1) - 1)
    def _():
        out = h_ref[...] + _rms(o_ref[...], gpost_ref[...])
        row = pl.program_id(0) * TM_FFN + lax.broadcasted_iota(jnp.int32, (TM_FFN, 1), 0)
        o_ref[...] = jnp.where(row >= ROW_PAD, out, 0.0)


def _ffn(h, gpre, w_up, w_conv, b_conv, w_down, gpost, next_weights=None):
    n_fc = D_FF // FC_FFN
    halo_blocks = TM_FFN // HALO
    in_specs = [
        pl.BlockSpec((TM_FFN, D_MODEL), lambda i, c: (i, 0)),
        pl.BlockSpec((HALO, D_MODEL), lambda i, c: (jnp.maximum(i * halo_blocks - 1, 0), 0)),
        pl.BlockSpec((1, D_MODEL), lambda i, c: (0, 0)),
        pl.BlockSpec((D_MODEL, FC_FFN), lambda i, c: (0, c)),
        pl.BlockSpec((D_MODEL, FC_FFN), lambda i, c: (0, n_fc + c)),
        pl.BlockSpec((CONV_K, FC_FFN), lambda i, c: (0, c)),
        pl.BlockSpec((CONV_K, FC_FFN), lambda i, c: (0, n_fc + c)),
        pl.BlockSpec((1, FC_FFN), lambda i, c: (0, c)),
        pl.BlockSpec((1, FC_FFN), lambda i, c: (0, n_fc + c)),
        pl.BlockSpec((FC_FFN, D_MODEL), lambda i, c: (c, 0)),
        pl.BlockSpec((1, D_MODEL), lambda i, c: (0, 0)),
    ]
    operands = [h, h, gpre, w_up, w_up, w_conv, w_conv, b_conv, b_conv, w_down, gpost]
    out_specs = [pl.BlockSpec((TM_FFN, D_MODEL), lambda i, c: (i, 0))]
    out_shape = [jax.ShapeDtypeStruct((L_BUF, D_MODEL), F32)]
    if next_weights is not None:
        w_up_f32, w_down_f32, nl = next_weights
        blk = lambda i, c: jnp.minimum(i * n_fc + c, CVT_STEPS - 1)
        up_cols, down_rows = 2 * D_FF // CVT_STEPS, D_FF // CVT_STEPS
        in_specs += [pl.BlockSpec((None, D_MODEL, up_cols), lambda i, c: (nl, 0, blk(i, c))),
                     pl.BlockSpec((None, down_rows, D_MODEL), lambda i, c: (nl, blk(i, c), 0))]
        operands += [w_up_f32, w_down_f32]
        out_specs += [pl.BlockSpec((D_MODEL, up_cols), lambda i, c: (0, blk(i, c))),
                      pl.BlockSpec((down_rows, D_MODEL), lambda i, c: (blk(i, c), 0))]
        out_shape += [jax.ShapeDtypeStruct((D_MODEL, 2 * D_FF), BF16),
                      jax.ShapeDtypeStruct((D_FF, D_MODEL), BF16)]
    return pl.pallas_call(
        _ffn_kernel,
        grid=(L_BUF // TM_FFN, n_fc),
        in_specs=in_specs,
        out_specs=out_specs,
        out_shape=out_shape,
        scratch_shapes=[pltpu.VMEM((HALO + TM_FFN, D_MODEL), BF16),
                        pltpu.VMEM((HALO + TM_FFN, FC_FFN), F32),
                        pltpu.VMEM((HALO + TM_FFN, FC_FFN), F32)],
        compiler_params=_cparams(("arbitrary", "arbitrary"), VMEM_LIMIT_FFN),
        name="conv_ffn",
    )(*operands)


def _rotate_half_cols(w):
    half = w.shape[-1] // 2
    return jnp.concatenate([-w[..., half:], w[..., :half]], axis=-1)


def _pad_cols(w, width):
    return jnp.pad(w, [(0, 0)] * (w.ndim - 1) + [(0, width - w.shape[-1])])


def _pack_w_in(w):
    o = np.cumsum([0, MLA_Q_LORA, MLA_KV_LORA, MLA_ROPE, FOX_W, FOX_W, FOX_W, FOX_W, FOX_HEADS])
    c_q, c_kv, k_rope, fq, fk, fv, fg, ff = [w[..., o[n]:o[n + 1]].astype(BF16) for n in range(8)]
    tail = jnp.concatenate([_pad_cols(k_rope, LANE), _pad_cols(_rotate_half_cols(k_rope), LANE),
                            _pad_cols(ff, 2 * LANE)], axis=-1)
    return jnp.concatenate([c_q, c_kv, fq, fk, fg, tail, fv], axis=-1)


def _pack_w_q_up(w):
    w = w.astype(BF16).reshape(DEPTH, MLA_Q_LORA, MLA_HEADS, MLA_NOPE + MLA_ROPE)
    nope, rope = w[..., :MLA_NOPE], w[..., MLA_NOPE:]
    packed = jnp.concatenate([nope, _pad_cols(rope, LANE), _pad_cols(_rotate_half_cols(rope), LANE)], axis=-1)
    return packed.reshape(DEPTH, MLA_Q_LORA, MLA_HEADS * Q_HEAD_W)


def _rope_tables():
    pos = jnp.maximum(jnp.arange(L_BUF, dtype=jnp.int32) - ROW_PAD, 0).astype(F32)
    half = MLA_ROPE // 2
    inv_freq = ROPE_THETA ** (-jnp.arange(half, dtype=F32) / half)
    ang = pos[:, None] * inv_freq[None, :]
    zeros = jnp.zeros((L_BUF, LANE - MLA_ROPE), F32)
    cos, sin = jnp.cos(ang), jnp.sin(ang)
    return (jnp.concatenate([cos, cos, zeros], axis=1), jnp.concatenate([sin, sin, zeros], axis=1))


def kernel(x, meta_tokens, ln_mix_pre, w_in, b_forget, g_q_latent, g_kv_latent, w_q_up, w_kv_up,
           g_fox_q, g_fox_k, w_out, ln_mix_post, ln_ffn_pre, w_ffn_up, w_ffn_conv, b_ffn_conv,
           w_ffn_down, ln_ffn_post):
    assert x.shape == (1, SEQ, D_MODEL), x.shape
    h = jnp.concatenate([jnp.zeros((ROW_PAD, D_MODEL), x.dtype), meta_tokens.astype(x.dtype), x[0]], axis=0)
    cos_t, sin_t = _rope_tables()
    tri = (lax.broadcasted_iota(jnp.int32, (Q_TILE, Q_TILE), 0)
           >= lax.broadcasted_iota(jnp.int32, (Q_TILE, Q_TILE), 1)).astype(BF16)
    row2d = lambda v: v.reshape(1, -1).astype(F32)
    w_in_p, w_q_p = _pack_w_in(w_in), _pack_w_q_up(w_q_up)
    w_kv_b, w_o_b = w_kv_up.astype(BF16), w_out.astype(BF16)
    w_up_b, w_down_b = w_ffn_up[0].astype(BF16), w_ffn_down[0].astype(BF16)
    assert MLA_HEADS * MLA_V == FOX_W

    for l in range(DEPTH):
        proj, vf = _proj_in(l, h, row2d(ln_mix_pre[l]), w_in_p)
        q, k, v, qf, kf, c = _prep(
            l, proj, cos_t, sin_t, row2d(g_q_latent[l]), row2d(g_kv_latent[l]),
            row2d(g_fox_q[l]), row2d(g_fox_k[l]), _pad_cols(row2d(b_forget[l]), LANE),
            w_q_p, w_kv_b, tri)
        a = _mla_attn(q, k, v)
        b = _fox_attn(qf, kf, vf, c, proj)
        h = _mix_out(l, a, b, w_o_b, h, row2d(ln_mix_post[l]))
        next_weights = (w_ffn_up, w_ffn_down, l + 1) if l + 1 < DEPTH else None
        h, *converted = _ffn(h, row2d(ln_ffn_pre[l]), w_up_b, w_ffn_conv[l].astype(F32),
                             row2d(b_ffn_conv[l]), w_down_b, row2d(ln_ffn_post[l]), next_weights)
        if converted:
            w_up_b, w_down_b = converted

    return h[ROW_PAD + N_META:][None]
```

```python
import jax
import jax.numpy as jnp
import numpy as np
from jax import lax
from jax.experimental import pallas as pl
from jax.experimental.pallas import tpu as pltpu

F32 = jnp.float32
BF16 = jnp.bfloat16

D_MODEL = 2048
SEQ = 8192
DEPTH = 4
CHUNK = 64
CHUNK_SHIFT = 6
N_META = 16
MLA_HEADS = 8
MLA_Q_LORA = 512
MLA_KV_LORA = 512
MLA_NOPE = 128
MLA_ROPE = 64
MLA_V = 128
ROPE_THETA = 10000.0
FOX_HEADS = 8
FOX_HD = 128
FOX_W = FOX_HEADS * FOX_HD
D_FF = 5632
CONV_K = 3
EPS = 1e-6
NEG = -1e30

LANE = 128
Q_TILE = 256
ROW_PAD = Q_TILE - N_META
L_BUF = ROW_PAD + N_META + SEQ
ATT_TILE = 768
ROW_BLK = 256
N_AT = L_BUF // ATT_TILE

LOG2E = float(np.log2(np.e))
MLA_SCALE = (MLA_NOPE + MLA_ROPE) ** -0.5 * LOG2E
FOX_SCALE = FOX_HD ** -0.5 * LOG2E

TAIL_W = 4 * LANE
IN_MAIN = MLA_Q_LORA + MLA_KV_LORA + 3 * FOX_W + TAIL_W
IN_PACKED = IN_MAIN + FOX_W
GATE_COL0 = MLA_Q_LORA + MLA_KV_LORA + 2 * FOX_W
Q_HEAD_W = 3 * LANE
QK_W = 2 * LANE

TM_PROJ = 1408
TN_PROJ = 512
TM_OUT = 256
CVT_ROW_STEPS = 32
TM_FFN = 768
FC_FFN = 512
HALO = 16
CVT_STEPS = 88

VMEM_LIMIT = 56 * 1024 * 1024
VMEM_LIMIT_FFN = 58 * 1024 * 1024


def _rms(x, g):
    ms = jnp.mean(x * x, axis=-1, keepdims=True)
    return x * lax.rsqrt(ms + EPS) * g


def _split3(x):
    hi = x.astype(BF16).astype(F32)
    r = x - hi
    mid = r.astype(BF16).astype(F32)
    return hi, mid, r - mid


def _cparams(sem, vmem_limit=VMEM_LIMIT):
    return pltpu.CompilerParams(dimension_semantics=sem, vmem_limit_bytes=vmem_limit)


def _proj_in_kernel(x_ref, g_ref, w_ref, o_ref, ov_ref, xn_ref):
    j = pl.program_id(1)

    @pl.when(j == 0)
    def _():
        def body(r, carry):
            rows = pl.ds(pl.multiple_of(r * 64, 64), 64)
            xn_ref[rows, :] = _rms(x_ref[rows, :], g_ref[...]).astype(BF16)
            return carry
        lax.fori_loop(0, TM_PROJ // 64, body, 0)

    res = jnp.dot(xn_ref[...], w_ref[...], preferred_element_type=F32)

    @pl.when(j < IN_MAIN // TN_PROJ)
    def _():
        o_ref[...] = res

    @pl.when(j >= IN_MAIN // TN_PROJ)
    def _():
        ov_ref[...] = res.astype(BF16)


def _proj_in(h, g, w):
    n_main = IN_MAIN // TN_PROJ
    return pl.pallas_call(
        _proj_in_kernel,
        grid=(L_BUF // TM_PROJ, IN_PACKED // TN_PROJ),
        in_specs=[
            pl.BlockSpec((TM_PROJ, D_MODEL), lambda i, j: (i, 0)),
            pl.BlockSpec((1, D_MODEL), lambda i, j: (0, 0)),
            pl.BlockSpec((D_MODEL, TN_PROJ), lambda i, j: (0, j)),
        ],
        out_specs=[
            pl.BlockSpec((TM_PROJ, TN_PROJ), lambda i, j: (i, jnp.minimum(j, n_main - 1))),
            pl.BlockSpec((None, TM_PROJ, TN_PROJ), lambda i, j: (0, i, jnp.maximum(j - n_main, 0))),
        ],
        out_shape=[jax.ShapeDtypeStruct((L_BUF, IN_MAIN), F32),
                   jax.ShapeDtypeStruct((1, L_BUF, FOX_W), BF16)],
        scratch_shapes=[pltpu.VMEM((TM_PROJ, D_MODEL), BF16)],
        compiler_params=_cparams(("parallel", "arbitrary")),
        name="proj_in",
    )(h, g, w)


def _prep_kernel(cq_ref, ckv_ref, fq_ref, fk_ref, tail_ref, cos_ref, sin_ref,
                 gql_ref, gkvl_ref, gfq_ref, gfk_ref, bf_ref, wq_ref, wkv_ref, tri_ref,
                 q_ref, k_ref, v_ref, qf_ref, kf_ref, c_ref, carry_ref):
    i = pl.program_id(0)
    lane = lax.broadcasted_iota(jnp.int32, (Q_TILE, LANE), 1)
    is_pad = (i * Q_TILE + lax.broadcasted_iota(jnp.int32, (Q_TILE, 1), 0)) < ROW_PAD
    cos = cos_ref[...]
    sin = sin_ref[...]
    q_flag = jnp.where(lane == MLA_ROPE, 1.0, 0.0)
    k_flag = jnp.where((lane == MLA_ROPE) & is_pad, NEG, 0.0)
    kr = (tail_ref[:, 0:LANE] * cos + tail_ref[:, LANE:2 * LANE] * sin + k_flag).astype(BF16)
    cqn = _rms(cq_ref[...], gql_ref[...]).astype(BF16)
    ckvn = _rms(ckv_ref[...], gkvl_ref[...]).astype(BF16)
    for h in range(MLA_HEADS):
        qh = jnp.dot(cqn, wq_ref[:, Q_HEAD_W * h:Q_HEAD_W * (h + 1)], preferred_element_type=F32)
        qr = qh[:, LANE:2 * LANE] * cos + qh[:, 2 * LANE:3 * LANE] * sin
        q_ref[h, :, 0:LANE] = (qh[:, 0:LANE] * MLA_SCALE).astype(BF16)
        q_ref[h, :, LANE:QK_W] = (qr * MLA_SCALE + q_flag).astype(BF16)
        kvh = jnp.dot(ckvn, wkv_ref[:, 2 * LANE * h:2 * LANE * (h + 1)], preferred_element_type=F32)
        k_ref[h, :, 0:LANE] = kvh[:, 0:LANE].astype(BF16)
        k_ref[h, :, LANE:QK_W] = kr
        v_ref[h] = kvh[:, LANE:2 * LANE].astype(BF16)

    z = tail_ref[:, 2 * LANE:3 * LANE] + bf_ref[...]
    logf = (jnp.minimum(z, 0.0) - jnp.log(1.0 + jnp.exp(-jnp.abs(z)))) * LOG2E
    hi, mid, lo = _split3(logf)
    tri = tri_ref[...]
    cs = (jnp.dot(tri, hi.astype(BF16), preferred_element_type=F32)
          + jnp.dot(tri, mid.astype(BF16), preferred_element_type=F32)
          + jnp.dot(tri, lo.astype(BF16), preferred_element_type=F32))

    @pl.when(i == 0)
    def _():
        carry_ref[...] = jnp.zeros_like(carry_ref)

    c = cs + carry_ref[0:1, :]
    c_ref[...] = c
    carry_ref[...] = jnp.broadcast_to(c[Q_TILE - 1:Q_TILE, :], carry_ref.shape)

    qf_ext = jnp.where(lane < 3, -1.0, 0.0).astype(BF16)
    for h in range(FOX_HEADS):
        cols = slice(FOX_HD * h, FOX_HD * (h + 1))
        qf_ref[h, :, 0:FOX_HD] = (_rms(fq_ref[:, cols], gfq_ref[...]) * FOX_SCALE).astype(BF16)
        qf_ref[h, :, FOX_HD:QK_W] = qf_ext
        kf_ref[h, :, 0:FOX_HD] = _rms(fk_ref[:, cols], gfk_ref[...]).astype(BF16)
        ck = jnp.sum(jnp.where(lane == h, c, 0.0), axis=-1, keepdims=True)
        ck_hi, ck_mid, ck_lo = _split3(jnp.where(is_pad, -NEG, ck))
        ext = jnp.where(lane == 0, ck_hi, jnp.where(lane == 1, ck_mid, jnp.where(lane == 2, ck_lo, 0.0)))
        kf_ref[h, :, FOX_HD:QK_W] = ext.astype(BF16)


def _prep(l, proj, cos_t, sin_t, gql, gkvl, gfq, gfk, bf_pad, wq, wkv, tri):
    tm = Q_TILE
    row = lambda i: (i, 0)
    const = lambda i: (0, 0)
    head_out = lambda w: pl.BlockSpec((MLA_HEADS, tm, w), lambda i: (0, i, 0))
    return pl.pallas_call(
        _prep_kernel,
        grid=(L_BUF // tm,),
        in_specs=[
            pl.BlockSpec((tm, MLA_Q_LORA), lambda i: (i, 0)),
            pl.BlockSpec((tm, MLA_KV_LORA), lambda i: (i, 1)),
            pl.BlockSpec((tm, FOX_W), lambda i: (i, 1)),
            pl.BlockSpec((tm, FOX_W), lambda i: (i, 2)),
            pl.BlockSpec((tm, TAIL_W), lambda i: (i, IN_MAIN // TAIL_W - 1)),
            pl.BlockSpec((tm, LANE), row),
            pl.BlockSpec((tm, LANE), row),
            pl.BlockSpec((1, MLA_Q_LORA), const),
            pl.BlockSpec((1, MLA_KV_LORA), const),
            pl.BlockSpec((1, FOX_HD), const),
            pl.BlockSpec((1, FOX_HD), const),
            pl.BlockSpec((1, LANE), const),
            pl.BlockSpec((None, MLA_Q_LORA, MLA_HEADS * Q_HEAD_W), lambda i: (l, 0, 0)),
            pl.BlockSpec((None, MLA_KV_LORA, MLA_HEADS * 2 * LANE), lambda i: (l, 0, 0)),
            pl.BlockSpec((tm, tm), const),
        ],
        out_specs=[
            head_out(QK_W), head_out(QK_W), head_out(MLA_V),
            head_out(QK_W), head_out(QK_W),
            pl.BlockSpec((tm, LANE), row),
        ],
        out_shape=[
            jax.ShapeDtypeStruct((MLA_HEADS, L_BUF, QK_W), BF16),
            jax.ShapeDtypeStruct((MLA_HEADS, L_BUF, QK_W), BF16),
            jax.ShapeDtypeStruct((MLA_HEADS, L_BUF, MLA_V), BF16),
            jax.ShapeDtypeStruct((FOX_HEADS, L_BUF, QK_W), BF16),
            jax.ShapeDtypeStruct((FOX_HEADS, L_BUF, QK_W), BF16),
            jax.ShapeDtypeStruct((L_BUF, LANE), F32),
        ],
        scratch_shapes=[pltpu.VMEM((8, LANE), F32)],
        compiler_params=_cparams(("arbitrary",)),
        name="attn_prep",
    )(proj, proj, proj, proj, proj, cos_t, sin_t, gql, gkvl, gfq, gfk, bf_pad, wq, wkv, tri)


def _scores(q, k):
    return lax.dot_general(q, k, (((1,), (1,)), ((), ())), preferred_element_type=F32)


def _flash_tile(i, q_ref, k_ref, v_ref, m_sc, acc_sc, p_sc, alpha_sc, p_new, alpha_new,
                diag_mask, row_shift=None):
    m_sc[...] = jnp.full_like(m_sc, NEG)
    acc_sc[...] = jnp.zeros_like(acc_sc)
    row_blocks = [slice(rb * ROW_BLK, (rb + 1) * ROW_BLK) for rb in range(ATT_TILE // ROW_BLK)]

    def chunk_start(j):
        return pl.multiple_of(j * ATT_TILE, ATT_TILE)

    def flush(j):
        v = v_ref[0, pl.ds(chunk_start(j), ATT_TILE), :]
        v_ones = jnp.concatenate([v, jnp.ones_like(v)], axis=1)
        for rows in row_blocks:
            pv = acc_sc[rows, :] * jnp.tile(alpha_sc[rows, :], (1, 2))
            for kc in range(0, ATT_TILE, 2 * LANE):
                pv = pv + jnp.dot(p_sc[rows, kc:kc + 2 * LANE], v_ones[kc:kc + 2 * LANE, :],
                                  preferred_element_type=F32)
            acc_sc[rows, :] = pv

    def scores(j, rows, ncols):
        return _scores(q_ref[0, rows, :], k_ref[0, pl.ds(chunk_start(j), ncols), :])

    def softmax_rows(rows, s, p_dst, alpha_dst):
        ncols = s.shape[1]
        m_prev = m_sc[rows, :]
        m_chunk = jnp.max(s, axis=-1, keepdims=True)
        if row_shift is not None:
            m_chunk = m_chunk + row_shift[rows, :]
        m_new = jnp.maximum(m_prev, m_chunk)
        alpha_dst[rows, :] = jnp.exp2(m_prev - m_new)
        sub = m_new if row_shift is None else m_new - row_shift[rows, :]
        p_dst[rows, 0:ncols] = jnp.exp2(s - jnp.tile(sub, (1, ncols // LANE))).astype(BF16)
        m_sc[rows, :] = m_new

    def make_pending():
        p_sc[...] = p_new[...]
        alpha_sc[...] = alpha_new[...]

    for rows in row_blocks:
        ncols = rows.stop
        s = scores(i, rows, ncols)
        r = lax.broadcasted_iota(jnp.int32, s.shape, 0) + rows.start
        c = lax.broadcasted_iota(jnp.int32, s.shape, 1)
        if ncols < ATT_TILE:
            p_sc[rows, ncols:] = jnp.zeros((ROW_BLK, ATT_TILE - ncols), BF16)
        softmax_rows(rows, jnp.where(diag_mask(r, c), s, NEG), p_sc, alpha_sc)

    def body(j, pending):
        for rows in row_blocks:
            softmax_rows(rows, scores(j, rows, ATT_TILE), p_new, alpha_new)
        flush(pending)
        make_pending()
        return j
    flush(lax.fori_loop(0, i, body, i))

    return acc_sc[:, 0:LANE] / acc_sc[:, LANE:2 * LANE]


_ATT_SCRATCH = [pltpu.VMEM((ATT_TILE, LANE), F32), pltpu.VMEM((ATT_TILE, 2 * LANE), F32),
                pltpu.VMEM((ATT_TILE, ATT_TILE), BF16), pltpu.VMEM((ATT_TILE, LANE), F32),
                pltpu.VMEM((ATT_TILE, ATT_TILE), BF16), pltpu.VMEM((ATT_TILE, LANE), F32)]


def _mla_attn_kernel(q_ref, k_ref, v_ref, o_ref, *scratch):
    mask = lambda r, c: jnp.right_shift(c, CHUNK_SHIFT) <= jnp.right_shift(r, CHUNK_SHIFT)
    out = _flash_tile(pl.program_id(1), q_ref, k_ref, v_ref, *scratch, diag_mask=mask)
    o_ref[...] = out.astype(o_ref.dtype)


def _mla_attn(q, k, v):
    return pl.pallas_call(
        _mla_attn_kernel,
        grid=(MLA_HEADS, N_AT),
        in_specs=[
            pl.BlockSpec((1, ATT_TILE, QK_W), lambda h, i: (h, i, 0)),
            pl.BlockSpec((1, L_BUF, QK_W), lambda h, i: (h, 0, 0)),
            pl.BlockSpec((1, L_BUF, MLA_V), lambda h, i: (h, 0, 0)),
        ],
        out_specs=pl.BlockSpec((ATT_TILE, MLA_V), lambda h, i: (i, h)),
        out_shape=jax.ShapeDtypeStruct((L_BUF, MLA_HEADS * MLA_V), BF16),
        scratch_shapes=_ATT_SCRATCH,
        compiler_params=_cparams(("parallel", "arbitrary")),
        name="mla_attn",
    )(q, k, v)


def _fox_attn_kernel(q_ref, k_ref, v_ref, c_ref, gate_ref, o_ref, *scratch):
    lane = lax.broadcasted_iota(jnp.int32, (ATT_TILE, LANE), 1)
    cq = jnp.sum(jnp.where(lane == pl.program_id(0), c_ref[...], 0.0), axis=-1, keepdims=True)
    out = _flash_tile(pl.program_id(1), q_ref, k_ref, v_ref, *scratch,
                      diag_mask=lambda r, c: c <= r, row_shift=cq)
    o_ref[...] = (out * jax.nn.sigmoid(gate_ref[...])).astype(o_ref.dtype)


def _fox_attn(q, k, v, c, proj):
    return pl.pallas_call(
        _fox_attn_kernel,
        grid=(FOX_HEADS, N_AT),
        in_specs=[
            pl.BlockSpec((1, ATT_TILE, QK_W), lambda h, i: (h, i, 0)),
            pl.BlockSpec((1, L_BUF, QK_W), lambda h, i: (h, 0, 0)),
            pl.BlockSpec((1, L_BUF, FOX_HD), lambda h, i: (0, 0, h)),
            pl.BlockSpec((ATT_TILE, LANE), lambda h, i: (i, 0)),
            pl.BlockSpec((ATT_TILE, FOX_HD), lambda h, i: (i, GATE_COL0 // FOX_HD + h)),
        ],
        out_specs=pl.BlockSpec((ATT_TILE, FOX_HD), lambda h, i: (i, h)),
        out_shape=jax.ShapeDtypeStruct((L_BUF, FOX_W), BF16),
        scratch_shapes=_ATT_SCRATCH,
        compiler_params=_cparams(("parallel", "arbitrary")),
        name="fox_attn",
    )(q, k, v, c, proj)


def _mix_out_kernel(a_ref, b_ref, wa_ref, wb_ref, h_ref, g_ref, *rest):
    if len(rest) == 1:
        (o_ref,) = rest
    else:
        nwi_ref, nwo_ref, o_ref, nwi_out, nwo_out = rest

        @pl.when(pl.program_id(0) < CVT_ROW_STEPS)
        def _():
            nwi_out[...] = nwi_ref[...].astype(BF16)
            nwo_out[...] = nwo_ref[...].astype(BF16)

    mix = (jnp.dot(a_ref[...], wa_ref[...], preferred_element_type=F32)
           + jnp.dot(b_ref[...], wb_ref[...], preferred_element_type=F32))
    out = h_ref[...] + _rms(mix, g_ref[...])
    row = pl.program_id(0) * TM_OUT + lax.broadcasted_iota(jnp.int32, (TM_OUT, 1), 0)
    o_ref[...] = jnp.where(row >= ROW_PAD, out, 0.0)


def _mix_out(a, b, w_o, h, g, next_weights=None):
    in_specs = [
        pl.BlockSpec((TM_OUT, MLA_HEADS * MLA_V), lambda i: (i, 0)),
        pl.BlockSpec((TM_OUT, FOX_W), lambda i: (i, 0)),
        pl.BlockSpec((MLA_HEADS * MLA_V, D_MODEL), lambda i: (0, 0)),
        pl.BlockSpec((FOX_W, D_MODEL), lambda i: (1, 0)),
        pl.BlockSpec((TM_OUT, D_MODEL), lambda i: (i, 0)),
        pl.BlockSpec((1, D_MODEL), lambda i: (0, 0)),
    ]
    operands = [a, b, w_o, w_o, h, g]
    out_specs = [pl.BlockSpec((TM_OUT, D_MODEL), lambda i: (i, 0))]
    out_shape = [jax.ShapeDtypeStruct((L_BUF, D_MODEL), F32)]
    if next_weights is not None:
        w_in_f32, w_out_f32, nl = next_weights
        blk = lambda i: jnp.minimum(i, CVT_ROW_STEPS - 1)
        rows = D_MODEL // CVT_ROW_STEPS
        in_cols = w_in_f32.shape[-1]
        in_specs += [pl.BlockSpec((None, rows, in_cols), lambda i: (nl, blk(i), 0)),
                     pl.BlockSpec((None, rows, D_MODEL), lambda i: (nl, blk(i), 0))]
        operands += [w_in_f32, w_out_f32]
        out_specs += [pl.BlockSpec((rows, in_cols), lambda i: (blk(i), 0)),
                      pl.BlockSpec((rows, D_MODEL), lambda i: (blk(i), 0))]
        out_shape += [jax.ShapeDtypeStruct((D_MODEL, in_cols), BF16),
                      jax.ShapeDtypeStruct((MLA_HEADS * MLA_V + FOX_W, D_MODEL), BF16)]
    return pl.pallas_call(
        _mix_out_kernel,
        grid=(L_BUF // TM_OUT,),
        in_specs=in_specs,
        out_specs=out_specs,
        out_shape=out_shape,
        compiler_params=_cparams(("arbitrary",)),
        name="mix_out",
    )(*operands)


def _gelu_tanh(x):
    return 0.5 * x * (1.0 + jnp.tanh(np.sqrt(2.0 / np.pi).astype(np.float32) * (x + 0.044715 * (x * x * x))))


def _ffn_kernel(h_ref, halo_ref, gpre_ref, wg_ref, wu_ref, cwg_ref, cwu_ref, cbg_ref, cbu_ref,
                wd_ref, gpost_ref, *rest):
    c = pl.program_id(1)
    if len(rest) == 4:
        o_ref, xn_ref, ug_ref, uu_ref = rest
    else:
        nu_ref, nd_ref, o_ref, nu_out, nd_out, xn_ref, ug_ref, uu_ref = rest

        @pl.when(pl.program_id(0) * pl.num_programs(1) + c < CVT_STEPS)
        def _():
            nu_out[...] = nu_ref[...].astype(BF16)
            nd_out[...] = nd_ref[...].astype(BF16)

    @pl.when(c == 0)
    def _():
        xn_ref[0:HALO, :] = _rms(halo_ref[...], gpre_ref[...]).astype(BF16)

        def body(r, carry):
            src = pl.ds(pl.multiple_of(r * 64, 64), 64)
            dst = pl.ds(pl.multiple_of(HALO + r * 64, 16), 64)
            xn_ref[dst, :] = _rms(h_ref[src, :], gpre_ref[...]).astype(BF16)
            return carry
        lax.fori_loop(0, TM_FFN // 64, body, 0)
        o_ref[...] = jnp.zeros_like(o_ref)

    xn = xn_ref[...]
    ug_ref[...] = jnp.dot(xn, wg_ref[...], preferred_element_type=F32)
    uu_ref[...] = jnp.dot(xn, wu_ref[...], preferred_element_type=F32)

    def conv(u_ref, w_ref, b_ref):
        acc = b_ref[...] + w_ref[CONV_K - 1:CONV_K, :] * u_ref[HALO:HALO + TM_FFN, :]
        for t in range(1, CONV_K):
            acc = acc + w_ref[CONV_K - 1 - t:CONV_K - t, :] * u_ref[HALO - t:HALO - t + TM_FFN, :]
        return acc

    act = _gelu_tanh(conv(ug_ref, cwg_ref, cbg_ref)) * conv(uu_ref, cwu_ref, cbu_ref)
    o_ref[...] += jnp.dot(act.astype(BF16), wd_ref[...], preferred_element_type=F32)

    @pl.when(c == pl.num_programs(1) - 1)
    def _():
        out = h_ref[...] + _rms(o_ref[...], gpost_ref[...])
        row = pl.program_id(0) * TM_FFN + lax.broadcasted_iota(jnp.int32, (TM_FFN, 1), 0)
        o_ref[...] = jnp.where(row >= ROW_PAD, out, 0.0)


def _ffn(h, gpre, w_up, w_conv, b_conv, w_down, gpost, next_weights=None):
    n_fc = D_FF // FC_FFN
    halo_blocks = TM_FFN // HALO
    in_specs = [
        pl.BlockSpec((TM_FFN, D_MODEL), lambda i, c: (i, 0)),
        pl.BlockSpec((HALO, D_MODEL), lambda i, c: (jnp.maximum(i * halo_blocks - 1, 0), 0)),
        pl.BlockSpec((1, D_MODEL), lambda i, c: (0, 0)),
        pl.BlockSpec((D_MODEL, FC_FFN), lambda i, c: (0, c)),
        pl.BlockSpec((D_MODEL, FC_FFN), lambda i, c: (0, n_fc + c)),
        pl.BlockSpec((CONV_K, FC_FFN), lambda i, c: (0, c)),
        pl.BlockSpec((CONV_K, FC_FFN), lambda i, c: (0, n_fc + c)),
        pl.BlockSpec((1, FC_FFN), lambda i, c: (0, c)),
        pl.BlockSpec((1, FC_FFN), lambda i, c: (0, n_fc + c)),
        pl.BlockSpec((FC_FFN, D_MODEL), lambda i, c: (c, 0)),
        pl.BlockSpec((1, D_MODEL), lambda i, c: (0, 0)),
    ]
    operands = [h, h, gpre, w_up, w_up, w_conv, w_conv, b_conv, b_conv, w_down, gpost]
    out_specs = [pl.BlockSpec((TM_FFN, D_MODEL), lambda i, c: (i, 0))]
    out_shape = [jax.ShapeDtypeStruct((L_BUF, D_MODEL), F32)]
    if next_weights is not None:
        w_up_f32, w_down_f32, nl = next_weights
        blk = lambda i, c: jnp.minimum(i * n_fc + c, CVT_STEPS - 1)
        up_cols, down_rows = 2 * D_FF // CVT_STEPS, D_FF // CVT_STEPS
        in_specs += [pl.BlockSpec((None, D_MODEL, up_cols), lambda i, c: (nl, 0, blk(i, c))),
                     pl.BlockSpec((None, down_rows, D_MODEL), lambda i, c: (nl, blk(i, c), 0))]
        operands += [w_up_f32, w_down_f32]
        out_specs += [pl.BlockSpec((D_MODEL, up_cols), lambda i, c: (0, blk(i, c))),
                      pl.BlockSpec((down_rows, D_MODEL), lambda i, c: (blk(i, c), 0))]
        out_shape += [jax.ShapeDtypeStruct((D_MODEL, 2 * D_FF), BF16),
                      jax.ShapeDtypeStruct((D_FF, D_MODEL), BF16)]
    return pl.pallas_call(
        _ffn_kernel,
        grid=(L_BUF // TM_FFN, n_fc),
        in_specs=in_specs,
        out_specs=out_specs,
        out_shape=out_shape,
        scratch_shapes=[pltpu.VMEM((HALO + TM_FFN, D_MODEL), BF16),
                        pltpu.VMEM((HALO + TM_FFN, FC_FFN), F32),
                        pltpu.VMEM((HALO + TM_FFN, FC_FFN), F32)],
        compiler_params=_cparams(("arbitrary", "arbitrary"), VMEM_LIMIT_FFN),
        name="conv_ffn",
    )(*operands)


def _rotate_half_cols(w):
    half = w.shape[-1] // 2
    return jnp.concatenate([-w[..., half:], w[..., :half]], axis=-1)


def _pad_cols(w, width):
    return jnp.pad(w, [(0, 0)] * (w.ndim - 1) + [(0, width - w.shape[-1])])


def _pack_w_in(w):
    o = np.cumsum([0, MLA_Q_LORA, MLA_KV_LORA, MLA_ROPE, FOX_W, FOX_W, FOX_W, FOX_W, FOX_HEADS])
    c_q, c_kv, k_rope, fq, fk, fv, fg, ff = [w[..., o[n]:o[n + 1]].astype(BF16) for n in range(8)]
    tail = jnp.concatenate([_pad_cols(k_rope, LANE), _pad_cols(_rotate_half_cols(k_rope), LANE),
                            _pad_cols(ff, 2 * LANE)], axis=-1)
    return jnp.concatenate([c_q, c_kv, fq, fk, fg, tail, fv], axis=-1)


def _pack_w_q_up(w):
    w = w.astype(BF16).reshape(DEPTH, MLA_Q_LORA, MLA_HEADS, MLA_NOPE + MLA_ROPE)
    nope, rope = w[..., :MLA_NOPE], w[..., MLA_NOPE:]
    packed = jnp.concatenate([nope, _pad_cols(rope, LANE), _pad_cols(_rotate_half_cols(rope), LANE)], axis=-1)
    return packed.reshape(DEPTH, MLA_Q_LORA, MLA_HEADS * Q_HEAD_W)


def _rope_tables():
    pos = jnp.maximum(jnp.arange(L_BUF, dtype=jnp.int32) - ROW_PAD, 0).astype(F32)
    half = MLA_ROPE // 2
    inv_freq = ROPE_THETA ** (-jnp.arange(half, dtype=F32) / half)
    ang = pos[:, None] * inv_freq[None, :]
    zeros = jnp.zeros((L_BUF, LANE - MLA_ROPE), F32)
    cos, sin = jnp.cos(ang), jnp.sin(ang)
    return (jnp.concatenate([cos, cos, zeros], axis=1), jnp.concatenate([sin, sin, zeros], axis=1))


def kernel(x, meta_tokens, ln_mix_pre, w_in, b_forget, g_q_latent, g_kv_latent, w_q_up, w_kv_up,
           g_fox_q, g_fox_k, w_out, ln_mix_post, ln_ffn_pre, w_ffn_up, w_ffn_conv, b_ffn_conv,
           w_ffn_down, ln_ffn_post):
    assert x.shape == (1, SEQ, D_MODEL), x.shape
    h = jnp.concatenate([jnp.zeros((ROW_PAD, D_MODEL), x.dtype), meta_tokens.astype(x.dtype), x[0]], axis=0)
    cos_t, sin_t = _rope_tables()
    tri = (lax.broadcasted_iota(jnp.int32, (Q_TILE, Q_TILE), 0)
           >= lax.broadcasted_iota(jnp.int32, (Q_TILE, Q_TILE), 1)).astype(BF16)
    row2d = lambda v: v.reshape(1, -1).astype(F32)
    w_q_p, w_kv_b = _pack_w_q_up(w_q_up), w_kv_up.astype(BF16)
    w_in_b, w_o_b = w_in[0].astype(BF16), w_out[0].astype(BF16)
    w_up_b, w_down_b = w_ffn_up[0].astype(BF16), w_ffn_down[0].astype(BF16)
    assert MLA_HEADS * MLA_V == FOX_W

    for l in range(DEPTH):
        proj, vf = _proj_in(h, row2d(ln_mix_pre[l]), _pack_w_in(w_in_b))
        q, k, v, qf, kf, c = _prep(
            l, proj, cos_t, sin_t, row2d(g_q_latent[l]), row2d(g_kv_latent[l]),
            row2d(g_fox_q[l]), row2d(g_fox_k[l]), _pad_cols(row2d(b_forget[l]), LANE),
            w_q_p, w_kv_b, tri)
        a = _mla_attn(q, k, v)
        b = _fox_attn(qf, kf, vf, c, proj)
        next_weights = (w_in, w_out, l + 1) if l + 1 < DEPTH else None
        h, *converted = _mix_out(a, b, w_o_b, h, row2d(ln_mix_post[l]), next_weights)
        if converted:
            w_in_b, w_o_b = converted
        next_weights = (w_ffn_up, w_ffn_down, l + 1) if l + 1 < DEPTH else None
        h, *converted = _ffn(h, row2d(ln_ffn_pre[l]), w_up_b, w_ffn_conv[l].astype(F32),
                             row2d(b_ffn_conv[l]), w_down_b, row2d(ln_ffn_post[l]), next_weights)
        if converted:
            w_up_b, w_down_b = converted

    return h[ROW_PAD + N_META:][None]
```

```python
import jax
import jax.numpy as jnp
import numpy as np
from jax import lax
from jax.experimental import pallas as pl
from jax.experimental.pallas import tpu as pltpu

F32 = jnp.float32
BF16 = jnp.bfloat16

D_MODEL = 2048
SEQ = 8192
DEPTH = 4
CHUNK = 64
CHUNK_SHIFT = 6
N_META = 16
MLA_HEADS = 8
MLA_Q_LORA = 512
MLA_KV_LORA = 512
MLA_NOPE = 128
MLA_ROPE = 64
MLA_V = 128
ROPE_THETA = 10000.0
FOX_HEADS = 8
FOX_HD = 128
FOX_W = FOX_HEADS * FOX_HD
D_FF = 5632
CONV_K = 3
EPS = 1e-6
NEG = -1e30

LANE = 128
Q_TILE = 256
ROW_PAD = Q_TILE - N_META
L_BUF = ROW_PAD + N_META + SEQ
ATT_TILE = 768
ROW_BLK = 256
N_AT = L_BUF // ATT_TILE

LOG2E = float(np.log2(np.e))
MLA_SCALE = (MLA_NOPE + MLA_ROPE) ** -0.5 * LOG2E
FOX_SCALE = FOX_HD ** -0.5 * LOG2E

TAIL_W = 4 * LANE
IN_MAIN = MLA_Q_LORA + MLA_KV_LORA + 3 * FOX_W + TAIL_W
IN_PACKED = IN_MAIN + FOX_W
GATE_COL0 = MLA_Q_LORA + MLA_KV_LORA + 2 * FOX_W
Q_HEAD_W = 3 * LANE
QK_W = 2 * LANE

TM_PROJ = 1408
TN_PROJ = 512
TM_OUT = 384
TM_FFN = 768
FC_FFN = 512
HALO = 16
PACK_TILE = 256
PIECE = 64
CVT_STEPS = 88

VMEM_LIMIT = 56 * 1024 * 1024
VMEM_LIMIT_FFN = 58 * 1024 * 1024


def _rms(x, g):
    ms = jnp.mean(x * x, axis=-1, keepdims=True)
    return x * lax.rsqrt(ms + EPS) * g


def _split3(x):
    hi = x.astype(BF16).astype(F32)
    r = x - hi
    mid = r.astype(BF16).astype(F32)
    return hi, mid, r - mid


def _cparams(sem, vmem_limit=VMEM_LIMIT):
    return pltpu.CompilerParams(dimension_semantics=sem, vmem_limit_bytes=vmem_limit)


def _proj_in_kernel(x_ref, g_ref, w_ref, o_ref, ov_ref, xn_ref):
    j = pl.program_id(1)

    @pl.when(j == 0)
    def _():
        def body(r, carry):
            rows = pl.ds(pl.multiple_of(r * 64, 64), 64)
            xn_ref[rows, :] = _rms(x_ref[rows, :], g_ref[...]).astype(BF16)
            return carry
        lax.fori_loop(0, TM_PROJ // 64, body, 0)

    res = jnp.dot(xn_ref[...], w_ref[...], preferred_element_type=F32)

    @pl.when(j < IN_MAIN // TN_PROJ)
    def _():
        o_ref[...] = res

    @pl.when(j >= IN_MAIN // TN_PROJ)
    def _():
        ov_ref[...] = res.astype(BF16)


def _proj_in(l, h, g, w):
    n_main = IN_MAIN // TN_PROJ
    return pl.pallas_call(
        _proj_in_kernel,
        grid=(L_BUF // TM_PROJ, IN_PACKED // TN_PROJ),
        in_specs=[
            pl.BlockSpec((TM_PROJ, D_MODEL), lambda i, j: (i, 0)),
            pl.BlockSpec((1, D_MODEL), lambda i, j: (0, 0)),
            pl.BlockSpec((None, D_MODEL, TN_PROJ), lambda i, j: (l, 0, j)),
        ],
        out_specs=[
            pl.BlockSpec((TM_PROJ, TN_PROJ), lambda i, j: (i, jnp.minimum(j, n_main - 1))),
            pl.BlockSpec((None, TM_PROJ, TN_PROJ), lambda i, j: (0, i, jnp.maximum(j - n_main, 0))),
        ],
        out_shape=[jax.ShapeDtypeStruct((L_BUF, IN_MAIN), F32),
                   jax.ShapeDtypeStruct((1, L_BUF, FOX_W), BF16)],
        scratch_shapes=[pltpu.VMEM((TM_PROJ, D_MODEL), BF16)],
        compiler_params=_cparams(("parallel", "arbitrary")),
        name="proj_in",
    )(h, g, w)


def _prep_kernel(cq_ref, ckv_ref, fq_ref, fk_ref, tail_ref, cos_ref, sin_ref,
                 gql_ref, gkvl_ref, gfq_ref, gfk_ref, bf_ref, wq_ref, wkv_ref, tri_ref,
                 q_ref, k_ref, v_ref, qf_ref, kf_ref, c_ref, carry_ref):
    i = pl.program_id(0)
    lane = lax.broadcasted_iota(jnp.int32, (Q_TILE, LANE), 1)
    is_pad = (i * Q_TILE + lax.broadcasted_iota(jnp.int32, (Q_TILE, 1), 0)) < ROW_PAD
    cos = cos_ref[...]
    sin = sin_ref[...]
    q_flag = jnp.where(lane == MLA_ROPE, 1.0, 0.0)
    k_flag = jnp.where((lane == MLA_ROPE) & is_pad, NEG, 0.0)
    kr = (tail_ref[:, 0:LANE] * cos + tail_ref[:, LANE:2 * LANE] * sin + k_flag).astype(BF16)
    cqn = _rms(cq_ref[...], gql_ref[...]).astype(BF16)
    ckvn = _rms(ckv_ref[...], gkvl_ref[...]).astype(BF16)
    for h in range(MLA_HEADS):
        qh = jnp.dot(cqn, wq_ref[:, Q_HEAD_W * h:Q_HEAD_W * (h + 1)], preferred_element_type=F32)
        qr = qh[:, LANE:2 * LANE] * cos + qh[:, 2 * LANE:3 * LANE] * sin
        q_ref[h, :, 0:LANE] = (qh[:, 0:LANE] * MLA_SCALE).astype(BF16)
        q_ref[h, :, LANE:QK_W] = (qr * MLA_SCALE + q_flag).astype(BF16)
        kvh = jnp.dot(ckvn, wkv_ref[:, 2 * LANE * h:2 * LANE * (h + 1)], preferred_element_type=F32)
        k_ref[h, :, 0:LANE] = kvh[:, 0:LANE].astype(BF16)
        k_ref[h, :, LANE:QK_W] = kr
        v_ref[h] = kvh[:, LANE:2 * LANE].astype(BF16)

    z = tail_ref[:, 2 * LANE:3 * LANE] + bf_ref[...]
    logf = (jnp.minimum(z, 0.0) - jnp.log(1.0 + jnp.exp(-jnp.abs(z)))) * LOG2E
    hi, mid, lo = _split3(logf)
    tri = tri_ref[...]
    cs = (jnp.dot(tri, hi.astype(BF16), preferred_element_type=F32)
          + jnp.dot(tri, mid.astype(BF16), preferred_element_type=F32)
          + jnp.dot(tri, lo.astype(BF16), preferred_element_type=F32))

    @pl.when(i == 0)
    def _():
        carry_ref[...] = jnp.zeros_like(carry_ref)

    c = cs + carry_ref[0:1, :]
    c_ref[...] = c
    carry_ref[...] = jnp.broadcast_to(c[Q_TILE - 1:Q_TILE, :], carry_ref.shape)

    qf_ext = jnp.where(lane < 3, -1.0, 0.0).astype(BF16)
    for h in range(FOX_HEADS):
        cols = slice(FOX_HD * h, FOX_HD * (h + 1))
        qf_ref[h, :, 0:FOX_HD] = (_rms(fq_ref[:, cols], gfq_ref[...]) * FOX_SCALE).astype(BF16)
        qf_ref[h, :, FOX_HD:QK_W] = qf_ext
        kf_ref[h, :, 0:FOX_HD] = _rms(fk_ref[:, cols], gfk_ref[...]).astype(BF16)
        ck = jnp.sum(jnp.where(lane == h, c, 0.0), axis=-1, keepdims=True)
        ck_hi, ck_mid, ck_lo = _split3(jnp.where(is_pad, -NEG, ck))
        ext = jnp.where(lane == 0, ck_hi, jnp.where(lane == 1, ck_mid, jnp.where(lane == 2, ck_lo, 0.0)))
        kf_ref[h, :, FOX_HD:QK_W] = ext.astype(BF16)


def _prep(l, proj, cos_t, sin_t, gql, gkvl, gfq, gfk, bf_pad, wq, wkv, tri):
    tm = Q_TILE
    row = lambda i: (i, 0)
    const = lambda i: (0, 0)
    head_out = lambda w: pl.BlockSpec((MLA_HEADS, tm, w), lambda i: (0, i, 0))
    return pl.pallas_call(
        _prep_kernel,
        grid=(L_BUF // tm,),
        in_specs=[
            pl.BlockSpec((tm, MLA_Q_LORA), lambda i: (i, 0)),
            pl.BlockSpec((tm, MLA_KV_LORA), lambda i: (i, 1)),
            pl.BlockSpec((tm, FOX_W), lambda i: (i, 1)),
            pl.BlockSpec((tm, FOX_W), lambda i: (i, 2)),
            pl.BlockSpec((tm, TAIL_W), lambda i: (i, IN_MAIN // TAIL_W - 1)),
            pl.BlockSpec((tm, LANE), row),
            pl.BlockSpec((tm, LANE), row),
            pl.BlockSpec((1, MLA_Q_LORA), const),
            pl.BlockSpec((1, MLA_KV_LORA), const),
            pl.BlockSpec((1, FOX_HD), const),
            pl.BlockSpec((1, FOX_HD), const),
            pl.BlockSpec((1, LANE), const),
            pl.BlockSpec((None, MLA_Q_LORA, MLA_HEADS * Q_HEAD_W), lambda i: (l, 0, 0)),
            pl.BlockSpec((None, MLA_KV_LORA, MLA_HEADS * 2 * LANE), lambda i: (l, 0, 0)),
            pl.BlockSpec((tm, tm), const),
        ],
        out_specs=[
            head_out(QK_W), head_out(QK_W), head_out(MLA_V),
            head_out(QK_W), head_out(QK_W),
            pl.BlockSpec((tm, LANE), row),
        ],
        out_shape=[
            jax.ShapeDtypeStruct((MLA_HEADS, L_BUF, QK_W), BF16),
            jax.ShapeDtypeStruct((MLA_HEADS, L_BUF, QK_W), BF16),
            jax.ShapeDtypeStruct((MLA_HEADS, L_BUF, MLA_V), BF16),
            jax.ShapeDtypeStruct((FOX_HEADS, L_BUF, QK_W), BF16),
            jax.ShapeDtypeStruct((FOX_HEADS, L_BUF, QK_W), BF16),
            jax.ShapeDtypeStruct((L_BUF, LANE), F32),
        ],
        scratch_shapes=[pltpu.VMEM((8, LANE), F32)],
        compiler_params=_cparams(("arbitrary",)),
        name="attn_prep",
    )(proj, proj, proj, proj, proj, cos_t, sin_t, gql, gkvl, gfq, gfk, bf_pad, wq, wkv, tri)


def _scores(q, k):
    return lax.dot_general(q, k, (((1,), (1,)), ((), ())), preferred_element_type=F32)


def _flash_tile(i, q_ref, k_ref, v_ref, m_sc, acc_sc, p_sc, alpha_sc, p_new, alpha_new,
                diag_mask, row_shift=None):
    m_sc[...] = jnp.full_like(m_sc, NEG)
    acc_sc[...] = jnp.zeros_like(acc_sc)
    row_blocks = [slice(rb * ROW_BLK, (rb + 1) * ROW_BLK) for rb in range(ATT_TILE // ROW_BLK)]

    def chunk_start(j):
        return pl.multiple_of(j * ATT_TILE, ATT_TILE)

    def flush(j):
        v = v_ref[0, pl.ds(chunk_start(j), ATT_TILE), :]
        v_ones = jnp.concatenate([v, jnp.ones_like(v)], axis=1)
        for rows in row_blocks:
            pv = acc_sc[rows, :] * jnp.tile(alpha_sc[rows, :], (1, 2))
            for kc in range(0, ATT_TILE, 2 * LANE):
                pv = pv + jnp.dot(p_sc[rows, kc:kc + 2 * LANE], v_ones[kc:kc + 2 * LANE, :],
                                  preferred_element_type=F32)
            acc_sc[rows, :] = pv

    def scores(j, rows, ncols):
        return _scores(q_ref[0, rows, :], k_ref[0, pl.ds(chunk_start(j), ncols), :])

    def softmax_rows(rows, s, p_dst, alpha_dst):
        ncols = s.shape[1]
        m_prev = m_sc[rows, :]
        m_chunk = jnp.max(s, axis=-1, keepdims=True)
        if row_shift is not None:
            m_chunk = m_chunk + row_shift[rows, :]
        m_new = jnp.maximum(m_prev, m_chunk)
        alpha_dst[rows, :] = jnp.exp2(m_prev - m_new)
        sub = m_new if row_shift is None else m_new - row_shift[rows, :]
        p_dst[rows, 0:ncols] = jnp.exp2(s - jnp.tile(sub, (1, ncols // LANE))).astype(BF16)
        m_sc[rows, :] = m_new

    def make_pending():
        p_sc[...] = p_new[...]
        alpha_sc[...] = alpha_new[...]

    for rows in row_blocks:
        ncols = rows.stop
        s = scores(i, rows, ncols)
        r = lax.broadcasted_iota(jnp.int32, s.shape, 0) + rows.start
        c = lax.broadcasted_iota(jnp.int32, s.shape, 1)
        if ncols < ATT_TILE:
            p_sc[rows, ncols:] = jnp.zeros((ROW_BLK, ATT_TILE - ncols), BF16)
        softmax_rows(rows, jnp.where(diag_mask(r, c), s, NEG), p_sc, alpha_sc)

    def body(j, pending):
        for rows in row_blocks:
            softmax_rows(rows, scores(j, rows, ATT_TILE), p_new, alpha_new)
        flush(pending)
        make_pending()
        return j
    flush(lax.fori_loop(0, i, body, i))

    return acc_sc[:, 0:LANE] / acc_sc[:, LANE:2 * LANE]


_ATT_SCRATCH = [pltpu.VMEM((ATT_TILE, LANE), F32), pltpu.VMEM((ATT_TILE, 2 * LANE), F32),
                pltpu.VMEM((ATT_TILE, ATT_TILE), BF16), pltpu.VMEM((ATT_TILE, LANE), F32),
                pltpu.VMEM((ATT_TILE, ATT_TILE), BF16), pltpu.VMEM((ATT_TILE, LANE), F32)]


def _mla_attn_kernel(q_ref, k_ref, v_ref, o_ref, *scratch):
    mask = lambda r, c: jnp.right_shift(c, CHUNK_SHIFT) <= jnp.right_shift(r, CHUNK_SHIFT)
    out = _flash_tile(pl.program_id(1), q_ref, k_ref, v_ref, *scratch, diag_mask=mask)
    o_ref[...] = out.astype(o_ref.dtype)


def _mla_attn(q, k, v):
    return pl.pallas_call(
        _mla_attn_kernel,
        grid=(MLA_HEADS, N_AT),
        in_specs=[
            pl.BlockSpec((1, ATT_TILE, QK_W), lambda h, i: (h, i, 0)),
            pl.BlockSpec((1, L_BUF, QK_W), lambda h, i: (h, 0, 0)),
            pl.BlockSpec((1, L_BUF, MLA_V), lambda h, i: (h, 0, 0)),
        ],
        out_specs=pl.BlockSpec((ATT_TILE, MLA_V), lambda h, i: (i, h)),
        out_shape=jax.ShapeDtypeStruct((L_BUF, MLA_HEADS * MLA_V), BF16),
        scratch_shapes=_ATT_SCRATCH,
        compiler_params=_cparams(("parallel", "arbitrary")),
        name="mla_attn",
    )(q, k, v)


def _fox_attn_kernel(q_ref, k_ref, v_ref, c_ref, gate_ref, o_ref, *scratch):
    lane = lax.broadcasted_iota(jnp.int32, (ATT_TILE, LANE), 1)
    cq = jnp.sum(jnp.where(lane == pl.program_id(0), c_ref[...], 0.0), axis=-1, keepdims=True)
    out = _flash_tile(pl.program_id(1), q_ref, k_ref, v_ref, *scratch,
                      diag_mask=lambda r, c: c <= r, row_shift=cq)
    o_ref[...] = (out * jax.nn.sigmoid(gate_ref[...])).astype(o_ref.dtype)


def _fox_attn(q, k, v, c, proj):
    return pl.pallas_call(
        _fox_attn_kernel,
        grid=(FOX_HEADS, N_AT),
        in_specs=[
            pl.BlockSpec((1, ATT_TILE, QK_W), lambda h, i: (h, i, 0)),
            pl.BlockSpec((1, L_BUF, QK_W), lambda h, i: (h, 0, 0)),
            pl.BlockSpec((1, L_BUF, FOX_HD), lambda h, i: (0, 0, h)),
            pl.BlockSpec((ATT_TILE, LANE), lambda h, i: (i, 0)),
            pl.BlockSpec((ATT_TILE, FOX_HD), lambda h, i: (i, GATE_COL0 // FOX_HD + h)),
        ],
        out_specs=pl.BlockSpec((ATT_TILE, FOX_HD), lambda h, i: (i, h)),
        out_shape=jax.ShapeDtypeStruct((L_BUF, FOX_W), BF16),
        scratch_shapes=_ATT_SCRATCH,
        compiler_params=_cparams(("parallel", "arbitrary")),
        name="fox_attn",
    )(q, k, v, c, proj)


def _mix_out_kernel(a_ref, b_ref, wa_ref, wb_ref, h_ref, g_ref, o_ref):
    mix = (jnp.dot(a_ref[...], wa_ref[...], preferred_element_type=F32)
           + jnp.dot(b_ref[...], wb_ref[...], preferred_element_type=F32))
    out = h_ref[...] + _rms(mix, g_ref[...])
    row = pl.program_id(0) * TM_OUT + lax.broadcasted_iota(jnp.int32, (TM_OUT, 1), 0)
    o_ref[...] = jnp.where(row >= ROW_PAD, out, 0.0)


def _mix_out(l, a, b, w_o, h, g):
    return pl.pallas_call(
        _mix_out_kernel,
        grid=(L_BUF // TM_OUT,),
        in_specs=[
            pl.BlockSpec((TM_OUT, MLA_HEADS * MLA_V), lambda i: (i, 0)),
            pl.BlockSpec((TM_OUT, FOX_W), lambda i: (i, 0)),
            pl.BlockSpec((None, MLA_HEADS * MLA_V, D_MODEL), lambda i: (l, 0, 0)),
            pl.BlockSpec((None, FOX_W, D_MODEL), lambda i: (l, 1, 0)),
            pl.BlockSpec((TM_OUT, D_MODEL), lambda i: (i, 0)),
            pl.BlockSpec((1, D_MODEL), lambda i: (0, 0)),
        ],
        out_specs=pl.BlockSpec((TM_OUT, D_MODEL), lambda i: (i, 0)),
        out_shape=jax.ShapeDtypeStruct((L_BUF, D_MODEL), F32),
        compiler_params=_cparams(("parallel",)),
        name="mix_out",
    )(a, b, w_o, w_o, h, g)


def _gelu_tanh(x):
    return 0.5 * x * (1.0 + jnp.tanh(np.sqrt(2.0 / np.pi).astype(np.float32) * (x + 0.044715 * (x * x * x))))


def _ffn_kernel(h_ref, halo_ref, gpre_ref, wg_ref, wu_ref, cwg_ref, cwu_ref, cbg_ref, cbu_ref,
                wd_ref, gpost_ref, *rest):
    c = pl.program_id(1)
    if len(rest) == 4:
        o_ref, xn_ref, ug_ref, uu_ref = rest
    else:
        nu_ref, nd_ref, o_ref, nu_out, nd_out, xn_ref, ug_ref, uu_ref = rest

        @pl.when(pl.program_id(0) * pl.num_programs(1) + c < CVT_STEPS)
        def _():
            nu_out[...] = nu_ref[...].astype(BF16)
            nd_out[...] = nd_ref[...].astype(BF16)

    @pl.when(c == 0)
    def _():
        xn_ref[0:HALO, :] = _rms(halo_ref[...], gpre_ref[...]).astype(BF16)

        def body(r, carry):
            src = pl.ds(pl.multiple_of(r * 64, 64), 64)
            dst = pl.ds(pl.multiple_of(HALO + r * 64, 16), 64)
            xn_ref[dst, :] = _rms(h_ref[src, :], gpre_ref[...]).astype(BF16)
            return carry
        lax.fori_loop(0, TM_FFN // 64, body, 0)
        o_ref[...] = jnp.zeros_like(o_ref)

    xn = xn_ref[...]
    ug_ref[...] = jnp.dot(xn, wg_ref[...], preferred_element_type=F32)
    uu_ref[...] = jnp.dot(xn, wu_ref[...], preferred_element_type=F32)

    def conv(u_ref, w_ref, b_ref):
        acc = b_ref[...] + w_ref[CONV_K - 1:CONV_K, :] * u_ref[HALO:HALO + TM_FFN, :]
        for t in range(1, CONV_K):
            acc = acc + w_ref[CONV_K - 1 - t:CONV_K - t, :] * u_ref[HALO - t:HALO - t + TM_FFN, :]
        return acc

    act = _gelu_tanh(conv(ug_ref, cwg_ref, cbg_ref)) * conv(uu_ref, cwu_ref, cbu_ref)
    o_ref[...] += jnp.dot(act.astype(BF16), wd_ref[...], preferred_element_type=F32)

    @pl.when(c == pl.num_programs(1) - 1)
    def _():
        out = h_ref[...] + _rms(o_ref[...], gpost_ref[...])
        row = pl.program_id(0) * TM_FFN + lax.broadcasted_iota(jnp.int32, (TM_FFN, 1), 0)
        o_ref[...] = jnp.where(row >= ROW_PAD, out, 0.0)


def _ffn(h, gpre, w_up, w_conv, b_conv, w_down, gpost, next_weights=None):
    n_fc = D_FF // FC_FFN
    halo_blocks = TM_FFN // HALO
    in_specs = [
        pl.BlockSpec((TM_FFN, D_MODEL), lambda i, c: (i, 0)),
        pl.BlockSpec((HALO, D_MODEL), lambda i, c: (jnp.maximum(i * halo_blocks - 1, 0), 0)),
        pl.BlockSpec((1, D_MODEL), lambda i, c: (0, 0)),
        pl.BlockSpec((D_MODEL, FC_FFN), lambda i, c: (0, c)),
        pl.BlockSpec((D_MODEL, FC_FFN), lambda i, c: (0, n_fc + c)),
        pl.BlockSpec((CONV_K, FC_FFN), lambda i, c: (0, c)),
        pl.BlockSpec((CONV_K, FC_FFN), lambda i, c: (0, n_fc + c)),
        pl.BlockSpec((1, FC_FFN), lambda i, c: (0, c)),
        pl.BlockSpec((1, FC_FFN), lambda i, c: (0, n_fc + c)),
        pl.BlockSpec((FC_FFN, D_MODEL), lambda i, c: (c, 0)),
        pl.BlockSpec((1, D_MODEL), lambda i, c: (0, 0)),
    ]
    operands = [h, h, gpre, w_up, w_up, w_conv, w_conv, b_conv, b_conv, w_down, gpost]
    out_specs = [pl.BlockSpec((TM_FFN, D_MODEL), lambda i, c: (i, 0))]
    out_shape = [jax.ShapeDtypeStruct((L_BUF, D_MODEL), F32)]
    if next_weights is not None:
        w_up_f32, w_down_f32, nl = next_weights
        blk = lambda i, c: jnp.minimum(i * n_fc + c, CVT_STEPS - 1)
        up_cols, down_rows = 2 * D_FF // CVT_STEPS, D_FF // CVT_STEPS
        in_specs += [pl.BlockSpec((None, D_MODEL, up_cols), lambda i, c: (nl, 0, blk(i, c))),
                     pl.BlockSpec((None, down_rows, D_MODEL), lambda i, c: (nl, blk(i, c), 0))]
        operands += [w_up_f32, w_down_f32]
        out_specs += [pl.BlockSpec((D_MODEL, up_cols), lambda i, c: (0, blk(i, c))),
                      pl.BlockSpec((down_rows, D_MODEL), lambda i, c: (blk(i, c), 0))]
        out_shape += [jax.ShapeDtypeStruct((D_MODEL, 2 * D_FF), BF16),
                      jax.ShapeDtypeStruct((D_FF, D_MODEL), BF16)]
    return pl.pallas_call(
        _ffn_kernel,
        grid=(L_BUF // TM_FFN, n_fc),
        in_specs=in_specs,
        out_specs=out_specs,
        out_shape=out_shape,
        scratch_shapes=[pltpu.VMEM((HALO + TM_FFN, D_MODEL), BF16),
                        pltpu.VMEM((HALO + TM_FFN, FC_FFN), F32),
                        pltpu.VMEM((HALO + TM_FFN, FC_FFN), F32)],
        compiler_params=_cparams(("arbitrary", "arbitrary"), VMEM_LIMIT_FFN),
        name="conv_ffn",
    )(*operands)


def _rotate_half_cols(w):
    half = w.shape[-1] // 2
    return jnp.concatenate([-w[..., half:], w[..., :half]], axis=-1)


def _pad_cols(w, width):
    return jnp.pad(w, [(0, 0)] * (w.ndim - 1) + [(0, width - w.shape[-1])])


def _pack_sources():
    o = np.cumsum([0, MLA_Q_LORA, MLA_KV_LORA, MLA_ROPE, FOX_W, FOX_W, FOX_W, FOX_W, FOX_HEADS])
    c_q, c_kv, k_rope, fq, fk, fv, fg, ff = [(int(o[n]), int(o[n + 1] - o[n])) for n in range(8)]
    per_tile = PACK_TILE // PIECE
    tiles, tail_tile = [], None
    for start, width in (c_q, c_kv, fq, fk, fg, (None, TAIL_W), fv):
        if start is None:
            tail_tile = len(tiles)
            tiles += [[k_rope[0] // PIECE] * per_tile, [ff[0] // PIECE] * per_tile]
            continue
        assert start % PIECE == 0 and width % PACK_TILE == 0, (start, width)
        tiles += [[(start + t * PACK_TILE) // PIECE + q for q in range(per_tile)]
                  for t in range(width // PACK_TILE)]
    assert len(tiles) * PACK_TILE == IN_PACKED and MLA_ROPE == PIECE and TAIL_W == 2 * PACK_TILE
    return tiles, tail_tile


_PACK_SRC, _PACK_TAIL = _pack_sources()


def _pack_w_in_kernel(a_ref, b_ref, c_ref, d_ref, o_ref):
    j = pl.program_id(1)
    zeros = lambda n: jnp.zeros((n, D_MODEL), F32)

    def emit(pieces):
        o_ref[...] = jnp.concatenate(pieces, axis=0).T.astype(BF16)

    @pl.when(j == _PACK_TAIL)
    def _():
        half = MLA_ROPE // 2
        emit([a_ref[...], zeros(LANE - MLA_ROPE), -a_ref[half:, :], a_ref[:half, :], zeros(LANE - MLA_ROPE)])

    @pl.when(j == _PACK_TAIL + 1)
    def _():
        emit([a_ref[:FOX_HEADS, :], zeros(PACK_TILE - FOX_HEADS)])

    @pl.when((j != _PACK_TAIL) & (j != _PACK_TAIL + 1))
    def _():
        emit([a_ref[...], b_ref[...], c_ref[...], d_ref[...]])


def _pack_w_in(w):
    def piece_spec(q):
        def index(l, j):
            blk = jnp.int32(_PACK_SRC[0][q])
            for t in range(1, len(_PACK_SRC)):
                blk = jnp.where(j == t, _PACK_SRC[t][q], blk)
            return l, blk, 0
        return pl.BlockSpec((None, PIECE, D_MODEL), index)

    wt = jnp.swapaxes(w, 1, 2)
    n_pieces = PACK_TILE // PIECE
    return pl.pallas_call(
        _pack_w_in_kernel,
        grid=(DEPTH, IN_PACKED // PACK_TILE),
        in_specs=[piece_spec(q) for q in range(n_pieces)],
        out_specs=pl.BlockSpec((None, D_MODEL, PACK_TILE), lambda l, j: (l, 0, j)),
        out_shape=jax.ShapeDtypeStruct((DEPTH, D_MODEL, IN_PACKED), BF16),
        compiler_params=_cparams(("parallel", "arbitrary")),
        name="pack_w_in",
    )(*([wt] * n_pieces))


def _pack_w_q_up(w):
    w = w.astype(BF16).reshape(DEPTH, MLA_Q_LORA, MLA_HEADS, MLA_NOPE + MLA_ROPE)
    nope, rope = w[..., :MLA_NOPE], w[..., MLA_NOPE:]
    packed = jnp.concatenate([nope, _pad_cols(rope, LANE), _pad_cols(_rotate_half_cols(rope), LANE)], axis=-1)
    return packed.reshape(DEPTH, MLA_Q_LORA, MLA_HEADS * Q_HEAD_W)


def _rope_tables():
    pos = jnp.maximum(jnp.arange(L_BUF, dtype=jnp.int32) - ROW_PAD, 0).astype(F32)
    half = MLA_ROPE // 2
    inv_freq = ROPE_THETA ** (-jnp.arange(half, dtype=F32) / half)
    ang = pos[:, None] * inv_freq[None, :]
    zeros = jnp.zeros((L_BUF, LANE - MLA_ROPE), F32)
    cos, sin = jnp.cos(ang), jnp.sin(ang)
    return (jnp.concatenate([cos, cos, zeros], axis=1), jnp.concatenate([sin, sin, zeros], axis=1))


def kernel(x, meta_tokens, ln_mix_pre, w_in, b_forget, g_q_latent, g_kv_latent, w_q_up, w_kv_up,
           g_fox_q, g_fox_k, w_out, ln_mix_post, ln_ffn_pre, w_ffn_up, w_ffn_conv, b_ffn_conv,
           w_ffn_down, ln_ffn_post):
    assert x.shape == (1, SEQ, D_MODEL), x.shape
    h = jnp.concatenate([jnp.zeros((ROW_PAD, D_MODEL), x.dtype), meta_tokens.astype(x.dtype), x[0]], axis=0)
    cos_t, sin_t = _rope_tables()
    tri = (lax.broadcasted_iota(jnp.int32, (Q_TILE, Q_TILE), 0)
           >= lax.broadcasted_iota(jnp.int32, (Q_TILE, Q_TILE), 1)).astype(BF16)
    row2d = lambda v: v.reshape(1, -1).astype(F32)
    w_in_p, w_q_p = _pack_w_in(w_in), _pack_w_q_up(w_q_up)
    w_kv_b, w_o_b = w_kv_up.astype(BF16), w_out.astype(BF16)
    w_up_b, w_down_b = w_ffn_up[0].astype(BF16), w_ffn_down[0].astype(BF16)
    assert MLA_HEADS * MLA_V == FOX_W

    for l in range(DEPTH):
        proj, vf = _proj_in(l, h, row2d(ln_mix_pre[l]), w_in_p)
        q, k, v, qf, kf, c = _prep(
            l, proj, cos_t, sin_t, row2d(g_q_latent[l]), row2d(g_kv_latent[l]),
            row2d(g_fox_q[l]), row2d(g_fox_k[l]), _pad_cols(row2d(b_forget[l]), LANE),
            w_q_p, w_kv_b, tri)
        a = _mla_attn(q, k, v)
        b = _fox_attn(qf, kf, vf, c, proj)
        h = _mix_out(l, a, b, w_o_b, h, row2d(ln_mix_post[l]))
        next_weights = (w_ffn_up, w_ffn_down, l + 1) if l + 1 < DEPTH else None
        h, *converted = _ffn(h, row2d(ln_ffn_pre[l]), w_up_b, w_ffn_conv[l].astype(F32),
                             row2d(b_ffn_conv[l]), w_down_b, row2d(ln_ffn_post[l]), next_weights)
        if converted:
            w_up_b, w_down_b = converted

    return h[ROW_PAD + N_META:][None]
```

```python
import jax
import jax.numpy as jnp
import numpy as np
from jax import lax
from jax.experimental import pallas as pl
from jax.experimental.pallas import tpu as pltpu

F32 = jnp.float32
BF16 = jnp.bfloat16

D_MODEL = 2048
SEQ = 8192
DEPTH = 4
CHUNK = 64
CHUNK_SHIFT = 6
N_META = 16
MLA_HEADS = 8
MLA_Q_LORA = 512
MLA_KV_LORA = 512
MLA_NOPE = 128
MLA_ROPE = 64
MLA_V = 128
ROPE_THETA = 10000.0
FOX_HEADS = 8
FOX_HD = 128
FOX_W = FOX_HEADS * FOX_HD
D_FF = 5632
CONV_K = 3
EPS = 1e-6
NEG = -1e30

LANE = 128
Q_TILE = 256
ROW_PAD = Q_TILE - N_META
L_BUF = ROW_PAD + N_META + SEQ
ATT_TILE = 768
ROW_BLK = 256
N_AT = L_BUF // ATT_TILE

LOG2E = float(np.log2(np.e))
MLA_SCALE = (MLA_NOPE + MLA_ROPE) ** -0.5 * LOG2E
FOX_SCALE = FOX_HD ** -0.5 * LOG2E

TAIL_W = 4 * LANE
IN_MAIN = MLA_Q_LORA + MLA_KV_LORA + 3 * FOX_W + TAIL_W
IN_PACKED = IN_MAIN + FOX_W
GATE_COL0 = MLA_Q_LORA + MLA_KV_LORA + 2 * FOX_W
Q_HEAD_W = 3 * LANE
QK_W = 2 * LANE

TM_PROJ = 1408
TN_PROJ = 512
TM_OUT = 384
TM_FFN = 768
FC_FFN = 512
HALO = 16
PACK_TILE = 256
PIECE = 64
CVT_STEPS = 88
CVT_STEPS_PROJ = 44

VMEM_LIMIT = 56 * 1024 * 1024
VMEM_LIMIT_FFN = 58 * 1024 * 1024


def _rms(x, g):
    ms = jnp.mean(x * x, axis=-1, keepdims=True)
    return x * lax.rsqrt(ms + EPS) * g


def _split3(x):
    hi = x.astype(BF16).astype(F32)
    r = x - hi
    mid = r.astype(BF16).astype(F32)
    return hi, mid, r - mid


def _cparams(sem, vmem_limit=VMEM_LIMIT):
    return pltpu.CompilerParams(dimension_semantics=sem, vmem_limit_bytes=vmem_limit)


def _proj_in_kernel(x_ref, g_ref, w_ref, *rest):
    j = pl.program_id(1)
    if len(rest) == 3:
        o_ref, ov_ref, xn_ref = rest
    else:
        nu_ref, nd_ref, o_ref, ov_ref, nu_out, nd_out, xn_ref = rest

        @pl.when(pl.program_id(0) * pl.num_programs(1) + j < CVT_STEPS_PROJ)
        def _():
            nu_out[...] = nu_ref[...].astype(BF16)
            nd_out[...] = nd_ref[...].astype(BF16)

    @pl.when(j == 0)
    def _():
        def body(r, carry):
            rows = pl.ds(pl.multiple_of(r * 64, 64), 64)
            xn_ref[rows, :] = _rms(x_ref[rows, :], g_ref[...]).astype(BF16)
            return carry
        lax.fori_loop(0, TM_PROJ // 64, body, 0)

    res = jnp.dot(xn_ref[...], w_ref[...], preferred_element_type=F32)

    @pl.when(j < IN_MAIN // TN_PROJ)
    def _():
        o_ref[...] = res

    @pl.when(j >= IN_MAIN // TN_PROJ)
    def _():
        ov_ref[...] = res.astype(BF16)


def _proj_in(l, h, g, w, ffn_weights=None):
    n_main = IN_MAIN // TN_PROJ
    n_j = IN_PACKED // TN_PROJ
    in_specs = [
        pl.BlockSpec((TM_PROJ, D_MODEL), lambda i, j: (i, 0)),
        pl.BlockSpec((1, D_MODEL), lambda i, j: (0, 0)),
        pl.BlockSpec((None, D_MODEL, TN_PROJ), lambda i, j: (l, 0, j)),
    ]
    operands = [h, g, w]
    out_specs = [
        pl.BlockSpec((TM_PROJ, TN_PROJ), lambda i, j: (i, jnp.minimum(j, n_main - 1))),
        pl.BlockSpec((None, TM_PROJ, TN_PROJ), lambda i, j: (0, i, jnp.maximum(j - n_main, 0))),
    ]
    out_shape = [jax.ShapeDtypeStruct((L_BUF, IN_MAIN), F32),
                 jax.ShapeDtypeStruct((1, L_BUF, FOX_W), BF16)]
    if ffn_weights is not None:
        w_up_f32, w_down_f32, fl = ffn_weights
        blk = lambda i, j: jnp.minimum(i * n_j + j, CVT_STEPS_PROJ - 1)
        up_cols, down_rows = 2 * D_FF // CVT_STEPS_PROJ, D_FF // CVT_STEPS_PROJ
        in_specs += [pl.BlockSpec((None, D_MODEL, up_cols), lambda i, j: (fl, 0, blk(i, j))),
                     pl.BlockSpec((None, down_rows, D_MODEL), lambda i, j: (fl, blk(i, j), 0))]
        operands += [w_up_f32, w_down_f32]
        out_specs += [pl.BlockSpec((D_MODEL, up_cols), lambda i, j: (0, blk(i, j))),
                      pl.BlockSpec((down_rows, D_MODEL), lambda i, j: (blk(i, j), 0))]
        out_shape += [jax.ShapeDtypeStruct((D_MODEL, 2 * D_FF), BF16),
                      jax.ShapeDtypeStruct((D_FF, D_MODEL), BF16)]
    semantics = ("parallel", "arbitrary") if ffn_weights is None else ("arbitrary", "arbitrary")
    return pl.pallas_call(
        _proj_in_kernel,
        grid=(L_BUF // TM_PROJ, n_j),
        in_specs=in_specs,
        out_specs=out_specs,
        out_shape=out_shape,
        scratch_shapes=[pltpu.VMEM((TM_PROJ, D_MODEL), BF16)],
        compiler_params=_cparams(semantics),
        name="proj_in",
    )(*operands)


def _prep_kernel(cq_ref, ckv_ref, fq_ref, fk_ref, tail_ref, cos_ref, sin_ref,
                 gql_ref, gkvl_ref, gfq_ref, gfk_ref, bf_ref, wq_ref, wkv_ref, tri_ref,
                 q_ref, k_ref, v_ref, qf_ref, kf_ref, c_ref, carry_ref):
    i = pl.program_id(0)
    lane = lax.broadcasted_iota(jnp.int32, (Q_TILE, LANE), 1)
    is_pad = (i * Q_TILE + lax.broadcasted_iota(jnp.int32, (Q_TILE, 1), 0)) < ROW_PAD
    cos = cos_ref[...]
    sin = sin_ref[...]
    q_flag = jnp.where(lane == MLA_ROPE, 1.0, 0.0)
    k_flag = jnp.where((lane == MLA_ROPE) & is_pad, NEG, 0.0)
    kr = (tail_ref[:, 0:LANE] * cos + tail_ref[:, LANE:2 * LANE] * sin + k_flag).astype(BF16)
    cqn = _rms(cq_ref[...], gql_ref[...]).astype(BF16)
    ckvn = _rms(ckv_ref[...], gkvl_ref[...]).astype(BF16)
    for h in range(MLA_HEADS):
        qh = jnp.dot(cqn, wq_ref[:, Q_HEAD_W * h:Q_HEAD_W * (h + 1)], preferred_element_type=F32)
        qr = qh[:, LANE:2 * LANE] * cos + qh[:, 2 * LANE:3 * LANE] * sin
        q_ref[h, :, 0:LANE] = (qh[:, 0:LANE] * MLA_SCALE).astype(BF16)
        q_ref[h, :, LANE:QK_W] = (qr * MLA_SCALE + q_flag).astype(BF16)
        kvh = jnp.dot(ckvn, wkv_ref[:, 2 * LANE * h:2 * LANE * (h + 1)], preferred_element_type=F32)
        k_ref[h, :, 0:LANE] = kvh[:, 0:LANE].astype(BF16)
        k_ref[h, :, LANE:QK_W] = kr
        v_ref[h] = kvh[:, LANE:2 * LANE].astype(BF16)

    z = tail_ref[:, 2 * LANE:3 * LANE] + bf_ref[...]
    logf = (jnp.minimum(z, 0.0) - jnp.log(1.0 + jnp.exp(-jnp.abs(z)))) * LOG2E
    hi, mid, lo = _split3(logf)
    tri = tri_ref[...]
    cs = (jnp.dot(tri, hi.astype(BF16), preferred_element_type=F32)
          + jnp.dot(tri, mid.astype(BF16), preferred_element_type=F32)
          + jnp.dot(tri, lo.astype(BF16), preferred_element_type=F32))

    @pl.when(i == 0)
    def _():
        carry_ref[...] = jnp.zeros_like(carry_ref)

    c = cs + carry_ref[0:1, :]
    c_ref[...] = c
    carry_ref[...] = jnp.broadcast_to(c[Q_TILE - 1:Q_TILE, :], carry_ref.shape)

    qf_ext = jnp.where(lane < 3, -1.0, 0.0).astype(BF16)
    for h in range(FOX_HEADS):
        cols = slice(FOX_HD * h, FOX_HD * (h + 1))
        qf_ref[h, :, 0:FOX_HD] = (_rms(fq_ref[:, cols], gfq_ref[...]) * FOX_SCALE).astype(BF16)
        qf_ref[h, :, FOX_HD:QK_W] = qf_ext
        kf_ref[h, :, 0:FOX_HD] = _rms(fk_ref[:, cols], gfk_ref[...]).astype(BF16)
        ck = jnp.sum(jnp.where(lane == h, c, 0.0), axis=-1, keepdims=True)
        ck_hi, ck_mid, ck_lo = _split3(jnp.where(is_pad, -NEG, ck))
        ext = jnp.where(lane == 0, ck_hi, jnp.where(lane == 1, ck_mid, jnp.where(lane == 2, ck_lo, 0.0)))
        kf_ref[h, :, FOX_HD:QK_W] = ext.astype(BF16)


def _prep(l, proj, cos_t, sin_t, gql, gkvl, gfq, gfk, bf_pad, wq, wkv, tri):
    tm = Q_TILE
    row = lambda i: (i, 0)
    const = lambda i: (0, 0)
    head_out = lambda w: pl.BlockSpec((MLA_HEADS, tm, w), lambda i: (0, i, 0))
    return pl.pallas_call(
        _prep_kernel,
        grid=(L_BUF // tm,),
        in_specs=[
            pl.BlockSpec((tm, MLA_Q_LORA), lambda i: (i, 0)),
            pl.BlockSpec((tm, MLA_KV_LORA), lambda i: (i, 1)),
            pl.BlockSpec((tm, FOX_W), lambda i: (i, 1)),
            pl.BlockSpec((tm, FOX_W), lambda i: (i, 2)),
            pl.BlockSpec((tm, TAIL_W), lambda i: (i, IN_MAIN // TAIL_W - 1)),
            pl.BlockSpec((tm, LANE), row),
            pl.BlockSpec((tm, LANE), row),
            pl.BlockSpec((1, MLA_Q_LORA), const),
            pl.BlockSpec((1, MLA_KV_LORA), const),
            pl.BlockSpec((1, FOX_HD), const),
            pl.BlockSpec((1, FOX_HD), const),
            pl.BlockSpec((1, LANE), const),
            pl.BlockSpec((None, MLA_Q_LORA, MLA_HEADS * Q_HEAD_W), lambda i: (l, 0, 0)),
            pl.BlockSpec((None, MLA_KV_LORA, MLA_HEADS * 2 * LANE), lambda i: (l, 0, 0)),
            pl.BlockSpec((tm, tm), const),
        ],
        out_specs=[
            head_out(QK_W), head_out(QK_W), head_out(MLA_V),
            head_out(QK_W), head_out(QK_W),
            pl.BlockSpec((tm, LANE), row),
        ],
        out_shape=[
            jax.ShapeDtypeStruct((MLA_HEADS, L_BUF, QK_W), BF16),
            jax.ShapeDtypeStruct((MLA_HEADS, L_BUF, QK_W), BF16),
            jax.ShapeDtypeStruct((MLA_HEADS, L_BUF, MLA_V), BF16),
            jax.ShapeDtypeStruct((FOX_HEADS, L_BUF, QK_W), BF16),
            jax.ShapeDtypeStruct((FOX_HEADS, L_BUF, QK_W), BF16),
            jax.ShapeDtypeStruct((L_BUF, LANE), F32),
        ],
        scratch_shapes=[pltpu.VMEM((8, LANE), F32)],
        compiler_params=_cparams(("arbitrary",)),
        name="attn_prep",
    )(proj, proj, proj, proj, proj, cos_t, sin_t, gql, gkvl, gfq, gfk, bf_pad, wq, wkv, tri)


def _scores(q, k):
    return lax.dot_general(q, k, (((1,), (1,)), ((), ())), preferred_element_type=F32)


def _flash_tile(i, q_ref, k_ref, v_ref, m_sc, acc_sc, p_sc, alpha_sc, p_new, alpha_new,
                diag_mask, row_shift=None):
    m_sc[...] = jnp.full_like(m_sc, NEG)
    acc_sc[...] = jnp.zeros_like(acc_sc)
    row_blocks = [slice(rb * ROW_BLK, (rb + 1) * ROW_BLK) for rb in range(ATT_TILE // ROW_BLK)]

    def chunk_start(j):
        return pl.multiple_of(j * ATT_TILE, ATT_TILE)

    def flush(j):
        v = v_ref[0, pl.ds(chunk_start(j), ATT_TILE), :]
        v_ones = jnp.concatenate([v, jnp.ones_like(v)], axis=1)
        for rows in row_blocks:
            pv = acc_sc[rows, :] * jnp.tile(alpha_sc[rows, :], (1, 2))
            for kc in range(0, ATT_TILE, 2 * LANE):
                pv = pv + jnp.dot(p_sc[rows, kc:kc + 2 * LANE], v_ones[kc:kc + 2 * LANE, :],
                                  preferred_element_type=F32)
            acc_sc[rows, :] = pv

    def scores(j, rows, ncols):
        return _scores(q_ref[0, rows, :], k_ref[0, pl.ds(chunk_start(j), ncols), :])

    def softmax_rows(rows, s, p_dst, alpha_dst):
        ncols = s.shape[1]
        m_prev = m_sc[rows, :]
        m_chunk = jnp.max(s, axis=-1, keepdims=True)
        if row_shift is not None:
            m_chunk = m_chunk + row_shift[rows, :]
        m_new = jnp.maximum(m_prev, m_chunk)
        alpha_dst[rows, :] = jnp.exp2(m_prev - m_new)
        sub = m_new if row_shift is None else m_new - row_shift[rows, :]
        p_dst[rows, 0:ncols] = jnp.exp2(s - jnp.tile(sub, (1, ncols // LANE))).astype(BF16)
        m_sc[rows, :] = m_new

    def make_pending():
        p_sc[...] = p_new[...]
        alpha_sc[...] = alpha_new[...]

    for rows in row_blocks:
        ncols = rows.stop
        s = scores(i, rows, ncols)
        r = lax.broadcasted_iota(jnp.int32, s.shape, 0) + rows.start
        c = lax.broadcasted_iota(jnp.int32, s.shape, 1)
        if ncols < ATT_TILE:
            p_sc[rows, ncols:] = jnp.zeros((ROW_BLK, ATT_TILE - ncols), BF16)
        softmax_rows(rows, jnp.where(diag_mask(r, c), s, NEG), p_sc, alpha_sc)

    def body(j, pending):
        for rows in row_blocks:
            softmax_rows(rows, scores(j, rows, ATT_TILE), p_new, alpha_new)
        flush(pending)
        make_pending()
        return j
    flush(lax.fori_loop(0, i, body, i))

    return acc_sc[:, 0:LANE] / acc_sc[:, LANE:2 * LANE]


_ATT_SCRATCH = [pltpu.VMEM((ATT_TILE, LANE), F32), pltpu.VMEM((ATT_TILE, 2 * LANE), F32),
                pltpu.VMEM((ATT_TILE, ATT_TILE), BF16), pltpu.VMEM((ATT_TILE, LANE), F32),
                pltpu.VMEM((ATT_TILE, ATT_TILE), BF16), pltpu.VMEM((ATT_TILE, LANE), F32)]


def _mla_attn_kernel(q_ref, k_ref, v_ref, o_ref, *scratch):
    mask = lambda r, c: jnp.right_shift(c, CHUNK_SHIFT) <= jnp.right_shift(r, CHUNK_SHIFT)
    out = _flash_tile(pl.program_id(1), q_ref, k_ref, v_ref, *scratch, diag_mask=mask)
    o_ref[...] = out.astype(o_ref.dtype)


def _mla_attn(q, k, v):
    return pl.pallas_call(
        _mla_attn_kernel,
        grid=(MLA_HEADS, N_AT),
        in_specs=[
            pl.BlockSpec((1, ATT_TILE, QK_W), lambda h, i: (h, i, 0)),
            pl.BlockSpec((1, L_BUF, QK_W), lambda h, i: (h, 0, 0)),
            pl.BlockSpec((1, L_BUF, MLA_V), lambda h, i: (h, 0, 0)),
        ],
        out_specs=pl.BlockSpec((ATT_TILE, MLA_V), lambda h, i: (i, h)),
        out_shape=jax.ShapeDtypeStruct((L_BUF, MLA_HEADS * MLA_V), BF16),
        scratch_shapes=_ATT_SCRATCH,
        compiler_params=_cparams(("parallel", "arbitrary")),
        name="mla_attn",
    )(q, k, v)


def _fox_attn_kernel(q_ref, k_ref, v_ref, c_ref, gate_ref, o_ref, *scratch):
    lane = lax.broadcasted_iota(jnp.int32, (ATT_TILE, LANE), 1)
    cq = jnp.sum(jnp.where(lane == pl.program_id(0), c_ref[...], 0.0), axis=-1, keepdims=True)
    out = _flash_tile(pl.program_id(1), q_ref, k_ref, v_ref, *scratch,
                      diag_mask=lambda r, c: c <= r, row_shift=cq)
    o_ref[...] = (out * jax.nn.sigmoid(gate_ref[...])).astype(o_ref.dtype)


def _fox_attn(q, k, v, c, proj):
    return pl.pallas_call(
        _fox_attn_kernel,
        grid=(FOX_HEADS, N_AT),
        in_specs=[
            pl.BlockSpec((1, ATT_TILE, QK_W), lambda h, i: (h, i, 0)),
            pl.BlockSpec((1, L_BUF, QK_W), lambda h, i: (h, 0, 0)),
            pl.BlockSpec((1, L_BUF, FOX_HD), lambda h, i: (0, 0, h)),
            pl.BlockSpec((ATT_TILE, LANE), lambda h, i: (i, 0)),
            pl.BlockSpec((ATT_TILE, FOX_HD), lambda h, i: (i, GATE_COL0 // FOX_HD + h)),
        ],
        out_specs=pl.BlockSpec((ATT_TILE, FOX_HD), lambda h, i: (i, h)),
        out_shape=jax.ShapeDtypeStruct((L_BUF, FOX_W), BF16),
        scratch_shapes=_ATT_SCRATCH,
        compiler_params=_cparams(("parallel", "arbitrary")),
        name="fox_attn",
    )(q, k, v, c, proj)


def _mix_out_kernel(a_ref, b_ref, wa_ref, wb_ref, h_ref, g_ref, o_ref):
    mix = (jnp.dot(a_ref[...], wa_ref[...], preferred_element_type=F32)
           + jnp.dot(b_ref[...], wb_ref[...], preferred_element_type=F32))
    out = h_ref[...] + _rms(mix, g_ref[...])
    row = pl.program_id(0) * TM_OUT + lax.broadcasted_iota(jnp.int32, (TM_OUT, 1), 0)
    o_ref[...] = jnp.where(row >= ROW_PAD, out, 0.0)


def _mix_out(l, a, b, w_o, h, g):
    return pl.pallas_call(
        _mix_out_kernel,
        grid=(L_BUF // TM_OUT,),
        in_specs=[
            pl.BlockSpec((TM_OUT, MLA_HEADS * MLA_V), lambda i: (i, 0)),
            pl.BlockSpec((TM_OUT, FOX_W), lambda i: (i, 0)),
            pl.BlockSpec((None, MLA_HEADS * MLA_V, D_MODEL), lambda i: (l, 0, 0)),
            pl.BlockSpec((None, FOX_W, D_MODEL), lambda i: (l, 1, 0)),
            pl.BlockSpec((TM_OUT, D_MODEL), lambda i: (i, 0)),
            pl.BlockSpec((1, D_MODEL), lambda i: (0, 0)),
        ],
        out_specs=pl.BlockSpec((TM_OUT, D_MODEL), lambda i: (i, 0)),
        out_shape=jax.ShapeDtypeStruct((L_BUF, D_MODEL), F32),
        compiler_params=_cparams(("parallel",)),
        name="mix_out",
    )(a, b, w_o, w_o, h, g)


def _gelu_tanh(x):
    return 0.5 * x * (1.0 + jnp.tanh(np.sqrt(2.0 / np.pi).astype(np.float32) * (x + 0.044715 * (x * x * x))))


def _ffn_kernel(h_ref, halo_ref, gpre_ref, wg_ref, wu_ref, cwg_ref, cwu_ref, cbg_ref, cbu_ref,
                wd_ref, gpost_ref, *rest):
    c = pl.program_id(1)
    if len(rest) == 4:
        o_ref, xn_ref, ug_ref, uu_ref = rest
    else:
        nu_ref, nd_ref, o_ref, nu_out, nd_out, xn_ref, ug_ref, uu_ref = rest

        @pl.when(pl.program_id(0) * pl.num_programs(1) + c < CVT_STEPS)
        def _():
            nu_out[...] = nu_ref[...].astype(BF16)
            nd_out[...] = nd_ref[...].astype(BF16)

    @pl.when(c == 0)
    def _():
        xn_ref[0:HALO, :] = _rms(halo_ref[...], gpre_ref[...]).astype(BF16)

        def body(r, carry):
            src = pl.ds(pl.multiple_of(r * 64, 64), 64)
            dst = pl.ds(pl.multiple_of(HALO + r * 64, 16), 64)
            xn_ref[dst, :] = _rms(h_ref[src, :], gpre_ref[...]).astype(BF16)
            return carry
        lax.fori_loop(0, TM_FFN // 64, body, 0)
        o_ref[...] = jnp.zeros_like(o_ref)

    xn = xn_ref[...]
    ug_ref[...] = jnp.dot(xn, wg_ref[...], preferred_element_type=F32)
    uu_ref[...] = jnp.dot(xn, wu_ref[...], preferred_element_type=F32)

    def conv(u_ref, w_ref, b_ref):
        acc = b_ref[...] + w_ref[CONV_K - 1:CONV_K, :] * u_ref[HALO:HALO + TM_FFN, :]
        for t in range(1, CONV_K):
            acc = acc + w_ref[CONV_K - 1 - t:CONV_K - t, :] * u_ref[HALO - t:HALO - t + TM_FFN, :]
        return acc

    act = _gelu_tanh(conv(ug_ref, cwg_ref, cbg_ref)) * conv(uu_ref, cwu_ref, cbu_ref)
    o_ref[...] += jnp.dot(act.astype(BF16), wd_ref[...], preferred_element_type=F32)

    @pl.when(c == pl.num_programs(1) - 1)
    def _():
        out = h_ref[...] + _rms(o_ref[...], gpost_ref[...])
        row = pl.program_id(0) * TM_FFN + lax.broadcasted_iota(jnp.int32, (TM_FFN, 1), 0)
        o_ref[...] = jnp.where(row >= ROW_PAD, out, 0.0)


def _ffn(h, gpre, w_up, w_conv, b_conv, w_down, gpost, next_weights=None):
    n_fc = D_FF // FC_FFN
    halo_blocks = TM_FFN // HALO
    in_specs = [
        pl.BlockSpec((TM_FFN, D_MODEL), lambda i, c: (i, 0)),
        pl.BlockSpec((HALO, D_MODEL), lambda i, c: (jnp.maximum(i * halo_blocks - 1, 0), 0)),
        pl.BlockSpec((1, D_MODEL), lambda i, c: (0, 0)),
        pl.BlockSpec((D_MODEL, FC_FFN), lambda i, c: (0, c)),
        pl.BlockSpec((D_MODEL, FC_FFN), lambda i, c: (0, n_fc + c)),
        pl.BlockSpec((CONV_K, FC_FFN), lambda i, c: (0, c)),
        pl.BlockSpec((CONV_K, FC_FFN), lambda i, c: (0, n_fc + c)),
        pl.BlockSpec((1, FC_FFN), lambda i, c: (0, c)),
        pl.BlockSpec((1, FC_FFN), lambda i, c: (0, n_fc + c)),
        pl.BlockSpec((FC_FFN, D_MODEL), lambda i, c: (c, 0)),
        pl.BlockSpec((1, D_MODEL), lambda i, c: (0, 0)),
    ]
    operands = [h, h, gpre, w_up, w_up, w_conv, w_conv, b_conv, b_conv, w_down, gpost]
    out_specs = [pl.BlockSpec((TM_FFN, D_MODEL), lambda i, c: (i, 0))]
    out_shape = [jax.ShapeDtypeStruct((L_BUF, D_MODEL), F32)]
    if next_weights is not None:
        w_up_f32, w_down_f32, nl = next_weights
        blk = lambda i, c: jnp.minimum(i * n_fc + c, CVT_STEPS - 1)
        up_cols, down_rows = 2 * D_FF // CVT_STEPS, D_FF // CVT_STEPS
        in_specs += [pl.BlockSpec((None, D_MODEL, up_cols), lambda i, c: (nl, 0, blk(i, c))),
                     pl.BlockSpec((None, down_rows, D_MODEL), lambda i, c: (nl, blk(i, c), 0))]
        operands += [w_up_f32, w_down_f32]
        out_specs += [pl.BlockSpec((D_MODEL, up_cols), lambda i, c: (0, blk(i, c))),
                      pl.BlockSpec((down_rows, D_MODEL), lambda i, c: (blk(i, c), 0))]
        out_shape += [jax.ShapeDtypeStruct((D_MODEL, 2 * D_FF), BF16),
                      jax.ShapeDtypeStruct((D_FF, D_MODEL), BF16)]
    return pl.pallas_call(
        _ffn_kernel,
        grid=(L_BUF // TM_FFN, n_fc),
        in_specs=in_specs,
        out_specs=out_specs,
        out_shape=out_shape,
        scratch_shapes=[pltpu.VMEM((HALO + TM_FFN, D_MODEL), BF16),
                        pltpu.VMEM((HALO + TM_FFN, FC_FFN), F32),
                        pltpu.VMEM((HALO + TM_FFN, FC_FFN), F32)],
        compiler_params=_cparams(("arbitrary", "arbitrary"), VMEM_LIMIT_FFN),
        name="conv_ffn",
    )(*operands)


def _rotate_half_cols(w):
    half = w.shape[-1] // 2
    return jnp.concatenate([-w[..., half:], w[..., :half]], axis=-1)


def _pad_cols(w, width):
    return jnp.pad(w, [(0, 0)] * (w.ndim - 1) + [(0, width - w.shape[-1])])


def _pack_sources():
    o = np.cumsum([0, MLA_Q_LORA, MLA_KV_LORA, MLA_ROPE, FOX_W, FOX_W, FOX_W, FOX_W, FOX_HEADS])
    c_q, c_kv, k_rope, fq, fk, fv, fg, ff = [(int(o[n]), int(o[n + 1] - o[n])) for n in range(8)]
    per_tile = PACK_TILE // PIECE
    tiles, tail_tile = [], None
    for start, width in (c_q, c_kv, fq, fk, fg, (None, TAIL_W), fv):
        if start is None:
            tail_tile = len(tiles)
            tiles += [[k_rope[0] // PIECE] * per_tile, [ff[0] // PIECE] * per_tile]
            continue
        assert start % PIECE == 0 and width % PACK_TILE == 0, (start, width)
        tiles += [[(start + t * PACK_TILE) // PIECE + q for q in range(per_tile)]
                  for t in range(width // PACK_TILE)]
    assert len(tiles) * PACK_TILE == IN_PACKED and MLA_ROPE == PIECE and TAIL_W == 2 * PACK_TILE
    return tiles, tail_tile


_PACK_SRC, _PACK_TAIL = _pack_sources()


def _pack_w_in_kernel(a_ref, b_ref, c_ref, d_ref, o_ref):
    j = pl.program_id(1)
    zeros = lambda n: jnp.zeros((n, D_MODEL), F32)

    def emit(pieces):
        o_ref[...] = jnp.concatenate(pieces, axis=0).T.astype(BF16)

    @pl.when(j == _PACK_TAIL)
    def _():
        half = MLA_ROPE // 2
        emit([a_ref[...], zeros(LANE - MLA_ROPE), -a_ref[half:, :], a_ref[:half, :], zeros(LANE - MLA_ROPE)])

    @pl.when(j == _PACK_TAIL + 1)
    def _():
        emit([a_ref[:FOX_HEADS, :], zeros(PACK_TILE - FOX_HEADS)])

    @pl.when((j != _PACK_TAIL) & (j != _PACK_TAIL + 1))
    def _():
        emit([a_ref[...], b_ref[...], c_ref[...], d_ref[...]])


def _pack_w_in(w):
    def piece_spec(q):
        def index(l, j):
            blk = jnp.int32(_PACK_SRC[0][q])
            for t in range(1, len(_PACK_SRC)):
                blk = jnp.where(j == t, _PACK_SRC[t][q], blk)
            return l, blk, 0
        return pl.BlockSpec((None, PIECE, D_MODEL), index)

    wt = jnp.swapaxes(w, 1, 2)
    n_pieces = PACK_TILE // PIECE
    return pl.pallas_call(
        _pack_w_in_kernel,
        grid=(DEPTH, IN_PACKED // PACK_TILE),
        in_specs=[piece_spec(q) for q in range(n_pieces)],
        out_specs=pl.BlockSpec((None, D_MODEL, PACK_TILE), lambda l, j: (l, 0, j)),
        out_shape=jax.ShapeDtypeStruct((DEPTH, D_MODEL, IN_PACKED), BF16),
        compiler_params=_cparams(("parallel", "arbitrary")),
        name="pack_w_in",
    )(*([wt] * n_pieces))


def _pack_w_q_up(w):
    w = w.astype(BF16).reshape(DEPTH, MLA_Q_LORA, MLA_HEADS, MLA_NOPE + MLA_ROPE)
    nope, rope = w[..., :MLA_NOPE], w[..., MLA_NOPE:]
    packed = jnp.concatenate([nope, _pad_cols(rope, LANE), _pad_cols(_rotate_half_cols(rope), LANE)], axis=-1)
    return packed.reshape(DEPTH, MLA_Q_LORA, MLA_HEADS * Q_HEAD_W)


def _rope_tables():
    pos = jnp.maximum(jnp.arange(L_BUF, dtype=jnp.int32) - ROW_PAD, 0).astype(F32)
    half = MLA_ROPE // 2
    inv_freq = ROPE_THETA ** (-jnp.arange(half, dtype=F32) / half)
    ang = pos[:, None] * inv_freq[None, :]
    zeros = jnp.zeros((L_BUF, LANE - MLA_ROPE), F32)
    cos, sin = jnp.cos(ang), jnp.sin(ang)
    return (jnp.concatenate([cos, cos, zeros], axis=1), jnp.concatenate([sin, sin, zeros], axis=1))


def kernel(x, meta_tokens, ln_mix_pre, w_in, b_forget, g_q_latent, g_kv_latent, w_q_up, w_kv_up,
           g_fox_q, g_fox_k, w_out, ln_mix_post, ln_ffn_pre, w_ffn_up, w_ffn_conv, b_ffn_conv,
           w_ffn_down, ln_ffn_post):
    assert x.shape == (1, SEQ, D_MODEL), x.shape
    h = jnp.concatenate([jnp.zeros((ROW_PAD, D_MODEL), x.dtype), meta_tokens.astype(x.dtype), x[0]], axis=0)
    cos_t, sin_t = _rope_tables()
    tri = (lax.broadcasted_iota(jnp.int32, (Q_TILE, Q_TILE), 0)
           >= lax.broadcasted_iota(jnp.int32, (Q_TILE, Q_TILE), 1)).astype(BF16)
    row2d = lambda v: v.reshape(1, -1).astype(F32)
    w_in_p, w_q_p = _pack_w_in(w_in), _pack_w_q_up(w_q_up)
    w_kv_b, w_o_b = w_kv_up.astype(BF16), w_out.astype(BF16)
    assert MLA_HEADS * MLA_V == FOX_W

    for l in range(DEPTH):
        first_ffn = (w_ffn_up, w_ffn_down, 0) if l == 0 else None
        proj, vf, *converted = _proj_in(l, h, row2d(ln_mix_pre[l]), w_in_p, first_ffn)
        if converted:
            w_up_b, w_down_b = converted
        q, k, v, qf, kf, c = _prep(
            l, proj, cos_t, sin_t, row2d(g_q_latent[l]), row2d(g_kv_latent[l]),
            row2d(g_fox_q[l]), row2d(g_fox_k[l]), _pad_cols(row2d(b_forget[l]), LANE),
            w_q_p, w_kv_b, tri)
        a = _mla_attn(q, k, v)
        b = _fox_attn(qf, kf, vf, c, proj)
        h = _mix_out(l, a, b, w_o_b, h, row2d(ln_mix_post[l]))
        next_weights = (w_ffn_up, w_ffn_down, l + 1) if l + 1 < DEPTH else None
        h, *converted = _ffn(h, row2d(ln_ffn_pre[l]), w_up_b, w_ffn_conv[l].astype(F32),
                             row2d(b_ffn_conv[l]), w_down_b, row2d(ln_ffn_post[l]), next_weights)
        if converted:
            w_up_b, w_down_b = converted

    return h[ROW_PAD + N_META:][None]
```

```python
import jax
import jax.numpy as jnp
import numpy as np
from jax import lax
from jax.experimental import pallas as pl
from jax.experimental.pallas import tpu as pltpu

F32 = jnp.float32
BF16 = jnp.bfloat16

D_MODEL = 2048
SEQ = 8192
DEPTH = 4
CHUNK = 64
CHUNK_SHIFT = 6
N_META = 16
MLA_HEADS = 8
MLA_Q_LORA = 512
MLA_KV_LORA = 512
MLA_NOPE = 128
MLA_ROPE = 64
MLA_V = 128
ROPE_THETA = 10000.0
FOX_HEADS = 8
FOX_HD = 128
FOX_W = FOX_HEADS * FOX_HD
D_FF = 5632
CONV_K = 3
EPS = 1e-6
NEG = -1e30

LANE = 128
Q_TILE = 256
ROW_PAD = Q_TILE - N_META
L_BUF = ROW_PAD + N_META + SEQ
ATT_TILE = 768
ROW_BLK = 256
N_AT = L_BUF // ATT_TILE

LOG2E = float(np.log2(np.e))
MLA_SCALE = (MLA_NOPE + MLA_ROPE) ** -0.5 * LOG2E
FOX_SCALE = FOX_HD ** -0.5 * LOG2E

TAIL_W = 4 * LANE
IN_MAIN = MLA_Q_LORA + MLA_KV_LORA + 3 * FOX_W + TAIL_W
IN_PACKED = IN_MAIN + FOX_W
GATE_COL0 = MLA_Q_LORA + MLA_KV_LORA + 2 * FOX_W
Q_HEAD_W = 3 * LANE
QK_W = 2 * LANE

TM_PROJ = 1408
TN_PROJ = 512
TM_OUT = 384
TM_FFN = 768
FC_FFN = 512
HALO = 16
PACK_TILE = 512
PIECE = 64
CVT_STEPS = 88
CVT_STEPS_PROJ = 44

VMEM_LIMIT = 56 * 1024 * 1024
VMEM_LIMIT_FFN = 58 * 1024 * 1024


def _rms(x, g):
    ms = jnp.mean(x * x, axis=-1, keepdims=True)
    return x * lax.rsqrt(ms + EPS) * g


def _split3(x):
    hi = x.astype(BF16).astype(F32)
    r = x - hi
    mid = r.astype(BF16).astype(F32)
    return hi, mid, r - mid


def _cparams(sem, vmem_limit=VMEM_LIMIT):
    return pltpu.CompilerParams(dimension_semantics=sem, vmem_limit_bytes=vmem_limit)


def _proj_in_kernel(x_ref, g_ref, w_ref, *rest):
    j = pl.program_id(1)
    if len(rest) == 3:
        o_ref, ov_ref, xn_ref = rest
    else:
        nu_ref, nd_ref, o_ref, ov_ref, nu_out, nd_out, xn_ref = rest

        @pl.when(pl.program_id(0) * pl.num_programs(1) + j < CVT_STEPS_PROJ)
        def _():
            nu_out[...] = nu_ref[...].astype(BF16)
            nd_out[...] = nd_ref[...].astype(BF16)

    @pl.when(j == 0)
    def _():
        def body(r, carry):
            rows = pl.ds(pl.multiple_of(r * 64, 64), 64)
            xn_ref[rows, :] = _rms(x_ref[rows, :], g_ref[...]).astype(BF16)
            return carry
        lax.fori_loop(0, TM_PROJ // 64, body, 0)

    res = jnp.dot(xn_ref[...], w_ref[...], preferred_element_type=F32)

    @pl.when(j < IN_MAIN // TN_PROJ)
    def _():
        o_ref[...] = res

    @pl.when(j >= IN_MAIN // TN_PROJ)
    def _():
        ov_ref[...] = res.astype(BF16)


def _proj_in(l, h, g, w, ffn_weights=None):
    n_main = IN_MAIN // TN_PROJ
    n_j = IN_PACKED // TN_PROJ
    in_specs = [
        pl.BlockSpec((TM_PROJ, D_MODEL), lambda i, j: (i, 0)),
        pl.BlockSpec((1, D_MODEL), lambda i, j: (0, 0)),
        pl.BlockSpec((None, D_MODEL, TN_PROJ), lambda i, j: (l, 0, j)),
    ]
    operands = [h, g, w]
    out_specs = [
        pl.BlockSpec((TM_PROJ, TN_PROJ), lambda i, j: (i, jnp.minimum(j, n_main - 1))),
        pl.BlockSpec((None, TM_PROJ, TN_PROJ), lambda i, j: (0, i, jnp.maximum(j - n_main, 0))),
    ]
    out_shape = [jax.ShapeDtypeStruct((L_BUF, IN_MAIN), F32),
                 jax.ShapeDtypeStruct((1, L_BUF, FOX_W), BF16)]
    if ffn_weights is not None:
        w_up_f32, w_down_f32, fl = ffn_weights
        blk = lambda i, j: jnp.minimum(i * n_j + j, CVT_STEPS_PROJ - 1)
        up_cols, down_rows = 2 * D_FF // CVT_STEPS_PROJ, D_FF // CVT_STEPS_PROJ
        in_specs += [pl.BlockSpec((None, D_MODEL, up_cols), lambda i, j: (fl, 0, blk(i, j))),
                     pl.BlockSpec((None, down_rows, D_MODEL), lambda i, j: (fl, blk(i, j), 0))]
        operands += [w_up_f32, w_down_f32]
        out_specs += [pl.BlockSpec((D_MODEL, up_cols), lambda i, j: (0, blk(i, j))),
                      pl.BlockSpec((down_rows, D_MODEL), lambda i, j: (blk(i, j), 0))]
        out_shape += [jax.ShapeDtypeStruct((D_MODEL, 2 * D_FF), BF16),
                      jax.ShapeDtypeStruct((D_FF, D_MODEL), BF16)]
    semantics = ("parallel", "arbitrary") if ffn_weights is None else ("arbitrary", "arbitrary")
    return pl.pallas_call(
        _proj_in_kernel,
        grid=(L_BUF // TM_PROJ, n_j),
        in_specs=in_specs,
        out_specs=out_specs,
        out_shape=out_shape,
        scratch_shapes=[pltpu.VMEM((TM_PROJ, D_MODEL), BF16)],
        compiler_params=_cparams(semantics),
        name="proj_in",
    )(*operands)


def _prep_kernel(cq_ref, ckv_ref, fq_ref, fk_ref, tail_ref, cos_ref, sin_ref,
                 gql_ref, gkvl_ref, gfq_ref, gfk_ref, bf_ref, wq_ref, wkv_ref, tri_ref,
                 q_ref, k_ref, v_ref, qf_ref, kf_ref, c_ref, carry_ref):
    i = pl.program_id(0)
    lane = lax.broadcasted_iota(jnp.int32, (Q_TILE, LANE), 1)
    is_pad = (i * Q_TILE + lax.broadcasted_iota(jnp.int32, (Q_TILE, 1), 0)) < ROW_PAD
    cos = cos_ref[...]
    sin = sin_ref[...]
    q_flag = jnp.where(lane == MLA_ROPE, 1.0, 0.0)
    k_flag = jnp.where((lane == MLA_ROPE) & is_pad, NEG, 0.0)
    kr = (tail_ref[:, 0:LANE] * cos + tail_ref[:, LANE:2 * LANE] * sin + k_flag).astype(BF16)
    cqn = _rms(cq_ref[...], gql_ref[...]).astype(BF16)
    ckvn = _rms(ckv_ref[...], gkvl_ref[...]).astype(BF16)
    for h in range(MLA_HEADS):
        qh = jnp.dot(cqn, wq_ref[:, Q_HEAD_W * h:Q_HEAD_W * (h + 1)], preferred_element_type=F32)
        qr = qh[:, LANE:2 * LANE] * cos + qh[:, 2 * LANE:3 * LANE] * sin
        q_ref[h, :, 0:LANE] = (qh[:, 0:LANE] * MLA_SCALE).astype(BF16)
        q_ref[h, :, LANE:QK_W] = (qr * MLA_SCALE + q_flag).astype(BF16)
        kvh = jnp.dot(ckvn, wkv_ref[:, 2 * LANE * h:2 * LANE * (h + 1)], preferred_element_type=F32)
        k_ref[h, :, 0:LANE] = kvh[:, 0:LANE].astype(BF16)
        k_ref[h, :, LANE:QK_W] = kr
        v_ref[h] = kvh[:, LANE:2 * LANE].astype(BF16)

    z = tail_ref[:, 2 * LANE:3 * LANE] + bf_ref[...]
    logf = (jnp.minimum(z, 0.0) - jnp.log(1.0 + jnp.exp(-jnp.abs(z)))) * LOG2E
    hi, mid, lo = _split3(logf)
    tri = tri_ref[...]
    cs = (jnp.dot(tri, hi.astype(BF16), preferred_element_type=F32)
          + jnp.dot(tri, mid.astype(BF16), preferred_element_type=F32)
          + jnp.dot(tri, lo.astype(BF16), preferred_element_type=F32))

    @pl.when(i == 0)
    def _():
        carry_ref[...] = jnp.zeros_like(carry_ref)

    c = cs + carry_ref[0:1, :]
    c_ref[...] = c
    carry_ref[...] = jnp.broadcast_to(c[Q_TILE - 1:Q_TILE, :], carry_ref.shape)

    qf_ext = jnp.where(lane < 3, -1.0, 0.0).astype(BF16)
    for h in range(FOX_HEADS):
        cols = slice(FOX_HD * h, FOX_HD * (h + 1))
        qf_ref[h, :, 0:FOX_HD] = (_rms(fq_ref[:, cols], gfq_ref[...]) * FOX_SCALE).astype(BF16)
        qf_ref[h, :, FOX_HD:QK_W] = qf_ext
        kf_ref[h, :, 0:FOX_HD] = _rms(fk_ref[:, cols], gfk_ref[...]).astype(BF16)
        ck = jnp.sum(jnp.where(lane == h, c, 0.0), axis=-1, keepdims=True)
        ck_hi, ck_mid, ck_lo = _split3(jnp.where(is_pad, -NEG, ck))
        ext = jnp.where(lane == 0, ck_hi, jnp.where(lane == 1, ck_mid, jnp.where(lane == 2, ck_lo, 0.0)))
        kf_ref[h, :, FOX_HD:QK_W] = ext.astype(BF16)


def _prep(l, proj, cos_t, sin_t, gql, gkvl, gfq, gfk, bf_pad, wq, wkv, tri):
    tm = Q_TILE
    row = lambda i: (i, 0)
    const = lambda i: (0, 0)
    head_out = lambda w: pl.BlockSpec((MLA_HEADS, tm, w), lambda i: (0, i, 0))
    return pl.pallas_call(
        _prep_kernel,
        grid=(L_BUF // tm,),
        in_specs=[
            pl.BlockSpec((tm, MLA_Q_LORA), lambda i: (i, 0)),
            pl.BlockSpec((tm, MLA_KV_LORA), lambda i: (i, 1)),
            pl.BlockSpec((tm, FOX_W), lambda i: (i, 1)),
            pl.BlockSpec((tm, FOX_W), lambda i: (i, 2)),
            pl.BlockSpec((tm, TAIL_W), lambda i: (i, IN_MAIN // TAIL_W - 1)),
            pl.BlockSpec((tm, LANE), row),
            pl.BlockSpec((tm, LANE), row),
            pl.BlockSpec((1, MLA_Q_LORA), const),
            pl.BlockSpec((1, MLA_KV_LORA), const),
            pl.BlockSpec((1, FOX_HD), const),
            pl.BlockSpec((1, FOX_HD), const),
            pl.BlockSpec((1, LANE), const),
            pl.BlockSpec((None, MLA_Q_LORA, MLA_HEADS * Q_HEAD_W), lambda i: (l, 0, 0)),
            pl.BlockSpec((None, MLA_KV_LORA, MLA_HEADS * 2 * LANE), lambda i: (l, 0, 0)),
            pl.BlockSpec((tm, tm), const),
        ],
        out_specs=[
            head_out(QK_W), head_out(QK_W), head_out(MLA_V),
            head_out(QK_W), head_out(QK_W),
            pl.BlockSpec((tm, LANE), row),
        ],
        out_shape=[
            jax.ShapeDtypeStruct((MLA_HEADS, L_BUF, QK_W), BF16),
            jax.ShapeDtypeStruct((MLA_HEADS, L_BUF, QK_W), BF16),
            jax.ShapeDtypeStruct((MLA_HEADS, L_BUF, MLA_V), BF16),
            jax.ShapeDtypeStruct((FOX_HEADS, L_BUF, QK_W), BF16),
            jax.ShapeDtypeStruct((FOX_HEADS, L_BUF, QK_W), BF16),
            jax.ShapeDtypeStruct((L_BUF, LANE), F32),
        ],
        scratch_shapes=[pltpu.VMEM((8, LANE), F32)],
        compiler_params=_cparams(("arbitrary",)),
        name="attn_prep",
    )(proj, proj, proj, proj, proj, cos_t, sin_t, gql, gkvl, gfq, gfk, bf_pad, wq, wkv, tri)


def _scores(q, k):
    return lax.dot_general(q, k, (((1,), (1,)), ((), ())), preferred_element_type=F32)


def _flash_tile(i, q_ref, k_ref, v_ref, m_sc, acc_sc, p_sc, alpha_sc, p_new, alpha_new,
                diag_mask, row_shift=None):
    m_sc[...] = jnp.full_like(m_sc, NEG)
    acc_sc[...] = jnp.zeros_like(acc_sc)
    row_blocks = [slice(rb * ROW_BLK, (rb + 1) * ROW_BLK) for rb in range(ATT_TILE // ROW_BLK)]

    def chunk_start(j):
        return pl.multiple_of(j * ATT_TILE, ATT_TILE)

    def flush(j):
        v = v_ref[0, pl.ds(chunk_start(j), ATT_TILE), :]
        v_ones = jnp.concatenate([v, jnp.ones_like(v)], axis=1)
        for rows in row_blocks:
            pv = acc_sc[rows, :] * jnp.tile(alpha_sc[rows, :], (1, 2))
            for kc in range(0, ATT_TILE, 2 * LANE):
                pv = pv + jnp.dot(p_sc[rows, kc:kc + 2 * LANE], v_ones[kc:kc + 2 * LANE, :],
                                  preferred_element_type=F32)
            acc_sc[rows, :] = pv

    def scores(j, rows, ncols):
        return _scores(q_ref[0, rows, :], k_ref[0, pl.ds(chunk_start(j), ncols), :])

    def softmax_rows(rows, s, p_dst, alpha_dst):
        ncols = s.shape[1]
        m_prev = m_sc[rows, :]
        m_chunk = jnp.max(s, axis=-1, keepdims=True)
        if row_shift is not None:
            m_chunk = m_chunk + row_shift[rows, :]
        m_new = jnp.maximum(m_prev, m_chunk)
        alpha_dst[rows, :] = jnp.exp2(m_prev - m_new)
        sub = m_new if row_shift is None else m_new - row_shift[rows, :]
        p_dst[rows, 0:ncols] = jnp.exp2(s - jnp.tile(sub, (1, ncols // LANE))).astype(BF16)
        m_sc[rows, :] = m_new

    def make_pending():
        p_sc[...] = p_new[...]
        alpha_sc[...] = alpha_new[...]

    for rows in row_blocks:
        ncols = rows.stop
        s = scores(i, rows, ncols)
        r = lax.broadcasted_iota(jnp.int32, s.shape, 0) + rows.start
        c = lax.broadcasted_iota(jnp.int32, s.shape, 1)
        if ncols < ATT_TILE:
            p_sc[rows, ncols:] = jnp.zeros((ROW_BLK, ATT_TILE - ncols), BF16)
        softmax_rows(rows, jnp.where(diag_mask(r, c), s, NEG), p_sc, alpha_sc)

    def body(j, pending):
        for rows in row_blocks:
            softmax_rows(rows, scores(j, rows, ATT_TILE), p_new, alpha_new)
        flush(pending)
        make_pending()
        return j
    flush(lax.fori_loop(0, i, body, i))

    return acc_sc[:, 0:LANE] / acc_sc[:, LANE:2 * LANE]


_ATT_SCRATCH = [pltpu.VMEM((ATT_TILE, LANE), F32), pltpu.VMEM((ATT_TILE, 2 * LANE), F32),
                pltpu.VMEM((ATT_TILE, ATT_TILE), BF16), pltpu.VMEM((ATT_TILE, LANE), F32),
                pltpu.VMEM((ATT_TILE, ATT_TILE), BF16), pltpu.VMEM((ATT_TILE, LANE), F32)]


def _mla_attn_kernel(q_ref, k_ref, v_ref, o_ref, *scratch):
    mask = lambda r, c: jnp.right_shift(c, CHUNK_SHIFT) <= jnp.right_shift(r, CHUNK_SHIFT)
    out = _flash_tile(pl.program_id(1), q_ref, k_ref, v_ref, *scratch, diag_mask=mask)
    o_ref[...] = out.astype(o_ref.dtype)


def _mla_attn(q, k, v):
    return pl.pallas_call(
        _mla_attn_kernel,
        grid=(MLA_HEADS, N_AT),
        in_specs=[
            pl.BlockSpec((1, ATT_TILE, QK_W), lambda h, i: (h, i, 0)),
            pl.BlockSpec((1, L_BUF, QK_W), lambda h, i: (h, 0, 0)),
            pl.BlockSpec((1, L_BUF, MLA_V), lambda h, i: (h, 0, 0)),
        ],
        out_specs=pl.BlockSpec((ATT_TILE, MLA_V), lambda h, i: (i, h)),
        out_shape=jax.ShapeDtypeStruct((L_BUF, MLA_HEADS * MLA_V), BF16),
        scratch_shapes=_ATT_SCRATCH,
        compiler_params=_cparams(("parallel", "arbitrary")),
        name="mla_attn",
    )(q, k, v)


def _fox_attn_kernel(q_ref, k_ref, v_ref, c_ref, gate_ref, o_ref, *scratch):
    lane = lax.broadcasted_iota(jnp.int32, (ATT_TILE, LANE), 1)
    cq = jnp.sum(jnp.where(lane == pl.program_id(0), c_ref[...], 0.0), axis=-1, keepdims=True)
    out = _flash_tile(pl.program_id(1), q_ref, k_ref, v_ref, *scratch,
                      diag_mask=lambda r, c: c <= r, row_shift=cq)
    o_ref[...] = (out * jax.nn.sigmoid(gate_ref[...])).astype(o_ref.dtype)


def _fox_attn(q, k, v, c, proj):
    return pl.pallas_call(
        _fox_attn_kernel,
        grid=(FOX_HEADS, N_AT),
        in_specs=[
            pl.BlockSpec((1, ATT_TILE, QK_W), lambda h, i: (h, i, 0)),
            pl.BlockSpec((1, L_BUF, QK_W), lambda h, i: (h, 0, 0)),
            pl.BlockSpec((1, L_BUF, FOX_HD), lambda h, i: (0, 0, h)),
            pl.BlockSpec((ATT_TILE, LANE), lambda h, i: (i, 0)),
            pl.BlockSpec((ATT_TILE, FOX_HD), lambda h, i: (i, GATE_COL0 // FOX_HD + h)),
        ],
        out_specs=pl.BlockSpec((ATT_TILE, FOX_HD), lambda h, i: (i, h)),
        out_shape=jax.ShapeDtypeStruct((L_BUF, FOX_W), BF16),
        scratch_shapes=_ATT_SCRATCH,
        compiler_params=_cparams(("parallel", "arbitrary")),
        name="fox_attn",
    )(q, k, v, c, proj)


def _mix_out_kernel(a_ref, b_ref, wa_ref, wb_ref, h_ref, g_ref, o_ref):
    mix = (jnp.dot(a_ref[...], wa_ref[...], preferred_element_type=F32)
           + jnp.dot(b_ref[...], wb_ref[...], preferred_element_type=F32))
    out = h_ref[...] + _rms(mix, g_ref[...])
    row = pl.program_id(0) * TM_OUT + lax.broadcasted_iota(jnp.int32, (TM_OUT, 1), 0)
    o_ref[...] = jnp.where(row >= ROW_PAD, out, 0.0)


def _mix_out(l, a, b, w_o, h, g):
    return pl.pallas_call(
        _mix_out_kernel,
        grid=(L_BUF // TM_OUT,),
        in_specs=[
            pl.BlockSpec((TM_OUT, MLA_HEADS * MLA_V), lambda i: (i, 0)),
            pl.BlockSpec((TM_OUT, FOX_W), lambda i: (i, 0)),
            pl.BlockSpec((None, MLA_HEADS * MLA_V, D_MODEL), lambda i: (l, 0, 0)),
            pl.BlockSpec((None, FOX_W, D_MODEL), lambda i: (l, 1, 0)),
            pl.BlockSpec((TM_OUT, D_MODEL), lambda i: (i, 0)),
            pl.BlockSpec((1, D_MODEL), lambda i: (0, 0)),
        ],
        out_specs=pl.BlockSpec((TM_OUT, D_MODEL), lambda i: (i, 0)),
        out_shape=jax.ShapeDtypeStruct((L_BUF, D_MODEL), F32),
        compiler_params=_cparams(("parallel",)),
        name="mix_out",
    )(a, b, w_o, w_o, h, g)


def _gelu_tanh(x):
    return 0.5 * x * (1.0 + jnp.tanh(np.sqrt(2.0 / np.pi).astype(np.float32) * (x + 0.044715 * (x * x * x))))


def _ffn_kernel(h_ref, halo_ref, gpre_ref, wg_ref, wu_ref, cwg_ref, cwu_ref, cbg_ref, cbu_ref,
                wd_ref, gpost_ref, *rest):
    c = pl.program_id(1)
    if len(rest) == 4:
        o_ref, xn_ref, ug_ref, uu_ref = rest
    else:
        nu_ref, nd_ref, o_ref, nu_out, nd_out, xn_ref, ug_ref, uu_ref = rest

        @pl.when(pl.program_id(0) * pl.num_programs(1) + c < CVT_STEPS)
        def _():
            nu_out[...] = nu_ref[...].astype(BF16)
            nd_out[...] = nd_ref[...].astype(BF16)

    @pl.when(c == 0)
    def _():
        xn_ref[0:HALO, :] = _rms(halo_ref[...], gpre_ref[...]).astype(BF16)

        def body(r, carry):
            src = pl.ds(pl.multiple_of(r * 64, 64), 64)
            dst = pl.ds(pl.multiple_of(HALO + r * 64, 16), 64)
            xn_ref[dst, :] = _rms(h_ref[src, :], gpre_ref[...]).astype(BF16)
            return carry
        lax.fori_loop(0, TM_FFN // 64, body, 0)
        o_ref[...] = jnp.zeros_like(o_ref)

    xn = xn_ref[...]
    ug_ref[...] = jnp.dot(xn, wg_ref[...], preferred_element_type=F32)
    uu_ref[...] = jnp.dot(xn, wu_ref[...], preferred_element_type=F32)

    def conv(u_ref, w_ref, b_ref):
        acc = b_ref[...] + w_ref[CONV_K - 1:CONV_K, :] * u_ref[HALO:HALO + TM_FFN, :]
        for t in range(1, CONV_K):
            acc = acc + w_ref[CONV_K - 1 - t:CONV_K - t, :] * u_ref[HALO - t:HALO - t + TM_FFN, :]
        return acc

    act = _gelu_tanh(conv(ug_ref, cwg_ref, cbg_ref)) * conv(uu_ref, cwu_ref, cbu_ref)
    o_ref[...] += jnp.dot(act.astype(BF16), wd_ref[...], preferred_element_type=F32)

    @pl.when(c == pl.num_programs(1) - 1)
    def _():
        out = h_ref[...] + _rms(o_ref[...], gpost_ref[...])
        row = pl.program_id(0) * TM_FFN + lax.broadcasted_iota(jnp.int32, (TM_FFN, 1), 0)
        o_ref[...] = jnp.where(row >= ROW_PAD, out, 0.0)


def _ffn(h, gpre, w_up, w_conv, b_conv, w_down, gpost, next_weights=None):
    n_fc = D_FF // FC_FFN
    halo_blocks = TM_FFN // HALO
    in_specs = [
        pl.BlockSpec((TM_FFN, D_MODEL), lambda i, c: (i, 0)),
        pl.BlockSpec((HALO, D_MODEL), lambda i, c: (jnp.maximum(i * halo_blocks - 1, 0), 0)),
        pl.BlockSpec((1, D_MODEL), lambda i, c: (0, 0)),
        pl.BlockSpec((D_MODEL, FC_FFN), lambda i, c: (0, c)),
        pl.BlockSpec((D_MODEL, FC_FFN), lambda i, c: (0, n_fc + c)),
        pl.BlockSpec((CONV_K, FC_FFN), lambda i, c: (0, c)),
        pl.BlockSpec((CONV_K, FC_FFN), lambda i, c: (0, n_fc + c)),
        pl.BlockSpec((1, FC_FFN), lambda i, c: (0, c)),
        pl.BlockSpec((1, FC_FFN), lambda i, c: (0, n_fc + c)),
        pl.BlockSpec((FC_FFN, D_MODEL), lambda i, c: (c, 0)),
        pl.BlockSpec((1, D_MODEL), lambda i, c: (0, 0)),
    ]
    operands = [h, h, gpre, w_up, w_up, w_conv, w_conv, b_conv, b_conv, w_down, gpost]
    out_specs = [pl.BlockSpec((TM_FFN, D_MODEL), lambda i, c: (i, 0))]
    out_shape = [jax.ShapeDtypeStruct((L_BUF, D_MODEL), F32)]
    if next_weights is not None:
        w_up_f32, w_down_f32, nl = next_weights
        blk = lambda i, c: jnp.minimum(i * n_fc + c, CVT_STEPS - 1)
        up_cols, down_rows = 2 * D_FF // CVT_STEPS, D_FF // CVT_STEPS
        in_specs += [pl.BlockSpec((None, D_MODEL, up_cols), lambda i, c: (nl, 0, blk(i, c))),
                     pl.BlockSpec((None, down_rows, D_MODEL), lambda i, c: (nl, blk(i, c), 0))]
        operands += [w_up_f32, w_down_f32]
        out_specs += [pl.BlockSpec((D_MODEL, up_cols), lambda i, c: (0, blk(i, c))),
                      pl.BlockSpec((down_rows, D_MODEL), lambda i, c: (blk(i, c), 0))]
        out_shape += [jax.ShapeDtypeStruct((D_MODEL, 2 * D_FF), BF16),
                      jax.ShapeDtypeStruct((D_FF, D_MODEL), BF16)]
    return pl.pallas_call(
        _ffn_kernel,
        grid=(L_BUF // TM_FFN, n_fc),
        in_specs=in_specs,
        out_specs=out_specs,
        out_shape=out_shape,
        scratch_shapes=[pltpu.VMEM((HALO + TM_FFN, D_MODEL), BF16),
                        pltpu.VMEM((HALO + TM_FFN, FC_FFN), F32),
                        pltpu.VMEM((HALO + TM_FFN, FC_FFN), F32)],
        compiler_params=_cparams(("arbitrary", "arbitrary"), VMEM_LIMIT_FFN),
        name="conv_ffn",
    )(*operands)


def _rotate_half_cols(w):
    half = w.shape[-1] // 2
    return jnp.concatenate([-w[..., half:], w[..., :half]], axis=-1)


def _pad_cols(w, width):
    return jnp.pad(w, [(0, 0)] * (w.ndim - 1) + [(0, width - w.shape[-1])])


def _pack_sources():
    o = np.cumsum([0, MLA_Q_LORA, MLA_KV_LORA, MLA_ROPE, FOX_W, FOX_W, FOX_W, FOX_W, FOX_HEADS])
    c_q, c_kv, k_rope, fq, fk, fv, fg, ff = [(int(o[n]), int(o[n + 1] - o[n])) for n in range(8)]
    per_tile = PACK_TILE // PIECE
    tiles, tail_tile = [], None
    for start, width in (c_q, c_kv, fq, fk, fg, (None, TAIL_W), fv):
        if start is None:
            tail_tile = len(tiles)
            tiles += [[k_rope[0] // PIECE, ff[0] // PIECE] * (per_tile // 2)]
            continue
        assert start % PIECE == 0 and width % PACK_TILE == 0, (start, width)
        tiles += [[(start + t * PACK_TILE) // PIECE + q for q in range(per_tile)]
                  for t in range(width // PACK_TILE)]
    assert len(tiles) * PACK_TILE == IN_PACKED and MLA_ROPE == PIECE and TAIL_W == PACK_TILE
    return tiles, tail_tile


_PACK_SRC, _PACK_TAIL = _pack_sources()


def _pack_w_in_kernel(*refs):
    *piece_refs, o_ref = refs
    a_ref, b_ref = piece_refs[:2]
    j = pl.program_id(1)
    zeros = lambda n: jnp.zeros((n, D_MODEL), F32)

    def emit(pieces):
        o_ref[...] = jnp.concatenate(pieces, axis=0).T.astype(BF16)

    @pl.when(j == _PACK_TAIL)
    def _():
        half = MLA_ROPE // 2
        emit([a_ref[...], zeros(LANE - MLA_ROPE), -a_ref[half:, :], a_ref[:half, :], zeros(LANE - MLA_ROPE),
              b_ref[:FOX_HEADS, :], zeros(2 * LANE - FOX_HEADS)])

    @pl.when(j != _PACK_TAIL)
    def _():
        emit([r[...] for r in piece_refs])


def _pack_w_in(w):
    def piece_spec(q):
        def index(l, j):
            blk = jnp.int32(_PACK_SRC[0][q])
            for t in range(1, len(_PACK_SRC)):
                blk = jnp.where(j == t, _PACK_SRC[t][q], blk)
            return l, blk, 0
        return pl.BlockSpec((None, PIECE, D_MODEL), index)

    wt = jnp.swapaxes(w, 1, 2)
    n_pieces = PACK_TILE // PIECE
    return pl.pallas_call(
        _pack_w_in_kernel,
        grid=(DEPTH, IN_PACKED // PACK_TILE),
        in_specs=[piece_spec(q) for q in range(n_pieces)],
        out_specs=pl.BlockSpec((None, D_MODEL, PACK_TILE), lambda l, j: (l, 0, j)),
        out_shape=jax.ShapeDtypeStruct((DEPTH, D_MODEL, IN_PACKED), BF16),
        compiler_params=_cparams(("parallel", "arbitrary")),
        name="pack_w_in",
    )(*([wt] * n_pieces))


def _pack_w_q_up(w):
    w = w.astype(BF16).reshape(DEPTH, MLA_Q_LORA, MLA_HEADS, MLA_NOPE + MLA_ROPE)
    nope, rope = w[..., :MLA_NOPE], w[..., MLA_NOPE:]
    packed = jnp.concatenate([nope, _pad_cols(rope, LANE), _pad_cols(_rotate_half_cols(rope), LANE)], axis=-1)
    return packed.reshape(DEPTH, MLA_Q_LORA, MLA_HEADS * Q_HEAD_W)


def _rope_tables():
    pos = jnp.maximum(jnp.arange(L_BUF, dtype=jnp.int32) - ROW_PAD, 0).astype(F32)
    half = MLA_ROPE // 2
    inv_freq = ROPE_THETA ** (-jnp.arange(half, dtype=F32) / half)
    ang = pos[:, None] * inv_freq[None, :]
    zeros = jnp.zeros((L_BUF, LANE - MLA_ROPE), F32)
    cos, sin = jnp.cos(ang), jnp.sin(ang)
    return (jnp.concatenate([cos, cos, zeros], axis=1), jnp.concatenate([sin, sin, zeros], axis=1))


def kernel(x, meta_tokens, ln_mix_pre, w_in, b_forget, g_q_latent, g_kv_latent, w_q_up, w_kv_up,
           g_fox_q, g_fox_k, w_out, ln_mix_post, ln_ffn_pre, w_ffn_up, w_ffn_conv, b_ffn_conv,
           w_ffn_down, ln_ffn_post):
    assert x.shape == (1, SEQ, D_MODEL), x.shape
    h = jnp.concatenate([jnp.zeros((ROW_PAD, D_MODEL), x.dtype), meta_tokens.astype(x.dtype), x[0]], axis=0)
    cos_t, sin_t = _rope_tables()
    tri = (lax.broadcasted_iota(jnp.int32, (Q_TILE, Q_TILE), 0)
           >= lax.broadcasted_iota(jnp.int32, (Q_TILE, Q_TILE), 1)).astype(BF16)
    row2d = lambda v: v.reshape(1, -1).astype(F32)
    w_in_p, w_q_p = _pack_w_in(w_in), _pack_w_q_up(w_q_up)
    w_kv_b, w_o_b = w_kv_up.astype(BF16), w_out.astype(BF16)
    assert MLA_HEADS * MLA_V == FOX_W

    for l in range(DEPTH):
        first_ffn = (w_ffn_up, w_ffn_down, 0) if l == 0 else None
        proj, vf, *converted = _proj_in(l, h, row2d(ln_mix_pre[l]), w_in_p, first_ffn)
        if converted:
            w_up_b, w_down_b = converted
        q, k, v, qf, kf, c = _prep(
            l, proj, cos_t, sin_t, row2d(g_q_latent[l]), row2d(g_kv_latent[l]),
            row2d(g_fox_q[l]), row2d(g_fox_k[l]), _pad_cols(row2d(b_forget[l]), LANE),
            w_q_p, w_kv_b, tri)
        a = _mla_attn(q, k, v)
        b = _fox_attn(qf, kf, vf, c, proj)
        h = _mix_out(l, a, b, w_o_b, h, row2d(ln_mix_post[l]))
        next_weights = (w_ffn_up, w_ffn_down, l + 1) if l + 1 < DEPTH else None
        h, *converted = _ffn(h, row2d(ln_ffn_pre[l]), w_up_b, w_ffn_conv[l].astype(F32),
                             row2d(b_ffn_conv[l]), w_down_b, row2d(ln_ffn_post[l]), next_weights)
        if converted:
            w_up_b, w_down_b = converted

    return h[ROW_PAD + N_META:][None]
```
